```python
import math
import jax
import jax.numpy as jnp
from jax import lax
import numpy as np

D_MODEL = 1024
BATCH = 4
SEQ = 4096
DEPTH = 2
DEC_BATCH = 32
DEC_SEQ = 1
PAST_LEN = 16384
PAGE_SIZE = 128

ATTN_HEADS = 8
ATTN_KV_HEADS = 4
ATTN_GROUP = ATTN_HEADS // ATTN_KV_HEADS
HEAD_DIM = 64
ROT_DIM = HEAD_DIM // 4
ROPE_THETA = 500000.0
Q_BLOCK = 128
ATTN_WIDTH = ATTN_HEADS * 2 * HEAD_DIM
KV_WIDTH = ATTN_KV_HEADS * 2 * HEAD_DIM
SSD_HEADS = 16
SSD_HEAD_DIM = 64
SSD_INNER = SSD_HEADS * SSD_HEAD_DIM
SSD_GROUPS = 2
SSD_STATE = 128
SSD_CONV_DIM = SSD_INNER + 2 * SSD_GROUPS * SSD_STATE
SSD_CHUNK = 64
DN_HEADS = 8
DN_HEAD_K = 128
DN_HEAD_V = 128
DN_KEY_WIDTH = DN_HEADS * DN_HEAD_K
DN_VAL_WIDTH = DN_HEADS * DN_HEAD_V
DN_CONV_DIM = 2 * DN_KEY_WIDTH + DN_VAL_WIDTH
DN_CHUNK = 64
CONV_WIDTH = 4
N_BRANCHES = 3
N_EXPERTS = 64
TOP_K = 8
N_GROUPS = 8
TOPK_GROUPS = 4
EXPERT_FF = 256
SHARED_FF = 256
ROUTE_SCALE = 2.5
MOE_BLOCK = 128
DEEPNORM_ALPHA = (2 * DEPTH) ** 0.25
DEEPNORM_BETA = (8 * DEPTH) ** -0.25
LN_EPS = 1e-5
RMS_EPS = 1e-6
IN_SPLITS = (ATTN_WIDTH, KV_WIDTH, KV_WIDTH, SSD_INNER, SSD_CONV_DIM, SSD_HEADS,
             DN_CONV_DIM, DN_VAL_WIDTH, DN_HEADS, DN_HEADS, N_BRANCHES * D_MODEL)
IN_COLS = sum(IN_SPLITS)
IN_OFFSETS = tuple(int(o) for o in np.cumsum(IN_SPLITS)[:-1])

kernel_name = 'hybrid_diffattn_ssd_gdn_moe_decode_step'


def layer_norm(x, g, b):
    xf = x.astype(jnp.float32)
    mu = jnp.mean(xf, -1, keepdims=True)
    var = jnp.mean(jnp.square(xf - mu), -1, keepdims=True)
    return ((xf - mu) * lax.rsqrt(var + LN_EPS) * g.astype(jnp.float32) + b.astype(jnp.float32)).astype(x.dtype)


def rms_norm(x, g):
    xf = x.astype(jnp.float32)
    return (xf * lax.rsqrt(jnp.mean(xf * xf, -1, keepdims=True) + RMS_EPS) * g.astype(jnp.float32)).astype(x.dtype)


def l2_normalize(x):
    return x * lax.rsqrt(jnp.sum(x * x, -1, keepdims=True) + RMS_EPS)


def rope_partial(x, pos):
    half = ROT_DIM // 2
    inv_freq = ROPE_THETA ** (-jnp.arange(half, dtype=jnp.float32) * 2.0 / ROT_DIM)
    ang = pos.astype(jnp.float32)[:, None] * inv_freq
    cos = jnp.cos(ang)[None, :, None, None, :]
    sin = jnp.sin(ang)[None, :, None, None, :]
    xf = x.astype(jnp.float32)
    x1, x2, rest = xf[..., :half], xf[..., half:ROT_DIM], xf[..., ROT_DIM:]
    return jnp.concatenate([x1 * cos - x2 * sin, x2 * cos + x1 * sin, rest], -1).astype(x.dtype)


def causal_conv_silu(xp, w, b=None):
    t = xp.shape[1] - (CONV_WIDTH - 1)
    y = xp[:, 0:t] * w[0]
    for i in range(1, CONV_WIDTH):
        y = y + xp[:, i:i + t] * w[i]
    if b is not None:
        y = y + b
    return jax.nn.silu(y)


def diff_attention(q, k, v, q_pos, k_pos, lam):
    b, tq = q.shape[:2]
    qg = q.reshape(b, tq, ATTN_KV_HEADS, ATTN_GROUP, 2, HEAD_DIM)
    s = jnp.einsum('bqhgcd,bkhcd->bhgcqk', qg, k, preferred_element_type=jnp.float32) * (HEAD_DIM ** -0.5)
    s = jnp.where(k_pos[None, :] <= q_pos[:, None], s, -jnp.inf)
    p = jax.nn.softmax(s, axis=-1)
    w = p[:, :, :, 0] - lam * p[:, :, :, 1]
    o = jnp.einsum('bhgqk,bkhe->bqhge', w.astype(v.dtype), v)
    return o.reshape(b, tq, ATTN_HEADS, 2 * HEAD_DIM)


def attn_prompt(q, k, v, lam):
    b, s = q.shape[:2]
    nb = s // Q_BLOCK
    qb = q.reshape(b, nb, Q_BLOCK, ATTN_HEADS, 2, HEAD_DIM).swapaxes(0, 1)
    k_pos = jnp.arange(s, dtype=jnp.int32)

    def one_block(args):
        qblk, i = args
        q_pos = i * Q_BLOCK + jnp.arange(Q_BLOCK, dtype=jnp.int32)
        return diff_attention(qblk, k, v, q_pos, k_pos, lam)

    o = lax.map(one_block, (qb, jnp.arange(nb, dtype=jnp.int32)))
    return o.swapaxes(0, 1).reshape(b, s, ATTN_HEADS, 2 * HEAD_DIM)


def ssd_chunked(x, dt, a_coef, bm, cm, h0):
    b, t = x.shape[:2]
    nc = t // SSD_CHUNK
    a = dt * a_coef

    def chunks(u):
        return u.reshape(b, nc, SSD_CHUNK, *u.shape[2:]).swapaxes(0, 1)

    causal = jnp.tril(jnp.ones((SSD_CHUNK, SSD_CHUNK), dtype=bool))

    def step(h, inp):
        xc, dtc, ac, bc, cc = inp
        acum = jnp.cumsum(ac, axis=1)
        seg = jnp.exp(jnp.where(causal[None, :, :, None], acum[:, :, None, :] - acum[:, None, :, :], -jnp.inf))
        xdt = xc * dtc[..., None]
        y = (jnp.einsum('blhn,bshn,blsh,bshp->blhp', cc, bc, seg, xdt)
             + jnp.einsum('blhn,bhpn,blh->blhp', cc, h, jnp.exp(acum)))
        decay_end = jnp.exp(acum[:, -1:] - acum)
        h = h * jnp.exp(acum[:, -1])[:, :, None, None] + jnp.einsum('bshn,bsh,bshp->bhpn', bc, decay_end, xdt)
        return h, y

    h, y = lax.scan(step, h0, (chunks(x), chunks(dt), chunks(a), chunks(bm), chunks(cm)))
    return y.swapaxes(0, 1).reshape(b, t, SSD_HEADS, SSD_HEAD_DIM), h


def ssd_recurrent(x, dt, a_coef, bm, cm, h0):
    def step(h, inp):
        xt, dtt, bt, ct = inp
        h = h * jnp.exp(dtt * a_coef)[..., None, None] + jnp.einsum('bhp,bhn->bhpn', xt * dtt[..., None], bt)
        return h, jnp.einsum('bhpn,bhn->bhp', h, ct)

    h, y = lax.scan(step, h0, (x.swapaxes(0, 1), dt.swapaxes(0, 1), bm.swapaxes(0, 1), cm.swapaxes(0, 1)))
    return y.swapaxes(0, 1), h


def ssd_branch(z, xbc_raw, dt_raw, hist, h0, lp, decode):
    b, t = z.shape[:2]
    xp = jnp.concatenate([hist.astype(xbc_raw.dtype), xbc_raw], axis=1)
    xbc = causal_conv_silu(xp, lp['ssd_conv_w'], lp['ssd_conv_b']).astype(jnp.float32)
    xs, bm, cm = jnp.split(xbc, [SSD_INNER, SSD_INNER + SSD_GROUPS * SSD_STATE], axis=-1)
    xs = xs.reshape(b, t, SSD_HEADS, SSD_HEAD_DIM)
    rep = SSD_HEADS // SSD_GROUPS
    bm = jnp.repeat(bm.reshape(b, t, SSD_GROUPS, SSD_STATE), rep, axis=2)
    cm = jnp.repeat(cm.reshape(b, t, SSD_GROUPS, SSD_STATE), rep, axis=2)
    dt = jax.nn.softplus(dt_raw.astype(jnp.float32) + lp['ssd_dt_bias'].astype(jnp.float32))
    a_coef = -jnp.exp(lp['ssd_a_log'].astype(jnp.float32))
    h0 = h0.astype(jnp.float32)
    if decode:
        y, h = ssd_recurrent(xs, dt, a_coef, bm, cm, h0)
    else:
        y, h = ssd_chunked(xs, dt, a_coef, bm, cm, h0)
    y = y + lp['ssd_d'].astype(jnp.float32)[:, None] * xs
    gsize = SSD_INNER // SSD_GROUPS
    y = y.reshape(b, t, SSD_GROUPS, gsize) * jax.nn.silu(z.astype(jnp.float32)).reshape(b, t, SSD_GROUPS, gsize)
    y = rms_norm(y, lp['ssd_norm'].reshape(SSD_GROUPS, gsize)).reshape(b, t, SSD_INNER)
    out = jnp.einsum('btc,cd->btd', y.astype(z.dtype), lp['w_ssd_o'])
    return out, h, xp[:, -(CONV_WIDTH - 1):]


def to_chunks(u, c):
    b, t = u.shape[:2]
    u = u.reshape(b, t // c, c, *u.shape[2:]).swapaxes(2, 3)
    return jnp.moveaxis(u, 1, 0)


def gated_delta_chunked(q, k, v, g, beta, s0):
    b, t = q.shape[:2]
    qc, kc, vc = to_chunks(q, DN_CHUNK), to_chunks(k, DN_CHUNK), to_chunks(v, DN_CHUNK)
    gc = jnp.cumsum(to_chunks(g, DN_CHUNK), axis=-1)
    bc = to_chunks(beta, DN_CHUNK)
    incl = jnp.tril(jnp.ones((DN_CHUNK, DN_CHUNK), dtype=bool))
    strict = jnp.tril(jnp.ones((DN_CHUNK, DN_CHUNK), dtype=jnp.float32), -1)
    decay = jnp.exp(jnp.where(incl, gc[..., :, None] - gc[..., None, :], -jnp.inf))
    kb = kc * bc[..., None]
    a_mat = jnp.einsum('nbhid,nbhjd->nbhij', kb, kc) * decay * strict
    rhs = jnp.concatenate([vc * bc[..., None], kb * jnp.exp(gc)[..., None]], axis=-1)
    sol = lax.linalg.triangular_solve(a_mat, rhs, left_side=True, lower=True, unit_diagonal=True)
    u, w = sol[..., :DN_HEAD_V], sol[..., DN_HEAD_V:]
    qk = jnp.einsum('nbhid,nbhjd->nbhij', qc, kc) * decay
    qg = qc * jnp.exp(gc)[..., None]
    kd = kc * jnp.exp(gc[..., -1:] - gc)[..., None]
    glast = jnp.exp(gc[..., -1])

    def step(s, inp):
        u_i, w_i, qk_i, qg_i, kd_i, gl_i = inp
        v_new = u_i - jnp.einsum('bhcd,bhde->bhce', w_i, s)
        o = jnp.einsum('bhcd,bhde->bhce', qg_i, s) + jnp.einsum('bhij,bhje->bhie', qk_i, v_new)
        s = s * gl_i[..., None, None] + jnp.einsum('bhcd,bhce->bhde', kd_i, v_new)
        return s, o

    s, o = lax.scan(step, s0, (u, w, qk, qg, kd, glast))
    o = jnp.moveaxis(o, 0, 1).swapaxes(2, 3).reshape(b, t, DN_HEADS, DN_HEAD_V)
    return o, s


def gated_delta_recurrent(q, k, v, g, beta, s0):
    def step(s, inp):
        qt, kt, vt, gt, bt = inp
        s = s * jnp.exp(gt)[..., None, None]
        delta = (vt - jnp.einsum('bhde,bhd->bhe', s, kt)) * bt[..., None]
        s = s + jnp.einsum('bhd,bhe->bhde', kt, delta)
        return s, jnp.einsum('bhde,bhd->bhe', s, qt)

    s, o = lax.scan(step, s0, (q.swapaxes(0, 1), k.swapaxes(0, 1), v.swapaxes(0, 1), g.swapaxes(0, 1), beta.swapaxes(0, 1)))
    return o.swapaxes(0, 1), s


def deltanet_branch(qkv_raw, z, b_raw, a_raw, hist, s0, lp, decode):
    b, t = z.shape[:2]
    xp = jnp.concatenate([hist.astype(qkv_raw.dtype), qkv_raw], axis=1)
    qkv = causal_conv_silu(xp, lp['dn_conv_w']).astype(jnp.float32)
    q, k, v = jnp.split(qkv, [DN_KEY_WIDTH, 2 * DN_KEY_WIDTH], axis=-1)
    q = l2_normalize(q.reshape(b, t, DN_HEADS, DN_HEAD_K)) * (DN_HEAD_K ** -0.5)
    k = l2_normalize(k.reshape(b, t, DN_HEADS, DN_HEAD_K))
    v = v.reshape(b, t, DN_HEADS, DN_HEAD_V)
    beta = jax.nn.sigmoid(b_raw.astype(jnp.float32))
    g = -jnp.exp(lp['dn_a_log'].astype(jnp.float32)) * jax.nn.softplus(a_raw.astype(jnp.float32) + lp['dn_dt_bias'].astype(jnp.float32))
    s0 = s0.astype(jnp.float32)
    if decode:
        o, s = gated_delta_recurrent(q, k, v, g, beta, s0)
    else:
        o, s = gated_delta_chunked(q, k, v, g, beta, s0)
    o = rms_norm(o, lp['dn_norm']) * jax.nn.silu(z.astype(jnp.float32).reshape(b, t, DN_HEADS, DN_HEAD_V))
    out = jnp.einsum('btc,cd->btd', o.reshape(b, t, DN_VAL_WIDTH).astype(z.dtype), lp['w_dn_o'])
    return out, s, xp[:, -(CONV_WIDTH - 1):]


def moe_ffn(x2, lp):
    t = x2.shape[0]
    n_assign = t * TOP_K
    scores = jax.nn.sigmoid(jnp.dot(x2, lp['w_router'], preferred_element_type=jnp.float32))
    biased = scores + lp['router_bias'].astype(jnp.float32)
    group_score = lax.top_k(biased.reshape(t, N_GROUPS, N_EXPERTS // N_GROUPS), 2)[0].sum(-1)
    _, top_groups = lax.top_k(group_score, TOPK_GROUPS)
    group_mask = jax.nn.one_hot(top_groups, N_GROUPS, dtype=jnp.float32).sum(1) > 0
    expert_mask = jnp.repeat(group_mask, N_EXPERTS // N_GROUPS, axis=1)
    _, idx = lax.top_k(jnp.where(expert_mask, biased, -jnp.inf), TOP_K)
    gate = jnp.take_along_axis(scores, idx, axis=1)
    gate = gate / jnp.sum(gate, -1, keepdims=True) * ROUTE_SCALE
    flat_e = idx.reshape(n_assign).astype(jnp.int32)
    order = jnp.argsort(flat_e).astype(jnp.int32)
    sorted_e = flat_e[order]
    tok_sorted = order // TOP_K
    counts = jnp.zeros((N_EXPERTS,), jnp.int32).at[flat_e].add(1)
    padded = (counts + MOE_BLOCK - 1) // MOE_BLOCK * MOE_BLOCK
    pad_end = jnp.cumsum(padded)
    pad_start = pad_end - padded
    start = jnp.cumsum(counts) - counts
    dest = pad_start[sorted_e] + jnp.arange(n_assign, dtype=jnp.int32) - start[sorted_e]
    n_blocks = -(-n_assign // MOE_BLOCK) + N_EXPERTS
    buf_tok = jnp.zeros((n_blocks * MOE_BLOCK,), jnp.int32).at[dest].set(tok_sorted)
    block_expert = jnp.minimum(
        jnp.searchsorted(pad_end, jnp.arange(n_blocks, dtype=jnp.int32) * MOE_BLOCK, side='right'), N_EXPERTS - 1)
    xb = x2[buf_tok].reshape(n_blocks, MOE_BLOCK, D_MODEL)
    w_gate, w_up, w_down = lp['w_exp_gate'], lp['w_exp_up'], lp['w_exp_down']

    def expert_block(args):
        xblk, e = args
        h = jax.nn.silu(xblk @ w_gate[e]) * (xblk @ w_up[e])
        return h @ w_down[e]

    yb = lax.map(expert_block, (xb, block_expert)).reshape(n_blocks * MOE_BLOCK, D_MODEL)
    contrib = yb[dest].astype(jnp.float32) * gate.reshape(n_assign)[order][:, None]
    routed = jax.ops.segment_sum(contrib, tok_sorted, num_segments=t)
    shared = (jax.nn.silu(x2 @ lp['w_sh_gate']) * (x2 @ lp['w_sh_up'])) @ lp['w_sh_down']
    return (routed + shared.astype(jnp.float32)).astype(x2.dtype)


def trunk_layer(x, pos0, past_k, past_v, ssd_h0, ssd_hist, dn_s0, dn_hist, lp, layer_idx, decode):
    b, t, _ = x.shape
    proj = jnp.einsum('btd,dc->btc', x, lp['w_in'])
    (q_raw, k_raw, v_raw, ssd_z, ssd_xbc, ssd_dt, dn_qkv, dn_z, dn_b, dn_a, gate_raw) = jnp.split(proj, IN_OFFSETS, axis=-1)
    pos = pos0 + jnp.arange(t, dtype=jnp.int32)
    q = rope_partial(q_raw.reshape(b, t, ATTN_HEADS, 2, HEAD_DIM), pos)
    k = rope_partial(k_raw.reshape(b, t, ATTN_KV_HEADS, 2, HEAD_DIM), pos)
    v = v_raw.reshape(b, t, ATTN_KV_HEADS, 2 * HEAD_DIM)
    lam_init = 0.8 - 0.6 * math.exp(-0.3 * layer_idx)
    lq = lp['attn_lambda'].astype(jnp.float32)
    lam = jnp.exp(jnp.sum(lq[0] * lq[1])) - jnp.exp(jnp.sum(lq[2] * lq[3])) + lam_init
    if decode:
        k_all = jnp.concatenate([past_k.astype(k.dtype), k], axis=1)
        v_all = jnp.concatenate([past_v.astype(v.dtype), v], axis=1)
        o = diff_attention(q, k_all, v_all, pos, jnp.arange(k_all.shape[1], dtype=jnp.int32), lam)
    else:
        o = attn_prompt(q, k, v, lam)
    o = rms_norm(o, lp['attn_subln']) * (1.0 - lam_init)
    attn_out = jnp.einsum('btc,cd->btd', o.reshape(b, t, ATTN_WIDTH).astype(x.dtype), lp['w_attn_o'])
    ssd_out, ssd_h, ssd_buf = ssd_branch(ssd_z, ssd_xbc, ssd_dt, ssd_hist, ssd_h0, lp, decode)
    dn_out, dn_s, dn_buf = deltanet_branch(dn_qkv, dn_z, dn_b, dn_a, dn_hist, dn_s0, lp, decode)
    gates = jax.nn.sigmoid(gate_raw.reshape(b, t, N_BRANCHES, D_MODEL))
    mixed = gates[:, :, 0] * attn_out + gates[:, :, 1] * ssd_out + gates[:, :, 2] * dn_out
    mix = jnp.einsum('btd,de->bte', mixed, lp['w_out'])
    x = layer_norm(DEEPNORM_ALPHA * x + mix, lp['ln1_g'], lp['ln1_b'])
    ffn = moe_ffn(x.reshape(b * t, D_MODEL), lp).reshape(b, t, D_MODEL)
    x = layer_norm(DEEPNORM_ALPHA * x + ffn, lp['ln2_g'], lp['ln2_b'])
    new_k = k.reshape(b, t, ATTN_KV_HEADS, 2 * HEAD_DIM)
    return x, new_k, v, ssd_h, ssd_buf, dn_s, dn_buf


def setup_inputs(seed: int = 0) -> dict:
    key = jax.random.key(seed)
    ks = iter(jax.random.split(key, 64))
    f32 = jnp.float32

    def nrm(shape, scale):
        return jax.random.normal(next(ks), shape, f32) * scale

    def unif(shape, lo, hi):
        return jax.random.uniform(next(ks), shape, f32, lo, hi)

    def dt_bias(shape):
        dt = jnp.exp(unif(shape, math.log(1e-3), math.log(1e-1)))
        return dt + jnp.log(-jnp.expm1(-dt))

    n_pages = PAST_LEN // PAGE_SIZE
    n_pool = (5 * DEC_BATCH * n_pages) // 4
    page_table = jax.random.permutation(next(ks), n_pool)[:DEC_BATCH * n_pages].reshape(DEC_BATCH, n_pages).astype(jnp.int32)
    L = DEPTH
    return {
        'x_prompt': nrm((BATCH, SEQ, D_MODEL), 1.0),
        'x_sample': nrm((DEC_BATCH, DEC_SEQ, D_MODEL), 1.0),
        'cache_k': nrm((L, n_pool, PAGE_SIZE, ATTN_KV_HEADS, 2 * HEAD_DIM), 1.0),
        'cache_v': nrm((L, n_pool, PAGE_SIZE, ATTN_KV_HEADS, 2 * HEAD_DIM), 1.0),
        'state_ssd': nrm((L, DEC_BATCH, SSD_HEADS, SSD_HEAD_DIM, SSD_STATE), 0.1),
        'state_ssd_conv': nrm((L, DEC_BATCH, CONV_WIDTH - 1, SSD_CONV_DIM), 1.0),
        'state_dn': nrm((L, DEC_BATCH, DN_HEADS, DN_HEAD_K, DN_HEAD_V), 0.1),
        'state_dn_conv': nrm((L, DEC_BATCH, CONV_WIDTH - 1, DN_CONV_DIM), 1.0),
        'page_table': page_table,
        'w_in': nrm((L, D_MODEL, IN_COLS), D_MODEL ** -0.5),
        'attn_lambda': nrm((L, 4, HEAD_DIM), 0.1),
        'attn_subln': 1.0 + nrm((L, 2 * HEAD_DIM), 0.02),
        'w_attn_o': nrm((L, ATTN_WIDTH, D_MODEL), ATTN_WIDTH ** -0.5 * DEEPNORM_BETA),
        'ssd_conv_w': nrm((L, CONV_WIDTH, SSD_CONV_DIM), CONV_WIDTH ** -0.5),
        'ssd_conv_b': nrm((L, SSD_CONV_DIM), 0.02),
        'ssd_dt_bias': dt_bias((L, SSD_HEADS)),
        'ssd_a_log': jnp.log(unif((L, SSD_HEADS), 1.0, 16.0)),
        'ssd_d': 1.0 + nrm((L, SSD_HEADS), 0.02),
        'ssd_norm': 1.0 + nrm((L, SSD_INNER), 0.02),
        'w_ssd_o': nrm((L, SSD_INNER, D_MODEL), SSD_INNER ** -0.5 * DEEPNORM_BETA),
        'dn_conv_w': nrm((L, CONV_WIDTH, DN_CONV_DIM), CONV_WIDTH ** -0.5),
        'dn_dt_bias': dt_bias((L, DN_HEADS)),
        'dn_a_log': jnp.log(unif((L, DN_HEADS), 1.0, 16.0)),
        'dn_norm': 1.0 + nrm((L, DN_HEAD_V), 0.02),
        'w_dn_o': nrm((L, DN_VAL_WIDTH, D_MODEL), DN_VAL_WIDTH ** -0.5 * DEEPNORM_BETA),
        'w_out': nrm((L, D_MODEL, D_MODEL), D_MODEL ** -0.5 * DEEPNORM_BETA),
        'ln1_g': 1.0 + nrm((L, D_MODEL), 0.02),
        'ln1_b': nrm((L, D_MODEL), 0.02),
        'w_router': nrm((L, D_MODEL, N_EXPERTS), D_MODEL ** -0.5),
        'router_bias': nrm((L, N_EXPERTS), 0.01),
        'w_exp_gate': nrm((L, N_EXPERTS, D_MODEL, EXPERT_FF), D_MODEL ** -0.5),
        'w_exp_up': nrm((L, N_EXPERTS, D_MODEL, EXPERT_FF), D_MODEL ** -0.5),
        'w_exp_down': nrm((L, N_EXPERTS, EXPERT_FF, D_MODEL), EXPERT_FF ** -0.5 * DEEPNORM_BETA),
        'w_sh_gate': nrm((L, D_MODEL, SHARED_FF), D_MODEL ** -0.5),
        'w_sh_up': nrm((L, D_MODEL, SHARED_FF), D_MODEL ** -0.5),
        'w_sh_down': nrm((L, SHARED_FF, D_MODEL), SHARED_FF ** -0.5 * DEEPNORM_BETA),
        'ln2_g': 1.0 + nrm((L, D_MODEL), 0.02),
        'ln2_b': nrm((L, D_MODEL), 0.02),
    }


def reference(x_prompt, x_sample, cache_k, cache_v, state_ssd, state_ssd_conv, state_dn, state_dn_conv, page_table,
              w_in, attn_lambda, attn_subln, w_attn_o, ssd_conv_w, ssd_conv_b, ssd_dt_bias, ssd_a_log, ssd_d, ssd_norm,
              w_ssd_o, dn_conv_w, dn_dt_bias, dn_a_log, dn_norm, w_dn_o, w_out, ln1_g, ln1_b, w_router, router_bias,
              w_exp_gate, w_exp_up, w_exp_down, w_sh_gate, w_sh_up, w_sh_down, ln2_g, ln2_b):
    xp, xs = x_prompt, x_sample
    kp_l, vp_l, ks_l, vs_l = [], [], [], []
    hp_l, cp_l, hs_l, cs_l = [], [], [], []
    sp_l, dp_l, ss_l, ds_l = [], [], [], []
    for l in range(DEPTH):
        lp = dict(w_in=w_in[l], attn_lambda=attn_lambda[l], attn_subln=attn_subln[l], w_attn_o=w_attn_o[l],
                  ssd_conv_w=ssd_conv_w[l], ssd_conv_b=ssd_conv_b[l], ssd_dt_bias=ssd_dt_bias[l], ssd_a_log=ssd_a_log[l],
                  ssd_d=ssd_d[l], ssd_norm=ssd_norm[l], w_ssd_o=w_ssd_o[l], dn_conv_w=dn_conv_w[l],
                  dn_dt_bias=dn_dt_bias[l], dn_a_log=dn_a_log[l], dn_norm=dn_norm[l], w_dn_o=w_dn_o[l], w_out=w_out[l],
                  ln1_g=ln1_g[l], ln1_b=ln1_b[l], w_router=w_router[l], router_bias=router_bias[l],
                  w_exp_gate=w_exp_gate[l], w_exp_up=w_exp_up[l], w_exp_down=w_exp_down[l],
                  w_sh_gate=w_sh_gate[l], w_sh_up=w_sh_up[l], w_sh_down=w_sh_down[l], ln2_g=ln2_g[l], ln2_b=ln2_b[l])
        xp, kp, vp, hp, cp, sp, dp = trunk_layer(
            xp, 0, None, None,
            jnp.zeros((BATCH, SSD_HEADS, SSD_HEAD_DIM, SSD_STATE), jnp.float32),
            jnp.zeros((BATCH, CONV_WIDTH - 1, SSD_CONV_DIM), xp.dtype),
            jnp.zeros((BATCH, DN_HEADS, DN_HEAD_K, DN_HEAD_V), jnp.float32),
            jnp.zeros((BATCH, CONV_WIDTH - 1, DN_CONV_DIM), xp.dtype),
            lp, l, False)
        past_k = cache_k[l, page_table].reshape(DEC_BATCH, PAST_LEN, ATTN_KV_HEADS, 2, HEAD_DIM)
        past_v = cache_v[l, page_table].reshape(DEC_BATCH, PAST_LEN, ATTN_KV_HEADS, 2 * HEAD_DIM)
        xs, ks, vs, hs, cs, ss, ds = trunk_layer(
            xs, PAST_LEN, past_k, past_v, state_ssd[l], state_ssd_conv[l], state_dn[l], state_dn_conv[l],
            lp, l, True)
        kp_l.append(kp); vp_l.append(vp); ks_l.append(ks); vs_l.append(vs)
        hp_l.append(hp); cp_l.append(cp); hs_l.append(hs); cs_l.append(cs)
        sp_l.append(sp); dp_l.append(dp); ss_l.append(ss); ds_l.append(ds)
    return (xp, xs,
            jnp.stack(kp_l), jnp.stack(vp_l), jnp.stack(ks_l), jnp.stack(vs_l),
            jnp.stack(hp_l), jnp.stack(cp_l), jnp.stack(hs_l), jnp.stack(cs_l),
            jnp.stack(sp_l), jnp.stack(dp_l), jnp.stack(ss_l), jnp.stack(ds_l))
```

```python
import functools
import math

import jax
import jax.numpy as jnp
import numpy as np
from jax import lax
from jax.experimental import pallas as pl
from jax.experimental.pallas import tpu as pltpu

D = 1024
NB, SEQ = 4, 4096
DEPTH = 2
DB = 32
PAST = 16384
PAGE = 128
AH, KVH, AG, HD = 8, 4, 2, 64
ROT = 16
THETA = 500000.0
QBLK = 128
AW, KW = 1024, 512
SH, SP, SI, SG, SN = 16, 64, 1024, 2, 128
SCONV = SI + 2 * SG * SN
SCH = 64
NH, NK, NV = 8, 128, 128
NKW, NVW = 1024, 1024
NCONV = 3072
NCH = 64
CW = 4
NE, TOPK, NGRP, TOPG, EFF, SFF = 64, 8, 8, 4, 256, 256
RSCALE = 2.5
MBLK = 128
ALPHA = (2 * DEPTH) ** 0.25
LN_EPS = 1e-5
RMS_EPS = 1e-6

O_Q, O_K, O_V, O_Z, O_XBC, O_NQKV, O_NZ, O_GATE, O_SMALL = 0, 1024, 1536, 2048, 3072, 4608, 7680, 8704, 11776
PCOLS = 11904
R_Q, R_K, R_V, R_Z, R_XBC, R_DT, R_NQKV, R_NZ, R_NB, R_NA, R_GATE = (
    0, 1024, 1536, 2048, 3072, 4608, 4624, 7696, 8720, 8728, 8736)

BF = jnp.bfloat16
F32 = jnp.float32


def _permute_w_in(w):
    pad = jnp.zeros((D, 96), w.dtype)
    return jnp.concatenate([
        w[:, R_Q:R_DT], w[:, R_NQKV:R_NB], w[:, R_GATE:],
        w[:, R_DT:R_NQKV], w[:, R_NB:R_GATE], pad], axis=1)


def _mm_kernel(x_ref, w_ref, o_ref):
    o_ref[...] = jnp.dot(x_ref[...].astype(BF), w_ref[...].astype(BF),
                         preferred_element_type=F32).astype(o_ref.dtype)


def matmul(x, w, out_dtype=F32, tm=512, tn=512):
    m, k = x.shape
    n = w.shape[1]
    tm = min(tm, m)
    tn = min(tn, n)
    assert m % tm == 0 and n % tn == 0, (m, n, tm, tn)
    return pl.pallas_call(
        _mm_kernel,
        grid=(m // tm, n // tn),
        in_specs=[pl.BlockSpec((tm, k), lambda i, j: (i, 0)),
                  pl.BlockSpec((k, tn), lambda i, j: (0, j))],
        out_specs=pl.BlockSpec((tm, tn), lambda i, j: (i, j)),
        out_shape=jax.ShapeDtypeStruct((m, n), out_dtype),
        compiler_params=pltpu.CompilerParams(dimension_semantics=("parallel", "parallel")),
    )(x, w)


def layer_norm(x, g, b):
    mu = jnp.mean(x, -1, keepdims=True)
    var = jnp.mean(jnp.square(x - mu), -1, keepdims=True)
    return (x - mu) * lax.rsqrt(var + LN_EPS) * g + b


def rms_norm(x, g):
    return x * lax.rsqrt(jnp.mean(x * x, -1, keepdims=True) + RMS_EPS) * g


def l2n(x):
    return x * lax.rsqrt(jnp.sum(x * x, -1, keepdims=True) + RMS_EPS)


def rope(x, pos):
    half = ROT // 2
    inv = THETA ** (-jnp.arange(half, dtype=F32) * 2.0 / ROT)
    ang = pos.astype(F32)[:, None] * inv
    cos = jnp.cos(ang)[None, :, None, None, :]
    sin = jnp.sin(ang)[None, :, None, None, :]
    x1, x2, rest = x[..., :half], x[..., half:ROT], x[..., ROT:]
    return jnp.concatenate([x1 * cos - x2 * sin, x2 * cos + x1 * sin, rest], -1)


def conv_silu(xp, w, b=None):
    t = xp.shape[1] - (CW - 1)
    y = xp[:, 0:t] * w[0]
    for i in range(1, CW):
        y = y + xp[:, i:i + t] * w[i]
    if b is not None:
        y = y + b
    return jax.nn.silu(y)


def diff_attn(q, k, v, q_pos, k_pos, lam):
    b, tq = q.shape[:2]
    qg = q.reshape(b, tq, KVH, AG, 2, HD)
    s = jnp.einsum('bqhgcd,bkhcd->bhgcqk', qg.astype(BF), k.astype(BF), preferred_element_type=F32) * (HD ** -0.5)
    s = jnp.where(k_pos[None, :] <= q_pos[:, None], s, -jnp.inf)
    p = jax.nn.softmax(s, axis=-1)
    w = p[:, :, :, 0] - lam * p[:, :, :, 1]
    o = jnp.einsum('bhgqk,bkhe->bqhge', w.astype(BF), v.astype(BF), preferred_element_type=F32)
    return o.reshape(b, tq, AH, 2 * HD)


def attn_prompt(q, k, v, lam):
    b, s = q.shape[:2]
    nb = s // QBLK
    qb = q.reshape(b, nb, QBLK, AH, 2, HD).swapaxes(0, 1)
    k_pos = jnp.arange(s, dtype=jnp.int32)

    def one(args):
        qblk, i = args
        q_pos = i * QBLK + jnp.arange(QBLK, dtype=jnp.int32)
        return diff_attn(qblk, k, v, q_pos, k_pos, lam)

    o = lax.map(one, (qb, jnp.arange(nb, dtype=jnp.int32)))
    return o.swapaxes(0, 1).reshape(b, s, AH, 2 * HD)


def ssd_chunked(x, dt, a_coef, bm, cm, h0):
    b, t = x.shape[:2]
    nc = t // SCH
    a = dt * a_coef

    def chunks(u):
        return u.reshape(b, nc, SCH, *u.shape[2:]).swapaxes(0, 1)

    causal = jnp.tril(jnp.ones((SCH, SCH), dtype=bool))

    def step(h, inp):
        xc, dtc, ac, bc, cc = inp
        acum = jnp.cumsum(ac, axis=1)
        seg = jnp.exp(jnp.where(causal[None, :, :, None], acum[:, :, None, :] - acum[:, None, :, :], -jnp.inf))
        xdt = xc * dtc[..., None]
        y = (jnp.einsum('blhn,bshn,blsh,bshp->blhp', cc, bc, seg, xdt)
             + jnp.einsum('blhn,bhpn,blh->blhp', cc, h, jnp.exp(acum)))
        dend = jnp.exp(acum[:, -1:] - acum)
        h = h * jnp.exp(acum[:, -1])[:, :, None, None] + jnp.einsum('bshn,bsh,bshp->bhpn', bc, dend, xdt)
        return h, y

    h, y = lax.scan(step, h0, (chunks(x), chunks(dt), chunks(a), chunks(bm), chunks(cm)))
    return y.swapaxes(0, 1).reshape(b, t, SH, SP), h


def ssd_recurrent(x, dt, a_coef, bm, cm, h0):
    xt, dtt, bt, ct = x[:, 0], dt[:, 0], bm[:, 0], cm[:, 0]
    h = h0 * jnp.exp(dtt * a_coef)[..., None, None] + (xt * dtt[..., None])[..., :, None] * bt[..., None, :]
    y = jnp.sum(h * ct[..., None, :], -1)
    return y[:, None], h


def ssd_branch(z, xbc_raw, dt_raw, hist, h0, lp, decode):
    b, t = z.shape[:2]
    xp = jnp.concatenate([hist, xbc_raw], axis=1)
    xbc = conv_silu(xp, lp['ssd_conv_w'], lp['ssd_conv_b'])
    xs, bm, cm = jnp.split(xbc, [SI, SI + SG * SN], axis=-1)
    xs = xs.reshape(b, t, SH, SP)
    rep = SH // SG
    bm = jnp.repeat(bm.reshape(b, t, SG, SN), rep, axis=2)
    cm = jnp.repeat(cm.reshape(b, t, SG, SN), rep, axis=2)
    dt = jax.nn.softplus(dt_raw + lp['ssd_dt_bias'])
    a_coef = -jnp.exp(lp['ssd_a_log'])
    if decode:
        y, h = ssd_recurrent(xs, dt, a_coef, bm, cm, h0)
    else:
        y, h = ssd_chunked(xs, dt, a_coef, bm, cm, h0)
    y = y + lp['ssd_d'][:, None] * xs
    gs = SI // SG
    y = y.reshape(b, t, SG, gs) * jax.nn.silu(z).reshape(b, t, SG, gs)
    y = rms_norm(y, lp['ssd_norm'].reshape(SG, gs)).reshape(b, t, SI)
    return y, h, xp[:, -(CW - 1):]


def to_chunks(u, c):
    b, t = u.shape[:2]
    u = u.reshape(b, t // c, c, *u.shape[2:]).swapaxes(2, 3)
    return jnp.moveaxis(u, 1, 0)


def gdn_chunked(q, k, v, g, beta, s0):
    b, t = q.shape[:2]
    qc, kc, vc = to_chunks(q, NCH), to_chunks(k, NCH), to_chunks(v, NCH)
    gc = jnp.cumsum(to_chunks(g, NCH), axis=-1)
    bc = to_chunks(beta, NCH)
    incl = jnp.tril(jnp.ones((NCH, NCH), dtype=bool))
    strict = jnp.tril(jnp.ones((NCH, NCH), dtype=F32), -1)
    decay = jnp.exp(jnp.where(incl, gc[..., :, None] - gc[..., None, :], -jnp.inf))
    kb = kc * bc[..., None]
    a_mat = jnp.einsum('nbhid,nbhjd->nbhij', kb, kc) * decay * strict
    rhs = jnp.concatenate([vc * bc[..., None], kb * jnp.exp(gc)[..., None]], axis=-1)
    sol = lax.linalg.triangular_solve(a_mat, rhs, left_side=True, lower=True, unit_diagonal=True)
    u, w = sol[..., :NV], sol[..., NV:]
    qk = jnp.einsum('nbhid,nbhjd->nbhij', qc, kc) * decay
    qg = qc * jnp.exp(gc)[..., None]
    kd = kc * jnp.exp(gc[..., -1:] - gc)[..., None]
    glast = jnp.exp(gc[..., -1])

    def step(s, inp):
        u_i, w_i, qk_i, qg_i, kd_i, gl_i = inp
        v_new = u_i - jnp.einsum('bhcd,bhde->bhce', w_i, s)
        o = jnp.einsum('bhcd,bhde->bhce', qg_i, s) + jnp.einsum('bhij,bhje->bhie', qk_i, v_new)
        s = s * gl_i[..., None, None] + jnp.einsum('bhcd,bhce->bhde', kd_i, v_new)
        return s, o

    s, o = lax.scan(step, s0, (u, w, qk, qg, kd, glast))
    o = jnp.moveaxis(o, 0, 1).swapaxes(2, 3).reshape(b, t, NH, NV)
    return o, s


def gdn_recurrent(q, k, v, g, beta, s0):
    qt, kt, vt, gt, bt = q[:, 0], k[:, 0], v[:, 0], g[:, 0], beta[:, 0]
    s = s0 * jnp.exp(gt)[..., None, None]
    delta = (vt - jnp.sum(s * kt[..., :, None], -2)) * bt[..., None]
    s = s + kt[..., :, None] * delta[..., None, :]
    o = jnp.sum(s * qt[..., :, None], -2)
    return o[:, None], s


def gdn_branch(qkv_raw, z, b_raw, a_raw, hist, s0, lp, decode):
    b, t = z.shape[:2]
    xp = jnp.concatenate([hist, qkv_raw], axis=1)
    qkv = conv_silu(xp, lp['dn_conv_w'])
    q, k, v = jnp.split(qkv, [NKW, 2 * NKW], axis=-1)
    q = l2n(q.reshape(b, t, NH, NK)) * (NK ** -0.5)
    k = l2n(k.reshape(b, t, NH, NK))
    v = v.reshape(b, t, NH, NV)
    beta = jax.nn.sigmoid(b_raw)
    g = -jnp.exp(lp['dn_a_log']) * jax.nn.softplus(a_raw + lp['dn_dt_bias'])
    if decode:
        o, s = gdn_recurrent(q, k, v, g, beta, s0)
    else:
        o, s = gdn_chunked(q, k, v, g, beta, s0)
    o = rms_norm(o, lp['dn_norm']) * jax.nn.silu(z.reshape(b, t, NH, NV))
    return o.reshape(b, t, NVW), s, xp[:, -(CW - 1):]


def moe_ffn(x2, lp):
    t = x2.shape[0]
    n_assign = t * TOPK
    scores = jax.nn.sigmoid(jnp.dot(x2, lp['w_router'], preferred_element_type=F32, precision=lax.Precision.HIGHEST))
    biased = scores + lp['router_bias']
    group_score = lax.top_k(biased.reshape(t, NGRP, NE // NGRP), 2)[0].sum(-1)
    _, top_groups = lax.top_k(group_score, TOPG)
    group_mask = jax.nn.one_hot(top_groups, NGRP, dtype=F32).sum(1) > 0
    expert_mask = jnp.repeat(group_mask, NE // NGRP, axis=1)
    _, idx = lax.top_k(jnp.where(expert_mask, biased, -jnp.inf), TOPK)
    gate = jnp.take_along_axis(scores, idx, axis=1)
    gate = gate / jnp.sum(gate, -1, keepdims=True) * RSCALE
    flat_e = idx.reshape(n_assign).astype(jnp.int32)
    order = jnp.argsort(flat_e).astype(jnp.int32)
    sorted_e = flat_e[order]
    tok_sorted = order // TOPK
    counts = jnp.zeros((NE,), jnp.int32).at[flat_e].add(1)
    padded = (counts + MBLK - 1) // MBLK * MBLK
    pad_end = jnp.cumsum(padded)
    pad_start = pad_end - padded
    start = jnp.cumsum(counts) - counts
    dest = pad_start[sorted_e] + jnp.arange(n_assign, dtype=jnp.int32) - start[sorted_e]
    n_blocks = -(-n_assign // MBLK) + NE
    buf_tok = jnp.zeros((n_blocks * MBLK,), jnp.int32).at[dest].set(tok_sorted)
    block_expert = jnp.minimum(
        jnp.searchsorted(pad_end, jnp.arange(n_blocks, dtype=jnp.int32) * MBLK, side='right'), NE - 1)
    xb = x2[buf_tok].reshape(n_blocks, MBLK, D).astype(BF)
    w_gate, w_up, w_down = lp['w_exp_gate'], lp['w_exp_up'], lp['w_exp_down']

    def expert_block(args):
        xblk, e = args
        h = jax.nn.silu(jnp.dot(xblk, w_gate[e].astype(BF), preferred_element_type=F32)) * jnp.dot(
            xblk, w_up[e].astype(BF), preferred_element_type=F32)
        return jnp.dot(h.astype(BF), w_down[e].astype(BF), preferred_element_type=F32)

    yb = lax.map(expert_block, (xb, block_expert)).reshape(n_blocks * MBLK, D)
    contrib = yb[dest] * gate.reshape(n_assign)[order][:, None]
    routed = jax.ops.segment_sum(contrib, tok_sorted, num_segments=t)
    sh = jax.nn.silu(matmul(x2, lp['w_sh_gate'])) * matmul(x2, lp['w_sh_up'])
    shared = matmul(sh, lp['w_sh_down'])
    return routed + shared


def trunk_layer(x, pos0, past_k, past_v, ssd_h0, ssd_hist, dn_s0, dn_hist, lp, layer_idx, decode):
    b, t, _ = x.shape
    proj = matmul(x.reshape(b * t, D), lp['w_in_p'], tn=384).reshape(b, t, PCOLS)
    q_raw = proj[..., O_Q:O_K]
    k_raw = proj[..., O_K:O_V]
    v_raw = proj[..., O_V:O_Z]
    ssd_z = proj[..., O_Z:O_XBC]
    ssd_xbc = proj[..., O_XBC:O_NQKV]
    dn_qkv = proj[..., O_NQKV:O_NZ]
    dn_z = proj[..., O_NZ:O_GATE]
    gate_raw = proj[..., O_GATE:O_SMALL]
    ssd_dt = proj[..., O_SMALL:O_SMALL + 16]
    dn_b = proj[..., O_SMALL + 16:O_SMALL + 24]
    dn_a = proj[..., O_SMALL + 24:O_SMALL + 32]
    pos = pos0 + jnp.arange(t, dtype=jnp.int32)
    q = rope(q_raw.reshape(b, t, AH, 2, HD), pos)
    k = rope(k_raw.reshape(b, t, KVH, 2, HD), pos)
    v = v_raw.reshape(b, t, KVH, 2 * HD)
    lam_init = 0.8 - 0.6 * math.exp(-0.3 * layer_idx)
    lq = lp['attn_lambda']
    lam = jnp.exp(jnp.sum(lq[0] * lq[1])) - jnp.exp(jnp.sum(lq[2] * lq[3])) + lam_init
    if decode:
        k_all = jnp.concatenate([past_k, k], axis=1)
        v_all = jnp.concatenate([past_v, v], axis=1)
        o = diff_attn(q, k_all, v_all, pos, jnp.arange(k_all.shape[1], dtype=jnp.int32), lam)
    else:
        o = attn_prompt(q, k, v, lam)
    o = rms_norm(o, lp['attn_subln']) * (1.0 - lam_init)
    attn_out = matmul(o.reshape(b * t, AW), lp['w_attn_o']).reshape(b, t, D)
    ssd_y, ssd_h, ssd_buf = ssd_branch(ssd_z, ssd_xbc, ssd_dt, ssd_hist, ssd_h0, lp, decode)
    ssd_out = matmul(ssd_y.reshape(b * t, SI), lp['w_ssd_o']).reshape(b, t, D)
    dn_o, dn_s, dn_buf = gdn_branch(dn_qkv, dn_z, dn_b, dn_a, dn_hist, dn_s0, lp, decode)
    dn_out = matmul(dn_o.reshape(b * t, NVW), lp['w_dn_o']).reshape(b, t, D)
    gates = jax.nn.sigmoid(gate_raw.reshape(b, t, 3, D))
    mixed = gates[:, :, 0] * attn_out + gates[:, :, 1] * ssd_out + gates[:, :, 2] * dn_out
    mix = matmul(mixed.reshape(b * t, D), lp['w_out']).reshape(b, t, D)
    x = layer_norm(ALPHA * x + mix, lp['ln1_g'], lp['ln1_b'])
    ffn = moe_ffn(x.reshape(b * t, D), lp).reshape(b, t, D)
    x = layer_norm(ALPHA * x + ffn, lp['ln2_g'], lp['ln2_b'])
    new_k = k.reshape(b, t, KVH, 2 * HD)
    return x, new_k, v, ssd_h, ssd_buf, dn_s, dn_buf


def kernel(x_prompt, x_sample, cache_k, cache_v, state_ssd, state_ssd_conv, state_dn, state_dn_conv, page_table,
           w_in, attn_lambda, attn_subln, w_attn_o, ssd_conv_w, ssd_conv_b, ssd_dt_bias, ssd_a_log, ssd_d, ssd_norm,
           w_ssd_o, dn_conv_w, dn_dt_bias, dn_a_log, dn_norm, w_dn_o, w_out, ln1_g, ln1_b, w_router, router_bias,
           w_exp_gate, w_exp_up, w_exp_down, w_sh_gate, w_sh_up, w_sh_down, ln2_g, ln2_b):
    xp, xs = x_prompt, x_sample
    outs = [[] for _ in range(12)]
    for l in range(DEPTH):
        lp = dict(w_in_p=_permute_w_in(w_in[l]), attn_lambda=attn_lambda[l], attn_subln=attn_subln[l],
                  w_attn_o=w_attn_o[l], ssd_conv_w=ssd_conv_w[l], ssd_conv_b=ssd_conv_b[l],
                  ssd_dt_bias=ssd_dt_bias[l], ssd_a_log=ssd_a_log[l], ssd_d=ssd_d[l], ssd_norm=ssd_norm[l],
                  w_ssd_o=w_ssd_o[l], dn_conv_w=dn_conv_w[l], dn_dt_bias=dn_dt_bias[l], dn_a_log=dn_a_log[l],
                  dn_norm=dn_norm[l], w_dn_o=w_dn_o[l], w_out=w_out[l], ln1_g=ln1_g[l], ln1_b=ln1_b[l],
                  w_router=w_router[l], router_bias=router_bias[l], w_exp_gate=w_exp_gate[l],
                  w_exp_up=w_exp_up[l], w_exp_down=w_exp_down[l], w_sh_gate=w_sh_gate[l], w_sh_up=w_sh_up[l],
                  w_sh_down=w_sh_down[l], ln2_g=ln2_g[l], ln2_b=ln2_b[l])
        xp, kp, vp, hp, cp, sp, dp = trunk_layer(
            xp, 0, None, None,
            jnp.zeros((NB, SH, SP, SN), F32), jnp.zeros((NB, CW - 1, SCONV), F32),
            jnp.zeros((NB, NH, NK, NV), F32), jnp.zeros((NB, CW - 1, NCONV), F32), lp, l, False)
        past_k = cache_k[l][page_table].reshape(DB, PAST, KVH, 2, HD)
        past_v = cache_v[l][page_table].reshape(DB, PAST, KVH, 2 * HD)
        xs, ks, vs, hs, cs, ss, ds = trunk_layer(
            xs, PAST, past_k, past_v, state_ssd[l], state_ssd_conv[l], state_dn[l], state_dn_conv[l], lp, l, True)
        for lst, val in zip(outs, (kp, vp, ks, vs, hp, cp, hs, cs, sp, dp, ss, ds)):
            lst.append(val)
    return (xp, xs) + tuple(jnp.stack(o) for o in outs)
```

```python
import functools
import math

import jax
import jax.numpy as jnp
from jax import lax
from jax.experimental import pallas as pl
from jax.experimental.pallas import tpu as pltpu

D = 1024
NB, SEQ = 4, 4096
DEPTH = 2
DB = 32
PAST = 16384
PAGE = 128
AH, KVH, HD = 8, 4, 64
ROT = 16
THETA = 500000.0
SH, SP, SI, SG, SN = 16, 64, 1024, 2, 128
SCONV = SI + 2 * SG * SN
NH, NK, NV = 8, 128, 128
NKW = 1024
NCONV = 3072
CW = 4
NE, TOPK, NGRP, TOPG = 64, 8, 8, 4
RSCALE = 2.5
ALPHA = (2 * DEPTH) ** 0.25
LN_EPS = 1e-5
RMS_EPS = 1e-6

R_Q, R_K, R_V, R_Z, R_XBC, R_DT, R_NQKV, R_NZ, R_NB, R_NA, R_GATE = (
    0, 1024, 1536, 2048, 3072, 4608, 4624, 7696, 8720, 8728, 8736)
L_DT, L_NB, L_NA = 0, 16, 24

LANES = 128
VMEM_LIMIT = 56 * 1024 * 1024
SSD_CHUNK = 128
GDN_CHUNK = 64
ATT_BLK = 512
FLASH_ROWS = 32
PAGES_PER_STEP = 16
MOE_BLK_PROMPT = 512
MOE_BLK_DECODE = 16

BF = jnp.bfloat16
F32 = jnp.float32


def _cparams(sem):
    return pltpu.CompilerParams(dimension_semantics=sem, vmem_limit_bytes=VMEM_LIMIT)


def _dot(a, b):
    return jnp.dot(a, b, preferred_element_type=F32)


def _dot_nt(a, b):
    return lax.dot_general(a, b, (((1,), (1,)), ((), ())), preferred_element_type=F32)


def _dot_tn(a, b):
    return lax.dot_general(a, b, (((0,), (0,)), ((), ())), preferred_element_type=F32)


def _split2(x):
    hi = x.astype(BF)
    lo = (x - hi.astype(F32)).astype(BF)
    return hi, lo


def _split3(x):
    hi = x.astype(BF)
    r = x - hi.astype(F32)
    mid = r.astype(BF)
    lo = (r - mid.astype(F32)).astype(BF)
    return hi, mid, lo


def _dot_sel(sel, x):
    hi, mid, lo = _split3(x)
    return _dot(sel, hi) + _dot(sel, mid) + _dot(sel, lo)


def _dot_x_sel(x, sel):
    hi, mid, lo = _split3(x)
    return _dot(hi, sel) + _dot(mid, sel) + _dot(lo, sel)


def _dot3(a, b):
    ah, al = _split2(a)
    bh, bl = _split2(b)
    return _dot(ah, bh) + _dot(ah, bl) + _dot(al, bh)


def _sigmoid(x):
    return 1.0 / (1.0 + jnp.exp(-x))


def _silu(x):
    return x * _sigmoid(x)


def _softplus(x):
    return jnp.maximum(x, 0.0) + jnp.log(1.0 + jnp.exp(-jnp.abs(x)))


def _mm_kernel(x_ref, w_ref, o_ref, xs_ref):
    @pl.when(pl.program_id(1) == 0)
    def _():
        xs_ref[...] = x_ref[...].astype(BF)

    o_ref[...] = _dot(xs_ref[...], w_ref[...]).astype(o_ref.dtype)


def matmul(x, w, out_dtype, tm=512, tn=512):
    m, k = x.shape
    n = w.shape[1]
    tm, tn = min(tm, m), min(tn, n)
    assert m % tm == 0 and n % tn == 0, (m, n, tm, tn)
    return pl.pallas_call(
        _mm_kernel,
        grid=(m // tm, n // tn),
        in_specs=[pl.BlockSpec((tm, k), lambda i, j: (i, 0)),
                  pl.BlockSpec((k, tn), lambda i, j: (0, j))],
        out_specs=pl.BlockSpec((tm, tn), lambda i, j: (i, j)),
        out_shape=jax.ShapeDtypeStruct((m, n), out_dtype),
        scratch_shapes=[pltpu.VMEM((tm, k), BF)],
        compiler_params=_cparams(("parallel", "arbitrary")),
        name="proj_matmul",
    )(x, w)


def _rope_tables(pos):
    half = ROT // 2
    inv = THETA ** (-jnp.arange(half, dtype=F32) * 2.0 / ROT)
    ang = pos.astype(F32)[:, None] * inv
    cos, sin = jnp.cos(ang), jnp.sin(ang)
    t = pos.shape[0]
    one, zero, z8 = jnp.ones((t, HD - ROT), F32), jnp.zeros((t, HD - ROT), F32), jnp.zeros((t, half), F32)
    c = jnp.concatenate([cos, cos, one, cos, cos, one], 1)
    sa = jnp.concatenate([-sin, z8, zero, -sin, z8, zero], 1)
    sb = jnp.concatenate([z8, sin, zero, z8, sin, zero], 1)
    return c, sa, sb


def _qkv_kernel(x_ref, w_ref, c_ref, sa_ref, sb_ref, q_ref, kf_ref, kb_ref, vf_ref, vb_ref):
    acc = _dot(x_ref[...].astype(BF), w_ref[...])
    c, sa, sb = c_ref[...], sa_ref[...], sb_ref[...]

    def rot(xg):
        return xg * c + pltpu.roll(xg, LANES - ROT // 2, 1) * sa + pltpu.roll(xg, ROT // 2, 1) * sb

    for h in range(AH):
        sl = slice(h * LANES, (h + 1) * LANES)
        q_ref[:, sl] = (rot(acc[:, sl]) * (HD ** -0.5)).astype(BF)
    for h in range(KVH):
        sl = slice(h * LANES, (h + 1) * LANES)
        kr = rot(acc[:, R_K + h * LANES:R_K + (h + 1) * LANES])
        kf_ref[:, sl] = kr
        kb_ref[:, sl] = kr.astype(BF)
    v = acc[:, R_V:R_Z]
    vf_ref[...] = v
    vb_ref[...] = v.astype(BF)


def qkv_project(x, w, tabs, tm):
    m = x.shape[0]
    tm = min(tm, m)
    nt = tabs[0].shape[0] // tm
    kw = KVH * 2 * HD
    tab_spec = pl.BlockSpec((tm, LANES), lambda i: (i % nt, 0))
    return pl.pallas_call(
        _qkv_kernel,
        grid=(m // tm,),
        in_specs=[pl.BlockSpec((tm, D), lambda i: (i, 0)),
                  pl.BlockSpec((D, R_Z), lambda i: (0, 0)),
                  tab_spec, tab_spec, tab_spec],
        out_specs=[pl.BlockSpec((tm, AH * 2 * HD), lambda i: (i, 0)),
                   pl.BlockSpec((tm, kw), lambda i: (i, 0)), pl.BlockSpec((tm, kw), lambda i: (i, 0)),
                   pl.BlockSpec((tm, kw), lambda i: (i, 0)), pl.BlockSpec((tm, kw), lambda i: (i, 0))],
        out_shape=[jax.ShapeDtypeStruct((m, AH * 2 * HD), BF),
                   jax.ShapeDtypeStruct((m, kw), F32), jax.ShapeDtypeStruct((m, kw), BF),
                   jax.ShapeDtypeStruct((m, kw), F32), jax.ShapeDtypeStruct((m, kw), BF)],
        compiler_params=_cparams(("parallel",)),
        name="qkv_rope",
    )(x, w, *tabs)


def _lambda_value(lq, lam_init):
    a = jnp.sum(lq[0:1, :] * lq[1:2, :], axis=-1, keepdims=True)
    b = jnp.sum(lq[2:3, :] * lq[3:4, :], axis=-1, keepdims=True)
    return jnp.exp(a) - jnp.exp(b) + lam_init


def _flash_kernel(lq_ref, sub_ref, q_ref, k_ref, v_ref, o_ref, qs_ref, m_ref, acc_ref, s_ref, p_ref, *, blk, lam_init):
    qi, ki = pl.program_id(2), pl.program_id(3)

    @pl.when(ki == 0)
    def _():
        q = q_ref[...]
        lane = lax.broadcasted_iota(jnp.int32, (blk, LANES), 1)
        for c in range(2):
            msk = (lane < HD) if c == 0 else (lane >= HD)
            for g in range(2):
                qs_ref[c, g * blk:(g + 1) * blk, :] = jnp.where(msk, q[:, g * LANES:(g + 1) * LANES], 0)
        m_ref[...] = jnp.full(m_ref.shape, -jnp.inf, F32)
        acc_ref[...] = jnp.zeros(acc_ref.shape, F32)

    def update(masked):
        k = k_ref[...]
        vext = jnp.concatenate([v_ref[...], jnp.ones((blk, LANES), BF)], axis=1)
        units = [(c, g) for g in range(2) for c in range(2)]
        for u, (c, g) in enumerate(units):
            s_ref[u] = _dot_nt(qs_ref[c, g * blk:(g + 1) * blk, :], k)
        if masked:
            keep = (lax.broadcasted_iota(jnp.int32, (blk, blk), 1) <= lax.broadcasted_iota(jnp.int32, (blk, blk), 0))
            diff = (lax.broadcasted_iota(jnp.int32, (FLASH_ROWS, blk), 1)
                    - lax.broadcasted_iota(jnp.int32, (FLASH_ROWS, blk), 0))
        for u, (c, g) in enumerate(units):
            rows = slice(g * blk, (g + 1) * blk)
            s_full = s_ref[u]
            if masked:
                s_full = jnp.where(keep, s_full, -jnp.inf)
            m_old = m_ref[c, rows]
            m_new = jnp.maximum(m_old, jnp.max(s_full, axis=-1, keepdims=True))
            alpha = jnp.exp(m_old - m_new)
            for r in range(0, blk, FLASH_ROWS):
                s = s_ref[u, r:r + FLASH_ROWS, :]
                if masked:
                    s = jnp.where(diff <= r, s, -jnp.inf)
                p_ref[u, r:r + FLASH_ROWS, :] = jnp.exp(s - m_new[r:r + FLASH_ROWS]).astype(BF)
            m_ref[c, rows] = m_new
            acc_ref[c, rows] = alpha * acc_ref[c, rows] + _dot(p_ref[u], vext)

    @pl.when(ki < qi)
    def _():
        update(False)

    @pl.when(ki == qi)
    def _():
        update(True)
        lam = _lambda_value(lq_ref[...], lam_init)
        a0, a1 = acc_ref[0], acc_ref[1]
        o = a0[:, :LANES] / a0[:, LANES:] - lam * (a1[:, :LANES] / a1[:, LANES:])
        o = o * lax.rsqrt(jnp.mean(o * o, axis=-1, keepdims=True) + RMS_EPS) * sub_ref[...] * (1.0 - lam_init)
        o_ref[:, 0:LANES] = o[0:blk].astype(BF)
        o_ref[:, LANES:2 * LANES] = o[blk:2 * blk].astype(BF)


def flash_diff_attention(q, k, v, lq, subln, lam_init, nb, seq, blk):
    nblk = seq // blk
    kern = functools.partial(_flash_kernel, blk=blk, lam_init=lam_init)
    return pl.pallas_call(
        kern,
        grid=(nb, KVH, nblk, nblk),
        in_specs=[pl.BlockSpec((4, HD), lambda b, h, i, j: (0, 0)),
                  pl.BlockSpec((1, LANES), lambda b, h, i, j: (0, 0)),
                  pl.BlockSpec((blk, 2 * LANES), lambda b, h, i, j: (b * nblk + i, h)),
                  pl.BlockSpec((blk, LANES), lambda b, h, i, j: (b * nblk + jnp.minimum(i, j), h)),
                  pl.BlockSpec((blk, LANES), lambda b, h, i, j: (b * nblk + jnp.minimum(i, j), h))],
        out_specs=pl.BlockSpec((blk, 2 * LANES), lambda b, h, i, j: (b * nblk + i, h)),
        out_shape=jax.ShapeDtypeStruct((nb * seq, AH * 2 * HD), BF),
        scratch_shapes=[pltpu.VMEM((2, 2 * blk, LANES), BF), pltpu.VMEM((2, 2 * blk, 1), F32),
                        pltpu.VMEM((2, 2 * blk, 2 * LANES), F32),
                        pltpu.VMEM((4, blk, blk), F32), pltpu.VMEM((4, blk, blk), BF)],
        compiler_params=_cparams(("parallel", "parallel", "parallel", "arbitrary")),
        name="flash_diff_attn",
    )(lq, subln.reshape(1, LANES), q, k, v)


def _paged_kernel(pt_ref, q_ref, ks_ref, vs_ref, *refs, pps):
    kp, vp = refs[:pps], refs[pps:2 * pps]
    o_ref, m_ref, l_ref, acc_ref = refs[2 * pps:]
    j = pl.program_id(1)
    q = q_ref[...]

    @pl.when(j == 0)
    def _():
        ks = ks_ref[...].astype(BF).astype(F32)
        m_ref[...] = jnp.sum(q.astype(F32) * ks, axis=-1, keepdims=True)
        l_ref[...] = jnp.ones(l_ref.shape, F32)
        acc_ref[...] = jnp.broadcast_to(vs_ref[...].astype(BF).astype(F32), acc_ref.shape)

    s = jnp.concatenate([_dot_nt(q, kp[i][...].astype(BF)) for i in range(pps)], axis=1)
    m_old = m_ref[...]
    m_new = jnp.maximum(m_old, jnp.max(s, axis=-1, keepdims=True))
    alpha = jnp.exp(m_old - m_new)
    p = jnp.exp(s - m_new)
    l_ref[...] = alpha * l_ref[...] + jnp.sum(p, axis=-1, keepdims=True)
    pv = _dot(p[:, 0:PAGE].astype(BF), vp[0][...].astype(BF))
    for i in range(1, pps):
        pv = pv + _dot(p[:, i * PAGE:(i + 1) * PAGE].astype(BF), vp[i][...].astype(BF))
    acc_ref[...] = alpha * acc_ref[...] + pv
    m_ref[...] = m_new

    @pl.when(j == pl.num_programs(1) - 1)
    def _():
        o_ref[...] = acc_ref[...] / l_ref[...]


def paged_attention(qbd, k_self, v_self, cache_k, cache_v, page_table, layer, pps):
    nseq, npg = page_table.shape
    kw = KVH * 2 * HD
    ck = cache_k.reshape(cache_k.shape[0], cache_k.shape[1], PAGE, kw)
    cv = cache_v.reshape(cache_v.shape[0], cache_v.shape[1], PAGE, kw)

    def page_spec(i):
        return pl.BlockSpec((None, None, PAGE, kw), lambda b, j, pt: (layer, pt[b, j * pps + i], 0, 0))

    row = pl.BlockSpec((None, 1, kw), lambda b, j, pt: (b, 0, 0))
    grid_spec = pltpu.PrefetchScalarGridSpec(
        num_scalar_prefetch=1,
        grid=(nseq, npg // pps),
        in_specs=[pl.BlockSpec((None, 16, kw), lambda b, j, pt: (b, 0, 0)), row, row]
        + [page_spec(i) for i in range(pps)] + [page_spec(i) for i in range(pps)],
        out_specs=pl.BlockSpec((None, 16, kw), lambda b, j, pt: (b, 0, 0)),
        scratch_shapes=[pltpu.VMEM((16, 1), F32), pltpu.VMEM((16, 1), F32), pltpu.VMEM((16, kw), F32)],
    )
    return pl.pallas_call(
        functools.partial(_paged_kernel, pps=pps),
        grid_spec=grid_spec,
        out_shape=jax.ShapeDtypeStruct((nseq, 16, kw), F32),
        compiler_params=_cparams(("parallel", "arbitrary")),
        name="paged_diff_attn",
    )(page_table, qbd, k_self.reshape(nseq, 1, kw), v_self.reshape(nseq, 1, kw),
      *([ck] * pps), *([cv] * pps))


def _ssd_kernel(xbc_ref, z_ref, sm_ref, cw_ref, cb_ref, dtb_ref, ac_ref, dx_ref, nw_ref, e_ref,
                y_ref, st_ref, tail_ref, xbuf, s_ref, *, chunk):
    n = pl.program_id(1)
    L = chunk

    @pl.when(n == 0)
    def _():
        xbuf[0:8, :] = jnp.zeros((8, SCONV), F32)
        s_ref[...] = jnp.zeros(s_ref.shape, F32)

    xbuf[8:8 + L, :] = xbc_ref[...].astype(F32)
    y = cb_ref[...] + cw_ref[0:1, :] * xbuf[pl.ds(5, L), :]
    for t in range(1, CW):
        y = y + cw_ref[t:t + 1, :] * xbuf[pl.ds(5 + t, L), :]
    xbc = _silu(y)
    xs, bm, cm = xbc[:, :SI], xbc[:, SI:SI + SG * SN], xbc[:, SI + SG * SN:]

    dt = _softplus(sm_ref[...] + dtb_ref[...])
    a = dt * ac_ref[...]
    row = lax.broadcasted_iota(jnp.int32, (L, L), 0)
    col = lax.broadcasted_iota(jnp.int32, (L, L), 1)
    causal = row >= col
    acum = _dot_sel(causal.astype(BF), a)
    acum_t = acum.T
    alast = acum[L - 1:L, :]
    e = e_ref[...]
    dt_e = _dot_x_sel(dt, e)
    ea_e = _dot_x_sel(jnp.exp(acum), e)
    dend_e = _dot_x_sel(jnp.exp(alast - acum), e)
    xdt = xs * dt_e
    xd = (xdt * dend_e).astype(BF)
    lane = lax.broadcasted_iota(jnp.int32, (L, LANES), 1)
    gw = SI // SG
    hpg = SH // SG
    y_parts = []
    for g in range(SG):
        cg = cm[:, g * SN:(g + 1) * SN].astype(BF)
        bg = bm[:, g * SN:(g + 1) * SN].astype(BF)
        s_old = s_ref[g * gw:(g + 1) * gw, :]
        y_inter = _dot_nt(cg, s_old.astype(BF))
        cb = _dot_nt(cg, bg)
        pairs = []
        for jp in range(hpg // 2):
            h0 = g * hpg + 2 * jp
            xp = xdt[:, h0 * SP:h0 * SP + LANES]
            acc = None
            for d in range(2):
                h = h0 + d
                seg = jnp.where(causal, jnp.exp(acum[:, h:h + 1] - acum_t[h:h + 1, :]), 0.0)
                mh = (cb * seg).astype(BF)
                xm = jnp.where((lane < SP) if d == 0 else (lane >= SP), xp, 0.0).astype(BF)
                t = _dot(mh, xm)
                acc = t if acc is None else acc + t
            pairs.append(acc)
        y_intra = jnp.concatenate(pairs, axis=1)
        y_parts.append(y_intra + y_inter * ea_e[:, g * gw:(g + 1) * gw])
        upd = _dot_tn(xd[:, g * gw:(g + 1) * gw], bg)
        for hh in range(hpg):
            h = g * hpg + hh
            r0 = h * SP
            s_ref[r0:r0 + SP, :] = (s_old[hh * SP:(hh + 1) * SP, :] * jnp.exp(alast[:, h:h + 1])
                                    + upd[hh * SP:(hh + 1) * SP, :])
    yv = jnp.concatenate(y_parts, axis=1) + dx_ref[...] * xs
    yv = yv * _silu(z_ref[...].astype(F32))
    outs = []
    for g in range(SG):
        yg = yv[:, g * gw:(g + 1) * gw]
        outs.append(yg * lax.rsqrt(jnp.mean(yg * yg, axis=-1, keepdims=True) + RMS_EPS) * nw_ref[:, g * gw:(g + 1) * gw])
    y_ref[...] = jnp.concatenate(outs, axis=1).astype(BF)
    xbuf[0:8, :] = xbuf[L:L + 8, :]

    @pl.when(n == pl.num_programs(1) - 1)
    def _():
        st_ref[...] = s_ref[...]
        tail_ref[...] = xbuf[pl.ds(L + 5, CW - 1), :]


def ssd_prompt(xbc, zz, small, lp, nb, seq, chunk, z_col):
    nc = seq // chunk
    heads = jnp.arange(LANES)[:, None]
    e = (heads == (jnp.arange(SI)[None, :] // SP)).astype(BF)

    def pad_lanes(v, off):
        return jnp.zeros((1, LANES), F32).at[0, off:off + v.shape[0]].set(v)

    full = lambda shape: pl.BlockSpec(shape, lambda b, n: (0,) * len(shape))
    return pl.pallas_call(
        functools.partial(_ssd_kernel, chunk=chunk),
        grid=(nb, nc),
        in_specs=[pl.BlockSpec((chunk, SCONV), lambda b, n: (b * nc + n, 0)),
                  pl.BlockSpec((chunk, SI), lambda b, n: (b * nc + n, z_col)),
                  pl.BlockSpec((chunk, LANES), lambda b, n: (b * nc + n, 0)),
                  full((CW, SCONV)), full((1, SCONV)), full((1, LANES)), full((1, LANES)),
                  full((1, SI)), full((1, SI)), full((LANES, SI))],
        out_specs=[pl.BlockSpec((chunk, SI), lambda b, n: (b * nc + n, 0)),
                   pl.BlockSpec((None, SH * SP, SN), lambda b, n: (b, 0, 0)),
                   pl.BlockSpec((None, CW - 1, SCONV), lambda b, n: (b, 0, 0))],
        out_shape=[jax.ShapeDtypeStruct((nb * seq, SI), BF),
                   jax.ShapeDtypeStruct((nb, SH * SP, SN), F32),
                   jax.ShapeDtypeStruct((nb, CW - 1, SCONV), F32)],
        scratch_shapes=[pltpu.VMEM((chunk + 8, SCONV), F32), pltpu.VMEM((SH * SP, SN), F32)],
        compiler_params=_cparams(("parallel", "arbitrary")),
        name="ssd_chunk_scan",
    )(xbc, zz, small, lp['ssd_conv_w'], lp['ssd_conv_b'].reshape(1, SCONV),
      pad_lanes(lp['ssd_dt_bias'], L_DT), pad_lanes(-jnp.exp(lp['ssd_a_log']), L_DT),
      jnp.repeat(lp['ssd_d'], SP).reshape(1, SI), lp['ssd_norm'].reshape(1, SI), e)


def _gdn_kernel(qkv_ref, z_ref, sm_ref, cw_ref, dtb_ref, na_ref, nw_ref,
                o_ref, st_ref, tail_ref, xbuf, s_ref, *, chunk):
    n = pl.program_id(1)
    C = chunk

    @pl.when(n == 0)
    def _():
        xbuf[0:8, :] = jnp.zeros((8, NCONV), F32)
        s_ref[...] = jnp.zeros(s_ref.shape, F32)

    xbuf[8:8 + C, :] = qkv_ref[...].astype(F32)
    y = cw_ref[0:1, :] * xbuf[pl.ds(5, C), :]
    for t in range(1, CW):
        y = y + cw_ref[t:t + 1, :] * xbuf[pl.ds(5 + t, C), :]
    qkv = _silu(y)

    sm = sm_ref[...]
    beta = _sigmoid(sm)
    g = na_ref[...] * _softplus(sm + dtb_ref[...])
    gpad = jnp.concatenate([g, jnp.zeros((LANES - C, LANES), F32)], axis=0)
    r128 = lax.broadcasted_iota(jnp.int32, (LANES, LANES), 0)
    c128 = lax.broadcasted_iota(jnp.int32, (LANES, LANES), 1)
    gc = _dot_sel((r128 >= c128).astype(BF), gpad)
    gc_t = gc.T
    row = lax.broadcasted_iota(jnp.int32, (C, C), 0)
    col = lax.broadcasted_iota(jnp.int32, (C, C), 1)
    eye = (row == col).astype(F32)
    zf = z_ref[...].astype(F32)
    for h in range(NH):
        sl = slice(h * NK, (h + 1) * NK)
        qh, kh, vh = qkv[:, sl], qkv[:, NKW + h * NK:NKW + (h + 1) * NK], qkv[:, 2 * NKW + h * NV:2 * NKW + (h + 1) * NV]
        qh = qh * lax.rsqrt(jnp.sum(qh * qh, axis=-1, keepdims=True) + RMS_EPS) * (NK ** -0.5)
        kh = kh * lax.rsqrt(jnp.sum(kh * kh, axis=-1, keepdims=True) + RMS_EPS)
        la = L_NA + h
        gcol = gc[0:C, la:la + 1]
        grow = gc_t[la:la + 1, 0:C]
        glast = gc[C - 1:C, la:la + 1]
        bh = beta[:, L_NB + h:L_NB + h + 1]
        dmat = jnp.exp(jnp.where(row >= col, gcol - grow, -1e30))
        kb = kh * bh
        khb = kh.astype(BF)
        x = -jnp.where(row > col, _dot_nt(kb.astype(BF), khb) * dmat, 0.0)
        tm = eye + x
        xj = _dot3(x, x)
        tm = tm + _dot3(tm, xj)
        span = 4
        while span < C:
            xjb = xj.astype(BF)
            xj = _dot(xjb, xjb)
            tm = tm + _dot(tm.astype(BF), xj.astype(BF))
            span *= 2
        rhs = jnp.concatenate([vh * bh, kb * jnp.exp(gcol)], axis=1)
        sol = _dot3(tm, rhs)
        u, w = sol[:, :NV], sol[:, NV:]
        s_old = s_ref[h]
        sb = s_old.astype(BF)
        v_new = u - _dot(w.astype(BF), sb)
        vnb = v_new.astype(BF)
        qk = _dot_nt(qh.astype(BF), khb) * dmat
        o = _dot((qh * jnp.exp(gcol)).astype(BF), sb) + _dot(qk.astype(BF), vnb)
        kd = (kh * jnp.exp(glast - gcol)).astype(BF)
        s_ref[h] = s_old * jnp.exp(glast) + _dot_tn(kd, vnb)
        o = o * lax.rsqrt(jnp.mean(o * o, axis=-1, keepdims=True) + RMS_EPS) * nw_ref[...]
        o_ref[:, sl] = (o * _silu(zf[:, sl])).astype(BF)
    xbuf[0:8, :] = xbuf[C:C + 8, :]

    @pl.when(n == pl.num_programs(1) - 1)
    def _():
        st_ref[...] = s_ref[...]
        tail_ref[...] = xbuf[pl.ds(C + 5, CW - 1), :]


def gdn_prompt(qkv, zz, small, lp, nb, seq, chunk, z_col):
    nc = seq // chunk

    def pad_lanes(v, off):
        return jnp.zeros((1, LANES), F32).at[0, off:off + v.shape[0]].set(v)

    full = lambda shape: pl.BlockSpec(shape, lambda b, n: (0,) * len(shape))
    return pl.pallas_call(
        functools.partial(_gdn_kernel, chunk=chunk),
        grid=(nb, nc),
        in_specs=[pl.BlockSpec((chunk, NCONV), lambda b, n: (b * nc + n, 0)),
                  pl.BlockSpec((chunk, NH * NV), lambda b, n: (b * nc + n, z_col)),
                  pl.BlockSpec((chunk, LANES), lambda b, n: (b * nc + n, 0)),
                  full((CW, NCONV)), full((1, LANES)), full((1, LANES)), full((1, NV))],
        out_specs=[pl.BlockSpec((chunk, NH * NV), lambda b, n: (b * nc + n, 0)),
                   pl.BlockSpec((None, NH, NK, NV), lambda b, n: (b, 0, 0, 0)),
                   pl.BlockSpec((None, CW - 1, NCONV), lambda b, n: (b, 0, 0))],
        out_shape=[jax.ShapeDtypeStruct((nb * seq, NH * NV), BF),
                   jax.ShapeDtypeStruct((nb, NH, NK, NV), F32),
                   jax.ShapeDtypeStruct((nb, CW - 1, NCONV), F32)],
        scratch_shapes=[pltpu.VMEM((chunk + 8, NCONV), F32), pltpu.VMEM((NH, NK, NV), F32)],
        compiler_params=_cparams(("parallel", "arbitrary")),
        name="gdn_chunk_scan",
    )(qkv, zz, small, lp['dn_conv_w'], pad_lanes(lp['dn_dt_bias'], L_NA),
      pad_lanes(-jnp.exp(lp['dn_a_log']), L_NA), lp['dn_norm'].reshape(1, NV))


def _gdn_stage_kernel(qkv_ref, z_ref, sm_ref, cw_ref, dtb_ref, na_ref, nw_ref,
                      o_ref, st_ref, tail_ref, xbuf, s_ref, *, chunk, nb):
    n = pl.program_id(0)
    C = chunk

    @pl.when(n == 0)
    def _():
        xbuf[:, 0:8, :] = jnp.zeros((nb, 8, NCONV), F32)
        s_ref[...] = jnp.zeros(s_ref.shape, F32)

    r128 = lax.broadcasted_iota(jnp.int32, (LANES, LANES), 0)
    c128 = lax.broadcasted_iota(jnp.int32, (LANES, LANES), 1)
    tri128 = (r128 >= c128).astype(BF)
    row = lax.broadcasted_iota(jnp.int32, (C, C), 0)
    col = lax.broadcasted_iota(jnp.int32, (C, C), 1)
    eye = (row == col).astype(F32)
    ch = []
    for b in range(nb):
        xbuf[b, 8:8 + C, :] = qkv_ref[b].astype(F32)
        y = cw_ref[0:1, :] * xbuf[b, pl.ds(5, C), :]
        for t in range(1, CW):
            y = y + cw_ref[t:t + 1, :] * xbuf[b, pl.ds(5 + t, C), :]
        qkv = _silu(y)
        sm = sm_ref[b]
        beta = _sigmoid(sm)
        g = na_ref[...] * _softplus(sm + dtb_ref[...])
        gc = _dot_sel(tri128, jnp.concatenate([g, jnp.zeros((LANES - C, LANES), F32)], axis=0))
        gc_t = gc.T
        for h in range(NH):
            qh = qkv[:, h * NK:(h + 1) * NK]
            kh = qkv[:, NKW + h * NK:NKW + (h + 1) * NK]
            vh = qkv[:, 2 * NKW + h * NV:2 * NKW + (h + 1) * NV]
            qh = qh * lax.rsqrt(jnp.sum(qh * qh, axis=-1, keepdims=True) + RMS_EPS) * (NK ** -0.5)
            kh = kh * lax.rsqrt(jnp.sum(kh * kh, axis=-1, keepdims=True) + RMS_EPS)
            la = L_NA + h
            gcol = gc[0:C, la:la + 1]
            grow = gc_t[la:la + 1, 0:C]
            glast = gc[C - 1:C, la:la + 1]
            bh = beta[:, L_NB + h:L_NB + h + 1]
            kb = kh * bh
            eg = jnp.exp(gcol)
            ch.append(dict(
                b=b, h=h, glast=glast,
                dmat=jnp.exp(jnp.where(row >= col, gcol - grow, -1e30)),
                kbb=kb.astype(BF), khb=kh.astype(BF), qhb=qh.astype(BF),
                rhs=jnp.concatenate([vh * bh, kb * eg], axis=1),
                qg=(qh * eg).astype(BF), kd=(kh * jnp.exp(glast - gcol)).astype(BF)))
    for c in ch:
        c['x'] = -jnp.where(row > col, _dot_nt(c['kbb'], c['khb']) * c['dmat'], 0.0)
        c['qk'] = (_dot_nt(c['qhb'], c['khb']) * c['dmat']).astype(BF)
    for c in ch:
        c['p'] = _dot3(c['x'], c['x'])
    span = 2
    first = True
    while span < C:
        for c in ch:
            tm = (eye + c['x']) if first else c['t']
            if first:
                c['t'] = tm + _dot3(tm, c['p'])
            else:
                c['t'] = tm + _dot(tm.astype(BF), c['p'].astype(BF))
            if span * 2 < C:
                pb = c['p'].astype(BF)
                c['p'] = _dot(pb, pb)
        first = False
        span *= 2
    for c in ch:
        c['sol'] = _dot3(c['t'], c['rhs'])
    for c in ch:
        s_old = s_ref[c['b'], c['h']]
        sb = s_old.astype(BF)
        c['s_old'] = s_old
        c['vn'] = (c['sol'][:, :NV] - _dot(c['sol'][:, NV:].astype(BF), sb)).astype(BF)
        c['oq'] = _dot(c['qg'], sb)
    for c in ch:
        b, h = c['b'], c['h']
        o = c['oq'] + _dot(c['qk'], c['vn'])
        s_ref[b, h] = c['s_old'] * jnp.exp(c['glast']) + _dot_tn(c['kd'], c['vn'])
        o = o * lax.rsqrt(jnp.mean(o * o, axis=-1, keepdims=True) + RMS_EPS) * nw_ref[...]
        sl = slice(h * NV, (h + 1) * NV)
        o_ref[b, :, sl] = (o * _silu(z_ref[b, :, sl].astype(F32))).astype(BF)
    xbuf[:, 0:8, :] = xbuf[:, C:C + 8, :]

    @pl.when(n == pl.num_programs(0) - 1)
    def _():
        st_ref[...] = s_ref[...]
        tail_ref[...] = xbuf[:, pl.ds(C + 5, CW - 1), :]


def gdn_prompt_staged(qkv, zz, small, lp, nb, seq, chunk, z_col):
    nc = seq // chunk

    def pad_lanes(v, off):
        return jnp.zeros((1, LANES), F32).at[0, off:off + v.shape[0]].set(v)

    full = lambda shape: pl.BlockSpec(shape, lambda n: (0,) * len(shape))
    return pl.pallas_call(
        functools.partial(_gdn_stage_kernel, chunk=chunk, nb=nb),
        grid=(nc,),
        in_specs=[pl.BlockSpec((nb, chunk, NCONV), lambda n: (0, n, 0)),
                  pl.BlockSpec((nb, chunk, NH * NV), lambda n: (0, n, z_col)),
                  pl.BlockSpec((nb, chunk, LANES), lambda n: (0, n, 0)),
                  full((CW, NCONV)), full((1, LANES)), full((1, LANES)), full((1, NV))],
        out_specs=[pl.BlockSpec((nb, chunk, NH * NV), lambda n: (0, n, 0)),
                   full((nb, NH, NK, NV)), full((nb, CW - 1, NCONV))],
        out_shape=[jax.ShapeDtypeStruct((nb, seq, NH * NV), BF),
                   jax.ShapeDtypeStruct((nb, NH, NK, NV), F32),
                   jax.ShapeDtypeStruct((nb, CW - 1, NCONV), F32)],
        scratch_shapes=[pltpu.VMEM((nb, chunk + 8, NCONV), F32), pltpu.VMEM((nb, NH, NK, NV), F32)],
        compiler_params=_cparams(("arbitrary",)),
        name="gdn_chunk_scan",
    )(qkv, zz, small, lp['dn_conv_w'], pad_lanes(lp['dn_dt_bias'], L_NA),
      pad_lanes(-jnp.exp(lp['dn_a_log']), L_NA), lp['dn_norm'].reshape(1, NV))


def _merge_kernel(x_ref, ao_ref, so_ref, no_ref, g0_ref, g1_ref, g2_ref, wa_ref, ws_ref, wn_ref, wo_ref,
                  lg_ref, lb_ref, wr_ref, wsg_ref, wsu_ref, wsd_ref, x1_ref, x1b_ref, r_ref, lgt_ref):
    def gate(ref):
        return _sigmoid(ref[...].astype(F32))

    mixed = (gate(g0_ref) * _dot(ao_ref[...].astype(BF), wa_ref[...])
             + gate(g1_ref) * _dot(so_ref[...].astype(BF), ws_ref[...])
             + gate(g2_ref) * _dot(no_ref[...].astype(BF), wn_ref[...]))
    yv = ALPHA * x_ref[...] + _dot(mixed.astype(BF), wo_ref[...])
    mu = jnp.mean(yv, axis=-1, keepdims=True)
    yc = yv - mu
    var = jnp.mean(yc * yc, axis=-1, keepdims=True)
    x1 = yc * lax.rsqrt(var + LN_EPS) * lg_ref[...] + lb_ref[...]
    x1b = x1.astype(BF)
    x1_ref[...] = x1
    x1b_ref[...] = x1b
    lgt_ref[...] = _dot(x1b, wr_ref[...])
    hsh = _silu(_dot(x1b, wsg_ref[...])) * _dot(x1b, wsu_ref[...])
    r_ref[...] = ALPHA * x1 + _dot(hsh.astype(BF), wsd_ref[...])


def merge_post(x, ao, so, no, gates, gate_col, wts, tm):
    m = x.shape[0]
    tm = min(tm, m)
    rowb = lambda c: pl.BlockSpec((tm, D), lambda i, c=c: (i, c))
    full = lambda a: pl.BlockSpec(a.shape, lambda i: (0,) * a.ndim)
    return pl.pallas_call(
        _merge_kernel,
        grid=(m // tm,),
        in_specs=[rowb(0), rowb(0), rowb(0), rowb(0), rowb(gate_col), rowb(gate_col + 1), rowb(gate_col + 2)]
        + [full(a) for a in wts],
        out_specs=[rowb(0), rowb(0), rowb(0), pl.BlockSpec((tm, LANES), lambda i: (i, 0))],
        out_shape=[jax.ShapeDtypeStruct((m, D), F32), jax.ShapeDtypeStruct((m, D), BF),
                   jax.ShapeDtypeStruct((m, D), F32), jax.ShapeDtypeStruct((m, LANES), F32)],
        compiler_params=_cparams(("parallel",)),
        name="merge_ln_router_shared",
    )(x, ao, so, no, gates, gates, gates, *wts)


def _moe_kernel(be_ref, nu_ref, x_ref, wg_ref, wu_ref, wd_ref, o_ref):
    i = pl.program_id(0)

    @pl.when(i < nu_ref[0])
    def _():
        x = x_ref[...]
        hid = _silu(_dot(x, wg_ref[...])) * _dot(x, wu_ref[...])
        o_ref[...] = _dot(hid.astype(BF), wd_ref[...]).astype(o_ref.dtype)

    @pl.when(i >= nu_ref[0])
    def _():
        o_ref[...] = jnp.zeros(o_ref.shape, o_ref.dtype)


def moe_experts(xb, block_expert, n_used, wg, wu, wd, bm, out_dtype):
    n_rows = xb.shape[0]
    ff = wg.shape[2]
    grid_spec = pltpu.PrefetchScalarGridSpec(
        num_scalar_prefetch=2,
        grid=(n_rows // bm,),
        in_specs=[pl.BlockSpec((bm, D), lambda i, be, nu: (i, 0)),
                  pl.BlockSpec((None, D, ff), lambda i, be, nu: (be[i], 0, 0)),
                  pl.BlockSpec((None, D, ff), lambda i, be, nu: (be[i], 0, 0)),
                  pl.BlockSpec((None, ff, D), lambda i, be, nu: (be[i], 0, 0))],
        out_specs=pl.BlockSpec((bm, D), lambda i, be, nu: (i, 0)),
    )
    return pl.pallas_call(
        _moe_kernel,
        grid_spec=grid_spec,
        out_shape=jax.ShapeDtypeStruct((n_rows, D), out_dtype),
        compiler_params=_cparams(("arbitrary",)),
        name="moe_grouped_ffn",
    )(block_expert, n_used, xb, wg, wu, wd)


def _route(logits, router_bias):
    t = logits.shape[0]
    scores = jax.nn.sigmoid(logits)
    biased = scores + router_bias
    group_score = lax.top_k(biased.reshape(t, NGRP, NE // NGRP), 2)[0].sum(-1)
    _, top_groups = lax.top_k(group_score, TOPG)
    group_mask = jax.nn.one_hot(top_groups, NGRP, dtype=F32).sum(1) > 0
    expert_mask = jnp.repeat(group_mask, NE // NGRP, axis=1)
    _, idx = lax.top_k(jnp.where(expert_mask, biased, -jnp.inf), TOPK)
    gate = jnp.take_along_axis(scores, idx, axis=1)
    gate = gate / jnp.sum(gate, -1, keepdims=True) * RSCALE
    return idx.astype(jnp.int32), gate


def _dispatch(idx, bm):
    t = idx.shape[0]
    na = t * TOPK
    assert na % bm == 0
    flat_e = idx.reshape(na)
    order = jnp.argsort(flat_e).astype(jnp.int32)
    sorted_e = flat_e[order]
    counts = jnp.zeros((NE,), jnp.int32).at[flat_e].add(1)
    padded = (counts + bm - 1) // bm * bm
    pad_end = jnp.cumsum(padded)
    pad_start = pad_end - padded
    start = jnp.cumsum(counts) - counts
    dest = pad_start[sorted_e] + jnp.arange(na, dtype=jnp.int32) - start[sorted_e]
    n_blocks = na // bm + NE
    buf_tok = jnp.zeros((n_blocks * bm,), jnp.int32).at[dest].set(order // TOPK)
    block_expert = jnp.minimum(
        jnp.searchsorted(pad_end, jnp.arange(n_blocks, dtype=jnp.int32) * bm, side='right'), NE - 1).astype(jnp.int32)
    n_used = (pad_end[-1:] // bm).astype(jnp.int32)
    pos = jnp.zeros((na,), jnp.int32).at[order].set(dest).reshape(t, TOPK)
    return buf_tok, block_expert, n_used, pos


def moe_routed(x1b, logits, lp, bm):
    idx, gate = _route(logits, lp['router_bias'])
    buf_tok, block_expert, n_used, pos = _dispatch(idx, bm)
    yb = moe_experts(x1b[buf_tok], block_expert, n_used, lp['w_exp_gate'], lp['w_exp_up'], lp['w_exp_down'], bm, BF)
    return jnp.einsum('tjd,tj->td', yb[pos].astype(F32), gate)


def _layer_norm(x, g, b):
    mu = jnp.mean(x, -1, keepdims=True)
    var = jnp.mean(jnp.square(x - mu), -1, keepdims=True)
    return (x - mu) * lax.rsqrt(var + LN_EPS) * g + b


def _rms(x, g):
    return x * lax.rsqrt(jnp.mean(x * x, -1, keepdims=True) + RMS_EPS) * g


def _prep_layer(l, w_in, p):
    wi = w_in[l]
    small = jnp.concatenate([wi[:, R_DT:R_NQKV], wi[:, R_NB:R_GATE], jnp.zeros((D, LANES - 32), wi.dtype)], axis=1)
    lp = {k: v[l] for k, v in p.items()}
    lp.update(
        w_qkv=wi[:, R_Q:R_Z].astype(BF),
        w_zg=jnp.concatenate([wi[:, R_Z:R_XBC], wi[:, R_NZ:R_NB], wi[:, R_GATE:]], axis=1).astype(BF),
        w_xbc=wi[:, R_XBC:R_DT].astype(BF),
        w_nqkv=wi[:, R_NQKV:R_NZ].astype(BF),
        w_small=small.astype(BF),
        w_router_p=jnp.concatenate([lp['w_router'], jnp.zeros((D, LANES - NE), F32)], axis=1).astype(BF),
    )
    for k in ('w_attn_o', 'w_ssd_o', 'w_dn_o', 'w_out', 'w_sh_gate', 'w_sh_up', 'w_sh_down',
              'w_exp_gate', 'w_exp_up', 'w_exp_down'):
        lp[k] = lp[k].astype(BF)
    return lp


def _merge_weights(lp):
    return [lp['w_attn_o'], lp['w_ssd_o'], lp['w_dn_o'], lp['w_out'], lp['ln1_g'].reshape(1, D),
            lp['ln1_b'].reshape(1, D), lp['w_router_p'], lp['w_sh_gate'], lp['w_sh_up'], lp['w_sh_down']]


def _finish(r, x1b, logits, lp, bm):
    routed = moe_routed(x1b, logits[:, :NE], lp, bm)
    return _layer_norm(r + routed, lp['ln2_g'], lp['ln2_b'])


def prompt_layer(x, lp, layer_idx, tabs):
    lam_init = 0.8 - 0.6 * math.exp(-0.3 * layer_idx)
    q, kf, kb, vf, vb = qkv_project(x, lp['w_qkv'], tabs, 512)
    zg = matmul(x, lp['w_zg'], BF)
    xbc = matmul(x, lp['w_xbc'], BF)
    nqkv = matmul(x, lp['w_nqkv'], BF)
    small = matmul(x, lp['w_small'], F32)
    ao = flash_diff_attention(q, kb, vb, lp['attn_lambda'], lp['attn_subln'], lam_init, NB, SEQ, ATT_BLK)
    so, ssd_h, ssd_tail = ssd_prompt(xbc, zg, small, lp, NB, SEQ, SSD_CHUNK, 0)
    no, dn_s, dn_tail = gdn_prompt_staged(nqkv.reshape(NB, SEQ, NCONV), zg.reshape(NB, SEQ, -1),
                                          small.reshape(NB, SEQ, LANES), lp, NB, SEQ, GDN_CHUNK, 1)
    no = no.reshape(NB * SEQ, NH * NV)
    x1, x1b, r, logits = merge_post(x, ao, so, no, zg, 2, _merge_weights(lp), 512)
    x2 = _finish(r, x1b, logits, lp, MOE_BLK_PROMPT)
    return (x2, kf.reshape(NB, SEQ, KVH, 2 * HD), vf.reshape(NB, SEQ, KVH, 2 * HD),
            ssd_h.reshape(NB, SH, SP, SN), ssd_tail, dn_s, dn_tail)


def _conv_step(hist, new, w, b=None):
    xp = jnp.concatenate([hist, new[:, None, :]], axis=1)
    y = jnp.sum(xp * w[None], axis=1)
    if b is not None:
        y = y + b
    return jax.nn.silu(y), xp[:, 1:]


def decode_layer(x, lp, layer_idx, tabs, cache_k, cache_v, page_table, ssd_h0, ssd_hist, dn_s0, dn_hist):
    lam_init = 0.8 - 0.6 * math.exp(-0.3 * layer_idx)
    q, kf, _, vf, _ = qkv_project(x, lp['w_qkv'], tabs, DB)
    zg = matmul(x, lp['w_zg'], F32, tm=DB)
    xbc_raw = matmul(x, lp['w_xbc'], F32, tm=DB)
    nqkv_raw = matmul(x, lp['w_nqkv'], F32, tm=DB)
    small = matmul(x, lp['w_small'], F32, tm=DB)
    qh = q.reshape(DB, KVH, 2, 2, HD)
    qbd = jnp.einsum('bhgcd,hk,ce->bhgcked', qh, jnp.eye(KVH, dtype=BF),
                     jnp.eye(2, dtype=BF)).reshape(DB, 16, KVH * 2 * HD)
    oc = paged_attention(qbd, kf, vf, cache_k, cache_v, page_table, layer_idx, PAGES_PER_STEP)
    oc = oc.reshape(DB, KVH, 2, 2, KVH, 2 * HD)
    oc = jnp.stack([oc[:, h, :, :, h] for h in range(KVH)], axis=1)
    lq = lp['attn_lambda']
    lam = jnp.exp(jnp.sum(lq[0] * lq[1])) - jnp.exp(jnp.sum(lq[2] * lq[3])) + lam_init
    o = oc[:, :, :, 0] - lam * oc[:, :, :, 1]
    ao = (_rms(o, lp['attn_subln']) * (1.0 - lam_init)).reshape(DB, AH * 2 * HD)
    xbc, ssd_tail = _conv_step(ssd_hist, xbc_raw, lp['ssd_conv_w'], lp['ssd_conv_b'])
    xs = xbc[:, :SI].reshape(DB, SH, SP)
    bm = jnp.repeat(xbc[:, SI:SI + SG * SN].reshape(DB, SG, SN), SH // SG, axis=1)
    cm = jnp.repeat(xbc[:, SI + SG * SN:].reshape(DB, SG, SN), SH // SG, axis=1)
    dt = jax.nn.softplus(small[:, L_DT:L_DT + SH] + lp['ssd_dt_bias'])
    a_coef = -jnp.exp(lp['ssd_a_log'])
    ssd_h = ssd_h0 * jnp.exp(dt * a_coef)[..., None, None] + (xs * dt[..., None])[..., :, None] * bm[..., None, :]
    ys = jnp.sum(ssd_h * cm[..., None, :], -1) + lp['ssd_d'][:, None] * xs
    gw = SI // SG
    ys = ys.reshape(DB, SG, gw) * jax.nn.silu(zg[:, :SI]).reshape(DB, SG, gw)
    so = _rms(ys, lp['ssd_norm'].reshape(SG, gw)).reshape(DB, SI)
    qkv, dn_tail = _conv_step(dn_hist, nqkv_raw, lp['dn_conv_w'])
    nq = qkv[:, :NKW].reshape(DB, NH, NK)
    nk = qkv[:, NKW:2 * NKW].reshape(DB, NH, NK)
    nv = qkv[:, 2 * NKW:].reshape(DB, NH, NV)
    nq = nq * lax.rsqrt(jnp.sum(nq * nq, -1, keepdims=True) + RMS_EPS) * (NK ** -0.5)
    nk = nk * lax.rsqrt(jnp.sum(nk * nk, -1, keepdims=True) + RMS_EPS)
    beta = jax.nn.sigmoid(small[:, L_NB:L_NB + NH])
    gl = -jnp.exp(lp['dn_a_log']) * jax.nn.softplus(small[:, L_NA:L_NA + NH] + lp['dn_dt_bias'])
    s = dn_s0 * jnp.exp(gl)[..., None, None]
    delta = (nv - jnp.sum(s * nk[..., :, None], -2)) * beta[..., None]
    dn_s = s + nk[..., :, None] * delta[..., None, :]
    on = jnp.sum(dn_s * nq[..., :, None], -2)
    no = (_rms(on, lp['dn_norm']) * jax.nn.silu(zg[:, SI:2 * SI].reshape(DB, NH, NV))).reshape(DB, NH * NV)
    x1, x1b, r, logits = merge_post(x, ao, so, no, zg, 2, _merge_weights(lp), DB)
    x2 = _finish(r, x1b, logits, lp, MOE_BLK_DECODE)
    return (x2, kf.reshape(DB, 1, KVH, 2 * HD), vf.reshape(DB, 1, KVH, 2 * HD), ssd_h, ssd_tail, dn_s, dn_tail)


def kernel(x_prompt, x_sample, cache_k, cache_v, state_ssd, state_ssd_conv, state_dn, state_dn_conv, page_table,
           w_in, attn_lambda, attn_subln, w_attn_o, ssd_conv_w, ssd_conv_b, ssd_dt_bias, ssd_a_log, ssd_d, ssd_norm,
           w_ssd_o, dn_conv_w, dn_dt_bias, dn_a_log, dn_norm, w_dn_o, w_out, ln1_g, ln1_b, w_router, router_bias,
           w_exp_gate, w_exp_up, w_exp_down, w_sh_gate, w_sh_up, w_sh_down, ln2_g, ln2_b):
    params = dict(attn_lambda=attn_lambda, attn_subln=attn_subln, w_attn_o=w_attn_o, ssd_conv_w=ssd_conv_w,
                  ssd_conv_b=ssd_conv_b, ssd_dt_bias=ssd_dt_bias, ssd_a_log=ssd_a_log, ssd_d=ssd_d,
                  ssd_norm=ssd_norm, w_ssd_o=w_ssd_o, dn_conv_w=dn_conv_w, dn_dt_bias=dn_dt_bias,
                  dn_a_log=dn_a_log, dn_norm=dn_norm, w_dn_o=w_dn_o, w_out=w_out, ln1_g=ln1_g, ln1_b=ln1_b,
                  w_router=w_router, router_bias=router_bias, w_exp_gate=w_exp_gate, w_exp_up=w_exp_up,
                  w_exp_down=w_exp_down, w_sh_gate=w_sh_gate, w_sh_up=w_sh_up, w_sh_down=w_sh_down,
                  ln2_g=ln2_g, ln2_b=ln2_b)
    tabs_p = _rope_tables(jnp.arange(SEQ, dtype=jnp.int32))
    tabs_d = _rope_tables(jnp.full((DB,), PAST, dtype=jnp.int32))
    xp = x_prompt.reshape(NB * SEQ, D)
    xs = x_sample.reshape(DB, D)
    outs = [[] for _ in range(12)]
    for l in range(DEPTH):
        lp = _prep_layer(l, w_in, params)
        xp, kp, vp, hp, cp, sp, dp = prompt_layer(xp, lp, l, tabs_p)
        xs, ks, vs, hs, cs, ss, ds = decode_layer(xs, lp, l, tabs_d, cache_k, cache_v, page_table,
                                                  state_ssd[l], state_ssd_conv[l], state_dn[l], state_dn_conv[l])
        for lst, val in zip(outs, (kp, vp, ks, vs, hp, cp, hs, cs, sp, dp, ss, ds)):
            lst.append(val)
    return (xp.reshape(NB, SEQ, D), xs.reshape(DB, 1, D)) + tuple(jnp.stack(o) for o in outs)
```

```python
import functools
import math

import jax
import jax.numpy as jnp
from jax import lax
from jax.experimental import pallas as pl
from jax.experimental.pallas import tpu as pltpu

D = 1024
NB, SEQ = 4, 4096
DEPTH = 2
DB = 32
PAST = 16384
PAGE = 128
AH, KVH, HD = 8, 4, 64
ROT = 16
THETA = 500000.0
SH, SP, SI, SG, SN = 16, 64, 1024, 2, 128
SCONV = SI + 2 * SG * SN
NH, NK, NV = 8, 128, 128
NKW = 1024
NCONV = 3072
CW = 4
NE, TOPK, NGRP, TOPG = 64, 8, 8, 4
RSCALE = 2.5
ALPHA = (2 * DEPTH) ** 0.25
LN_EPS = 1e-5
RMS_EPS = 1e-6

R_Q, R_K, R_V, R_Z, R_XBC, R_DT, R_NQKV, R_NZ, R_NB, R_NA, R_GATE = (
    0, 1024, 1536, 2048, 3072, 4608, 4624, 7696, 8720, 8728, 8736)
L_DT, L_NB, L_NA = 0, 16, 24

LANES = 128
VMEM_LIMIT = 56 * 1024 * 1024
SSD_CHUNK = 128
GDN_CHUNK = 64
ATT_BLK = 512
FLASH_ROWS = 32
PAGES_PER_STEP = 16
MOE_BLK_PROMPT = 512
MOE_BLK_DECODE = 16

BF = jnp.bfloat16
F32 = jnp.float32


def _cparams(sem):
    return pltpu.CompilerParams(dimension_semantics=sem, vmem_limit_bytes=VMEM_LIMIT)


def _dot(a, b):
    return jnp.dot(a, b, preferred_element_type=F32)


def _dot_nt(a, b):
    return lax.dot_general(a, b, (((1,), (1,)), ((), ())), preferred_element_type=F32)


def _dot_tn(a, b):
    return lax.dot_general(a, b, (((0,), (0,)), ((), ())), preferred_element_type=F32)


def _split2(x):
    hi = x.astype(BF)
    lo = (x - hi.astype(F32)).astype(BF)
    return hi, lo


def _split3(x):
    hi = x.astype(BF)
    r = x - hi.astype(F32)
    mid = r.astype(BF)
    lo = (r - mid.astype(F32)).astype(BF)
    return hi, mid, lo


def _dot_sel(sel, x):
    hi, mid, lo = _split3(x)
    return _dot(sel, hi) + _dot(sel, mid) + _dot(sel, lo)


def _dot_x_sel(x, sel):
    hi, mid, lo = _split3(x)
    return _dot(hi, sel) + _dot(mid, sel) + _dot(lo, sel)


def _dot3(a, b):
    ah, al = _split2(a)
    bh, bl = _split2(b)
    return _dot(ah, bh) + _dot(ah, bl) + _dot(al, bh)


def _sigmoid(x):
    return 1.0 / (1.0 + jnp.exp(-x))


def _silu(x):
    return x * _sigmoid(x)


def _softplus(x):
    return jnp.maximum(x, 0.0) + jnp.log(1.0 + jnp.exp(-jnp.abs(x)))


def _mm_kernel(x_ref, w_ref, o_ref, xs_ref):
    @pl.when(pl.program_id(1) == 0)
    def _():
        xs_ref[...] = x_ref[...].astype(BF)

    o_ref[...] = _dot(xs_ref[...], w_ref[...]).astype(o_ref.dtype)


def matmul(x, w, out_dtype, tm=512, tn=512):
    m, k = x.shape
    n = w.shape[1]
    tm, tn = min(tm, m), min(tn, n)
    assert m % tm == 0 and n % tn == 0, (m, n, tm, tn)
    return pl.pallas_call(
        _mm_kernel,
        grid=(m // tm, n // tn),
        in_specs=[pl.BlockSpec((tm, k), lambda i, j: (i, 0)),
                  pl.BlockSpec((k, tn), lambda i, j: (0, j))],
        out_specs=pl.BlockSpec((tm, tn), lambda i, j: (i, j)),
        out_shape=jax.ShapeDtypeStruct((m, n), out_dtype),
        scratch_shapes=[pltpu.VMEM((tm, k), BF)],
        compiler_params=_cparams(("parallel", "arbitrary")),
        name="proj_matmul",
    )(x, w)


def _rope_tables(pos):
    half = ROT // 2
    inv = THETA ** (-jnp.arange(half, dtype=F32) * 2.0 / ROT)
    ang = pos.astype(F32)[:, None] * inv
    cos, sin = jnp.cos(ang), jnp.sin(ang)
    t = pos.shape[0]
    one, zero, z8 = jnp.ones((t, HD - ROT), F32), jnp.zeros((t, HD - ROT), F32), jnp.zeros((t, half), F32)
    c = jnp.concatenate([cos, cos, one, cos, cos, one], 1)
    sa = jnp.concatenate([-sin, z8, zero, -sin, z8, zero], 1)
    sb = jnp.concatenate([z8, sin, zero, z8, sin, zero], 1)
    return c, sa, sb


def _qkv_kernel(x_ref, w_ref, c_ref, sa_ref, sb_ref, q_ref, kf_ref, kb_ref, vf_ref, vb_ref):
    acc = _dot(x_ref[...].astype(BF), w_ref[...])
    c, sa, sb = c_ref[...], sa_ref[...], sb_ref[...]

    def rot(xg):
        return xg * c + pltpu.roll(xg, LANES - ROT // 2, 1) * sa + pltpu.roll(xg, ROT // 2, 1) * sb

    for h in range(AH):
        sl = slice(h * LANES, (h + 1) * LANES)
        q_ref[:, sl] = (rot(acc[:, sl]) * (HD ** -0.5)).astype(BF)
    for h in range(KVH):
        sl = slice(h * LANES, (h + 1) * LANES)
        kr = rot(acc[:, R_K + h * LANES:R_K + (h + 1) * LANES])
        kf_ref[:, sl] = kr
        kb_ref[:, sl] = kr.astype(BF)
    v = acc[:, R_V:R_Z]
    vf_ref[...] = v
    vb_ref[...] = v.astype(BF)


def qkv_project(x, w, tabs, tm):
    m = x.shape[0]
    tm = min(tm, m)
    nt = tabs[0].shape[0] // tm
    kw = KVH * 2 * HD
    tab_spec = pl.BlockSpec((tm, LANES), lambda i: (i % nt, 0))
    return pl.pallas_call(
        _qkv_kernel,
        grid=(m // tm,),
        in_specs=[pl.BlockSpec((tm, D), lambda i: (i, 0)),
                  pl.BlockSpec((D, R_Z), lambda i: (0, 0)),
                  tab_spec, tab_spec, tab_spec],
        out_specs=[pl.BlockSpec((tm, AH * 2 * HD), lambda i: (i, 0)),
                   pl.BlockSpec((tm, kw), lambda i: (i, 0)), pl.BlockSpec((tm, kw), lambda i: (i, 0)),
                   pl.BlockSpec((tm, kw), lambda i: (i, 0)), pl.BlockSpec((tm, kw), lambda i: (i, 0))],
        out_shape=[jax.ShapeDtypeStruct((m, AH * 2 * HD), BF),
                   jax.ShapeDtypeStruct((m, kw), F32), jax.ShapeDtypeStruct((m, kw), BF),
                   jax.ShapeDtypeStruct((m, kw), F32), jax.ShapeDtypeStruct((m, kw), BF)],
        compiler_params=_cparams(("parallel",)),
        name="qkv_rope",
    )(x, w, *tabs)


def _lambda_value(lq, lam_init):
    a = jnp.sum(lq[0:1, :] * lq[1:2, :], axis=-1, keepdims=True)
    b = jnp.sum(lq[2:3, :] * lq[3:4, :], axis=-1, keepdims=True)
    return jnp.exp(a) - jnp.exp(b) + lam_init


def _flash_kernel(lq_ref, sub_ref, q_ref, k_ref, v_ref, o_ref, qs_ref, m_ref, acc_ref, s_ref, p_ref, *, blk, lam_init):
    qi, ki = pl.program_id(2), pl.program_id(3)

    @pl.when(ki == 0)
    def _():
        q = q_ref[...]
        lane = lax.broadcasted_iota(jnp.int32, (blk, LANES), 1)
        for c in range(2):
            msk = (lane < HD) if c == 0 else (lane >= HD)
            for g in range(2):
                qs_ref[c, g * blk:(g + 1) * blk, :] = jnp.where(msk, q[:, g * LANES:(g + 1) * LANES], 0)
        m_ref[...] = jnp.full(m_ref.shape, -jnp.inf, F32)
        acc_ref[...] = jnp.zeros(acc_ref.shape, F32)

    def update(masked):
        k = k_ref[...]
        vext = jnp.concatenate([v_ref[...], jnp.ones((blk, LANES), BF)], axis=1)
        units = [(c, g) for g in range(2) for c in range(2)]
        for u, (c, g) in enumerate(units):
            s_ref[u] = _dot_nt(qs_ref[c, g * blk:(g + 1) * blk, :], k)
        if masked:
            keep = (lax.broadcasted_iota(jnp.int32, (blk, blk), 1) <= lax.broadcasted_iota(jnp.int32, (blk, blk), 0))
            diff = (lax.broadcasted_iota(jnp.int32, (FLASH_ROWS, blk), 1)
                    - lax.broadcasted_iota(jnp.int32, (FLASH_ROWS, blk), 0))
        for u, (c, g) in enumerate(units):
            rows = slice(g * blk, (g + 1) * blk)
            s_full = s_ref[u]
            if masked:
                s_full = jnp.where(keep, s_full, -jnp.inf)
            m_old = m_ref[c, rows]
            m_new = jnp.maximum(m_old, jnp.max(s_full, axis=-1, keepdims=True))
            alpha = jnp.exp(m_old - m_new)
            for r in range(0, blk, FLASH_ROWS):
                s = s_ref[u, r:r + FLASH_ROWS, :]
                if masked:
                    s = jnp.where(diff <= r, s, -jnp.inf)
                p_ref[u, r:r + FLASH_ROWS, :] = jnp.exp(s - m_new[r:r + FLASH_ROWS]).astype(BF)
            m_ref[c, rows] = m_new
            acc_ref[c, rows] = alpha * acc_ref[c, rows] + _dot(p_ref[u], vext)

    @pl.when(ki < qi)
    def _():
        update(False)

    @pl.when(ki == qi)
    def _():
        update(True)
        lam = _lambda_value(lq_ref[...], lam_init)
        a0, a1 = acc_ref[0], acc_ref[1]
        o = a0[:, :LANES] / a0[:, LANES:] - lam * (a1[:, :LANES] / a1[:, LANES:])
        o = o * lax.rsqrt(jnp.mean(o * o, axis=-1, keepdims=True) + RMS_EPS) * sub_ref[...] * (1.0 - lam_init)
        o_ref[:, 0:LANES] = o[0:blk].astype(BF)
        o_ref[:, LANES:2 * LANES] = o[blk:2 * blk].astype(BF)


def flash_diff_attention(q, k, v, lq, subln, lam_init, nb, seq, blk):
    nblk = seq // blk
    kern = functools.partial(_flash_kernel, blk=blk, lam_init=lam_init)
    return pl.pallas_call(
        kern,
        grid=(nb, KVH, nblk, nblk),
        in_specs=[pl.BlockSpec((4, HD), lambda b, h, i, j: (0, 0)),
                  pl.BlockSpec((1, LANES), lambda b, h, i, j: (0, 0)),
                  pl.BlockSpec((blk, 2 * LANES), lambda b, h, i, j: (b * nblk + i, h)),
                  pl.BlockSpec((blk, LANES), lambda b, h, i, j: (b * nblk + jnp.minimum(i, j), h)),
                  pl.BlockSpec((blk, LANES), lambda b, h, i, j: (b * nblk + jnp.minimum(i, j), h))],
        out_specs=pl.BlockSpec((blk, 2 * LANES), lambda b, h, i, j: (b * nblk + i, h)),
        out_shape=jax.ShapeDtypeStruct((nb * seq, AH * 2 * HD), BF),
        scratch_shapes=[pltpu.VMEM((2, 2 * blk, LANES), BF), pltpu.VMEM((2, 2 * blk, 1), F32),
                        pltpu.VMEM((2, 2 * blk, 2 * LANES), F32),
                        pltpu.VMEM((4, blk, blk), F32), pltpu.VMEM((4, blk, blk), BF)],
        compiler_params=_cparams(("parallel", "parallel", "parallel", "arbitrary")),
        name="flash_diff_attn",
    )(lq, subln.reshape(1, LANES), q, k, v)


def _paged_kernel(pt_ref, q_ref, ks_ref, vs_ref, *refs, pps):
    kp, vp = refs[:pps], refs[pps:2 * pps]
    o_ref, m_ref, l_ref, acc_ref = refs[2 * pps:]
    j = pl.program_id(1)
    q = q_ref[...]
    pw = PAGE * KVH

    @pl.when(j == 0)
    def _():
        ks = ks_ref[...].astype(BF).astype(F32)
        m_ref[...] = jnp.sum(q.astype(F32) * ks, axis=-1, keepdims=True)
        l_ref[...] = jnp.ones(l_ref.shape, F32)
        acc_ref[...] = vs_ref[...].astype(BF).astype(F32)

    row_head = lax.broadcasted_iota(jnp.int32, (16, pw), 0) // 4
    col_head = lax.broadcasted_iota(jnp.int32, (16, pw), 1) % KVH
    own = row_head == col_head
    s = jnp.concatenate([jnp.where(own, _dot_nt(q, kp[i][...].astype(BF)), -jnp.inf) for i in range(pps)], axis=1)
    m_old = m_ref[...]
    m_new = jnp.maximum(m_old, jnp.max(s, axis=-1, keepdims=True))
    alpha = jnp.exp(m_old - m_new)
    p = jnp.exp(s - m_new)
    l_ref[...] = alpha * l_ref[...] + jnp.sum(p, axis=-1, keepdims=True)
    pv = _dot(p[:, 0:pw].astype(BF), vp[0][...].astype(BF))
    for i in range(1, pps):
        pv = pv + _dot(p[:, i * pw:(i + 1) * pw].astype(BF), vp[i][...].astype(BF))
    acc_ref[...] = alpha * acc_ref[...] + pv
    m_ref[...] = m_new

    @pl.when(j == pl.num_programs(1) - 1)
    def _():
        o_ref[...] = acc_ref[...] / l_ref[...]


def paged_attention(q16, k16, v16, cache_k, cache_v, page_table, layer, pps):
    nseq, npg = page_table.shape
    pw = PAGE * KVH
    ck = cache_k.reshape(cache_k.shape[0], cache_k.shape[1], pw, 2 * HD)
    cv = cache_v.reshape(cache_v.shape[0], cache_v.shape[1], pw, 2 * HD)

    def page_spec(i):
        return pl.BlockSpec((None, None, pw, 2 * HD), lambda b, j, pt: (layer, pt[b, j * pps + i], 0, 0))

    row = pl.BlockSpec((None, 16, 2 * HD), lambda b, j, pt: (b, 0, 0))
    grid_spec = pltpu.PrefetchScalarGridSpec(
        num_scalar_prefetch=1,
        grid=(nseq, npg // pps),
        in_specs=[row, row, row] + [page_spec(i) for i in range(pps)] + [page_spec(i) for i in range(pps)],
        out_specs=row,
        scratch_shapes=[pltpu.VMEM((16, 1), F32), pltpu.VMEM((16, 1), F32), pltpu.VMEM((16, 2 * HD), F32)],
    )
    return pl.pallas_call(
        functools.partial(_paged_kernel, pps=pps),
        grid_spec=grid_spec,
        out_shape=jax.ShapeDtypeStruct((nseq, 16, 2 * HD), F32),
        compiler_params=_cparams(("parallel", "arbitrary")),
        name="paged_diff_attn",
    )(page_table, q16, k16, v16, *([ck] * pps), *([cv] * pps))


def _ssd_kernel(xbc_ref, z_ref, sm_ref, cw_ref, cb_ref, dtb_ref, ac_ref, dx_ref, nw_ref, e_ref,
                y_ref, st_ref, tail_ref, xbuf, s_ref, *, chunk):
    n = pl.program_id(1)
    L = chunk

    @pl.when(n == 0)
    def _():
        xbuf[0:8, :] = jnp.zeros((8, SCONV), F32)
        s_ref[...] = jnp.zeros(s_ref.shape, F32)

    xbuf[8:8 + L, :] = xbc_ref[...].astype(F32)
    y = cb_ref[...] + cw_ref[0:1, :] * xbuf[pl.ds(5, L), :]
    for t in range(1, CW):
        y = y + cw_ref[t:t + 1, :] * xbuf[pl.ds(5 + t, L), :]
    xbc = _silu(y)
    xs, bm, cm = xbc[:, :SI], xbc[:, SI:SI + SG * SN], xbc[:, SI + SG * SN:]

    dt = _softplus(sm_ref[...] + dtb_ref[...])
    a = dt * ac_ref[...]
    row = lax.broadcasted_iota(jnp.int32, (L, L), 0)
    col = lax.broadcasted_iota(jnp.int32, (L, L), 1)
    causal = row >= col
    acum = _dot_sel(causal.astype(BF), a)
    acum_t = acum.T
    alast = acum[L - 1:L, :]
    e = e_ref[...]
    dt_e = _dot_x_sel(dt, e)
    ea_e = _dot_x_sel(jnp.exp(acum), e)
    dend_e = _dot_x_sel(jnp.exp(alast - acum), e)
    xdt = xs * dt_e
    xd = (xdt * dend_e).astype(BF)
    lane = lax.broadcasted_iota(jnp.int32, (L, LANES), 1)
    gw = SI // SG
    hpg = SH // SG
    y_parts = []
    for g in range(SG):
        cg = cm[:, g * SN:(g + 1) * SN].astype(BF)
        bg = bm[:, g * SN:(g + 1) * SN].astype(BF)
        s_old = s_ref[g * gw:(g + 1) * gw, :]
        y_inter = _dot_nt(cg, s_old.astype(BF))
        cb = _dot_nt(cg, bg)
        pairs = []
        for jp in range(hpg // 2):
            h0 = g * hpg + 2 * jp
            xp = xdt[:, h0 * SP:h0 * SP + LANES]
            acc = None
            for d in range(2):
                h = h0 + d
                seg = jnp.where(causal, jnp.exp(acum[:, h:h + 1] - acum_t[h:h + 1, :]), 0.0)
                mh = (cb * seg).astype(BF)
                xm = jnp.where((lane < SP) if d == 0 else (lane >= SP), xp, 0.0).astype(BF)
                t = _dot(mh, xm)
                acc = t if acc is None else acc + t
            pairs.append(acc)
        y_intra = jnp.concatenate(pairs, axis=1)
        y_parts.append(y_intra + y_inter * ea_e[:, g * gw:(g + 1) * gw])
        upd = _dot_tn(xd[:, g * gw:(g + 1) * gw], bg)
        for hh in range(hpg):
            h = g * hpg + hh
            r0 = h * SP
            s_ref[r0:r0 + SP, :] = (s_old[hh * SP:(hh + 1) * SP, :] * jnp.exp(alast[:, h:h + 1])
                                    + upd[hh * SP:(hh + 1) * SP, :])
    yv = jnp.concatenate(y_parts, axis=1) + dx_ref[...] * xs
    yv = yv * _silu(z_ref[...].astype(F32))
    outs = []
    for g in range(SG):
        yg = yv[:, g * gw:(g + 1) * gw]
        outs.append(yg * lax.rsqrt(jnp.mean(yg * yg, axis=-1, keepdims=True) + RMS_EPS) * nw_ref[:, g * gw:(g + 1) * gw])
    y_ref[...] = jnp.concatenate(outs, axis=1).astype(BF)
    xbuf[0:8, :] = xbuf[L:L + 8, :]

    @pl.when(n == pl.num_programs(1) - 1)
    def _():
        st_ref[...] = s_ref[...]
        tail_ref[...] = xbuf[pl.ds(L + 5, CW - 1), :]


def ssd_prompt(xbc, zz, small, lp, nb, seq, chunk, z_col):
    nc = seq // chunk
    heads = jnp.arange(LANES)[:, None]
    e = (heads == (jnp.arange(SI)[None, :] // SP)).astype(BF)

    def pad_lanes(v, off):
        return jnp.zeros((1, LANES), F32).at[0, off:off + v.shape[0]].set(v)

    full = lambda shape: pl.BlockSpec(shape, lambda b, n: (0,) * len(shape))
    return pl.pallas_call(
        functools.partial(_ssd_kernel, chunk=chunk),
        grid=(nb, nc),
        in_specs=[pl.BlockSpec((chunk, SCONV), lambda b, n: (b * nc + n, 0)),
                  pl.BlockSpec((chunk, SI), lambda b, n: (b * nc + n, z_col)),
                  pl.BlockSpec((chunk, LANES), lambda b, n: (b * nc + n, 0)),
                  full((CW, SCONV)), full((1, SCONV)), full((1, LANES)), full((1, LANES)),
                  full((1, SI)), full((1, SI)), full((LANES, SI))],
        out_specs=[pl.BlockSpec((chunk, SI), lambda b, n: (b * nc + n, 0)),
                   pl.BlockSpec((None, SH * SP, SN), lambda b, n: (b, 0, 0)),
                   pl.BlockSpec((None, CW - 1, SCONV), lambda b, n: (b, 0, 0))],
        out_shape=[jax.ShapeDtypeStruct((nb * seq, SI), BF),
                   jax.ShapeDtypeStruct((nb, SH * SP, SN), F32),
                   jax.ShapeDtypeStruct((nb, CW - 1, SCONV), F32)],
        scratch_shapes=[pltpu.VMEM((chunk + 8, SCONV), F32), pltpu.VMEM((SH * SP, SN), F32)],
        compiler_params=_cparams(("parallel", "arbitrary")),
        name="ssd_chunk_scan",
    )(xbc, zz, small, lp['ssd_conv_w'], lp['ssd_conv_b'].reshape(1, SCONV),
      pad_lanes(lp['ssd_dt_bias'], L_DT), pad_lanes(-jnp.exp(lp['ssd_a_log']), L_DT),
      jnp.repeat(lp['ssd_d'], SP).reshape(1, SI), lp['ssd_norm'].reshape(1, SI), e)


def _gdn_stage_kernel(qkv_ref, z_ref, sm_ref, cw_ref, dtb_ref, na_ref, nw_ref,
                      o_ref, st_ref, tail_ref, xbuf, s_ref, *, chunk, nb):
    n = pl.program_id(0)
    C = chunk

    @pl.when(n == 0)
    def _():
        xbuf[:, 0:8, :] = jnp.zeros((nb, 8, NCONV), F32)
        s_ref[...] = jnp.zeros(s_ref.shape, F32)

    r128 = lax.broadcasted_iota(jnp.int32, (LANES, LANES), 0)
    c128 = lax.broadcasted_iota(jnp.int32, (LANES, LANES), 1)
    tri128 = (r128 >= c128).astype(BF)
    row = lax.broadcasted_iota(jnp.int32, (C, C), 0)
    col = lax.broadcasted_iota(jnp.int32, (C, C), 1)
    eye = (row == col).astype(F32)
    ch = []
    for b in range(nb):
        xbuf[b, 8:8 + C, :] = qkv_ref[b].astype(F32)
        y = cw_ref[0:1, :] * xbuf[b, pl.ds(5, C), :]
        for t in range(1, CW):
            y = y + cw_ref[t:t + 1, :] * xbuf[b, pl.ds(5 + t, C), :]
        qkv = _silu(y)
        sm = sm_ref[b]
        beta = _sigmoid(sm)
        g = na_ref[...] * _softplus(sm + dtb_ref[...])
        gc = _dot_sel(tri128, jnp.concatenate([g, jnp.zeros((LANES - C, LANES), F32)], axis=0))
        gc_t = gc.T
        for h in range(NH):
            qh = qkv[:, h * NK:(h + 1) * NK]
            kh = qkv[:, NKW + h * NK:NKW + (h + 1) * NK]
            vh = qkv[:, 2 * NKW + h * NV:2 * NKW + (h + 1) * NV]
            qh = qh * lax.rsqrt(jnp.sum(qh * qh, axis=-1, keepdims=True) + RMS_EPS) * (NK ** -0.5)
            kh = kh * lax.rsqrt(jnp.sum(kh * kh, axis=-1, keepdims=True) + RMS_EPS)
            la = L_NA + h
            gcol = gc[0:C, la:la + 1]
            grow = gc_t[la:la + 1, 0:C]
            glast = gc[C - 1:C, la:la + 1]
            bh = beta[:, L_NB + h:L_NB + h + 1]
            kb = kh * bh
            eg = jnp.exp(gcol)
            ch.append(dict(
                b=b, h=h, glast=glast,
                dmat=jnp.exp(jnp.where(row >= col, gcol - grow, -1e30)),
                kbb=kb.astype(BF), khb=kh.astype(BF), qhb=qh.astype(BF),
                rhs=jnp.concatenate([vh * bh, kb * eg], axis=1),
                qg=(qh * eg).astype(BF), kd=(kh * jnp.exp(glast - gcol)).astype(BF)))
    for c in ch:
        c['x'] = -jnp.where(row > col, _dot_nt(c['kbb'], c['khb']) * c['dmat'], 0.0)
        c['qk'] = (_dot_nt(c['qhb'], c['khb']) * c['dmat']).astype(BF)
    for c in ch:
        c['p'] = _dot3(c['x'], c['x'])
    span = 2
    first = True
    while span < C:
        for c in ch:
            tm = (eye + c['x']) if first else c['t']
            if first:
                c['t'] = tm + _dot3(tm, c['p'])
            else:
                c['t'] = tm + _dot(tm.astype(BF), c['p'].astype(BF))
            if span * 2 < C:
                pb = c['p'].astype(BF)
                c['p'] = _dot(pb, pb)
        first = False
        span *= 2
    for c in ch:
        c['sol'] = _dot3(c['t'], c['rhs'])
    for c in ch:
        s_old = s_ref[c['b'], c['h']]
        sb = s_old.astype(BF)
        c['s_old'] = s_old
        c['vn'] = (c['sol'][:, :NV] - _dot(c['sol'][:, NV:].astype(BF), sb)).astype(BF)
        c['oq'] = _dot(c['qg'], sb)
    for c in ch:
        b, h = c['b'], c['h']
        o = c['oq'] + _dot(c['qk'], c['vn'])
        s_ref[b, h] = c['s_old'] * jnp.exp(c['glast']) + _dot_tn(c['kd'], c['vn'])
        o = o * lax.rsqrt(jnp.mean(o * o, axis=-1, keepdims=True) + RMS_EPS) * nw_ref[...]
        sl = slice(h * NV, (h + 1) * NV)
        o_ref[b, :, sl] = (o * _silu(z_ref[b, :, sl].astype(F32))).astype(BF)
    xbuf[:, 0:8, :] = xbuf[:, C:C + 8, :]

    @pl.when(n == pl.num_programs(0) - 1)
    def _():
        st_ref[...] = s_ref[...]
        tail_ref[...] = xbuf[:, pl.ds(C + 5, CW - 1), :]


def gdn_prompt_staged(qkv, zz, small, lp, nb, seq, chunk, z_col):
    nc = seq // chunk

    def pad_lanes(v, off):
        return jnp.zeros((1, LANES), F32).at[0, off:off + v.shape[0]].set(v)

    full = lambda shape: pl.BlockSpec(shape, lambda n: (0,) * len(shape))
    return pl.pallas_call(
        functools.partial(_gdn_stage_kernel, chunk=chunk, nb=nb),
        grid=(nc,),
        in_specs=[pl.BlockSpec((nb, chunk, NCONV), lambda n: (0, n, 0)),
                  pl.BlockSpec((nb, chunk, NH * NV), lambda n: (0, n, z_col)),
                  pl.BlockSpec((nb, chunk, LANES), lambda n: (0, n, 0)),
                  full((CW, NCONV)), full((1, LANES)), full((1, LANES)), full((1, NV))],
        out_specs=[pl.BlockSpec((nb, chunk, NH * NV), lambda n: (0, n, 0)),
                   full((nb, NH, NK, NV)), full((nb, CW - 1, NCONV))],
        out_shape=[jax.ShapeDtypeStruct((nb, seq, NH * NV), BF),
                   jax.ShapeDtypeStruct((nb, NH, NK, NV), F32),
                   jax.ShapeDtypeStruct((nb, CW - 1, NCONV), F32)],
        scratch_shapes=[pltpu.VMEM((nb, chunk + 8, NCONV), F32), pltpu.VMEM((nb, NH, NK, NV), F32)],
        compiler_params=_cparams(("arbitrary",)),
        name="gdn_chunk_scan",
    )(qkv, zz, small, lp['dn_conv_w'], pad_lanes(lp['dn_dt_bias'], L_NA),
      pad_lanes(-jnp.exp(lp['dn_a_log']), L_NA), lp['dn_norm'].reshape(1, NV))


def _merge_kernel(x_ref, ao_ref, so_ref, no_ref, g0_ref, g1_ref, g2_ref, wa_ref, ws_ref, wn_ref, wo_ref,
                  lg_ref, lb_ref, wr_ref, wsg_ref, wsu_ref, wsd_ref, x1_ref, x1b_ref, r_ref, lgt_ref):
    def gate(ref):
        return _sigmoid(ref[...].astype(F32))

    mixed = (gate(g0_ref) * _dot(ao_ref[...].astype(BF), wa_ref[...])
             + gate(g1_ref) * _dot(so_ref[...].astype(BF), ws_ref[...])
             + gate(g2_ref) * _dot(no_ref[...].astype(BF), wn_ref[...]))
    yv = ALPHA * x_ref[...] + _dot(mixed.astype(BF), wo_ref[...])
    mu = jnp.mean(yv, axis=-1, keepdims=True)
    yc = yv - mu
    var = jnp.mean(yc * yc, axis=-1, keepdims=True)
    x1 = yc * lax.rsqrt(var + LN_EPS) * lg_ref[...] + lb_ref[...]
    x1b = x1.astype(BF)
    x1_ref[...] = x1
    x1b_ref[...] = x1b
    lgt_ref[...] = _dot(x1b, wr_ref[...])
    hsh = _silu(_dot(x1b, wsg_ref[...])) * _dot(x1b, wsu_ref[...])
    r_ref[...] = ALPHA * x1 + _dot(hsh.astype(BF), wsd_ref[...])


def merge_post(x, ao, so, no, gates, gate_col, wts, tm):
    m = x.shape[0]
    tm = min(tm, m)
    rowb = lambda c: pl.BlockSpec((tm, D), lambda i, c=c: (i, c))
    full = lambda a: pl.BlockSpec(a.shape, lambda i: (0,) * a.ndim)
    return pl.pallas_call(
        _merge_kernel,
        grid=(m // tm,),
        in_specs=[rowb(0), rowb(0), rowb(0), rowb(0), rowb(gate_col), rowb(gate_col + 1), rowb(gate_col + 2)]
        + [full(a) for a in wts],
        out_specs=[rowb(0), rowb(0), rowb(0), pl.BlockSpec((tm, LANES), lambda i: (i, 0))],
        out_shape=[jax.ShapeDtypeStruct((m, D), F32), jax.ShapeDtypeStruct((m, D), BF),
                   jax.ShapeDtypeStruct((m, D), F32), jax.ShapeDtypeStruct((m, LANES), F32)],
        compiler_params=_cparams(("parallel",)),
        name="merge_ln_router_shared",
    )(x, ao, so, no, gates, gates, gates, *wts)


def _moe_kernel(be_ref, nu_ref, x_ref, wg_ref, wu_ref, wd_ref, o_ref):
    i = pl.program_id(0)

    @pl.when(i < nu_ref[0])
    def _():
        x = x_ref[...]
        hid = _silu(_dot(x, wg_ref[...])) * _dot(x, wu_ref[...])
        o_ref[...] = _dot(hid.astype(BF), wd_ref[...]).astype(o_ref.dtype)

    @pl.when(i >= nu_ref[0])
    def _():
        o_ref[...] = jnp.zeros(o_ref.shape, o_ref.dtype)


def moe_experts(xb, block_expert, n_used, wg, wu, wd, bm, out_dtype):
    n_rows = xb.shape[0]
    ff = wg.shape[2]
    grid_spec = pltpu.PrefetchScalarGridSpec(
        num_scalar_prefetch=2,
        grid=(n_rows // bm,),
        in_specs=[pl.BlockSpec((bm, D), lambda i, be, nu: (i, 0)),
                  pl.BlockSpec((None, D, ff), lambda i, be, nu: (be[i], 0, 0)),
                  pl.BlockSpec((None, D, ff), lambda i, be, nu: (be[i], 0, 0)),
                  pl.BlockSpec((None, ff, D), lambda i, be, nu: (be[i], 0, 0))],
        out_specs=pl.BlockSpec((bm, D), lambda i, be, nu: (i, 0)),
    )
    return pl.pallas_call(
        _moe_kernel,
        grid_spec=grid_spec,
        out_shape=jax.ShapeDtypeStruct((n_rows, D), out_dtype),
        compiler_params=_cparams(("arbitrary",)),
        name="moe_grouped_ffn",
    )(block_expert, n_used, xb, wg, wu, wd)


def _route(logits, router_bias):
    t = logits.shape[0]
    scores = jax.nn.sigmoid(logits)
    biased = scores + router_bias
    group_score = lax.top_k(biased.reshape(t, NGRP, NE // NGRP), 2)[0].sum(-1)
    _, top_groups = lax.top_k(group_score, TOPG)
    group_mask = jax.nn.one_hot(top_groups, NGRP, dtype=F32).sum(1) > 0
    expert_mask = jnp.repeat(group_mask, NE // NGRP, axis=1)
    _, idx = lax.top_k(jnp.where(expert_mask, biased, -jnp.inf), TOPK)
    gate = jnp.take_along_axis(scores, idx, axis=1)
    gate = gate / jnp.sum(gate, -1, keepdims=True) * RSCALE
    return idx.astype(jnp.int32), gate


def _dispatch(idx, bm):
    t = idx.shape[0]
    na = t * TOPK
    assert na % bm == 0
    onehot = idx[:, :, None] == jnp.arange(NE, dtype=jnp.int32)[None, None, :]
    mask = jnp.any(onehot, axis=1).astype(jnp.int32)
    cum = jnp.cumsum(mask, axis=0)
    counts = cum[-1]
    padded = (counts + bm - 1) // bm * bm
    pad_end = jnp.cumsum(padded)
    dest_te = (pad_end - padded)[None, :] + cum - mask
    pos = jnp.sum(jnp.where(onehot, dest_te[:, None, :], 0), axis=-1).astype(jnp.int32)
    n_blocks = na // bm + NE
    tok = jnp.broadcast_to(jnp.arange(t, dtype=jnp.int32)[:, None], (t, TOPK))
    buf_tok = jnp.zeros((n_blocks * bm,), jnp.int32).at[pos.reshape(na)].set(tok.reshape(na), unique_indices=True)
    block_expert = jnp.minimum(
        jnp.searchsorted(pad_end, jnp.arange(n_blocks, dtype=jnp.int32) * bm, side='right'), NE - 1).astype(jnp.int32)
    n_used = (pad_end[-1:] // bm).astype(jnp.int32)
    return buf_tok, block_expert, n_used, pos


def _combine_kernel(yg_ref, gate_ref, r_ref, g_ref, b_ref, o_ref):
    acc = r_ref[...]
    gate = gate_ref[...]
    for j in range(TOPK):
        acc = acc + yg_ref[:, j * D:(j + 1) * D].astype(F32) * gate[:, j:j + 1]
    mu = jnp.mean(acc, axis=-1, keepdims=True)
    yc = acc - mu
    var = jnp.mean(yc * yc, axis=-1, keepdims=True)
    o_ref[...] = yc * lax.rsqrt(var + LN_EPS) * g_ref[...] + b_ref[...]


def combine_ln(yg, gate, r, g, b, tm):
    t = r.shape[0]
    tm = min(tm, t)
    gate_p = jnp.concatenate([gate, jnp.zeros((t, LANES - TOPK), F32)], axis=1)
    return pl.pallas_call(
        _combine_kernel,
        grid=(t // tm,),
        in_specs=[pl.BlockSpec((tm, TOPK * D), lambda i: (i, 0)), pl.BlockSpec((tm, LANES), lambda i: (i, 0)),
                  pl.BlockSpec((tm, D), lambda i: (i, 0)), pl.BlockSpec((1, D), lambda i: (0, 0)),
                  pl.BlockSpec((1, D), lambda i: (0, 0))],
        out_specs=pl.BlockSpec((tm, D), lambda i: (i, 0)),
        out_shape=jax.ShapeDtypeStruct((t, D), F32),
        compiler_params=_cparams(("parallel",)),
        name="moe_combine_ln",
    )(yg, gate_p, r, g.reshape(1, D), b.reshape(1, D))


def moe_finish(r, x1b, logits, lp, bm):
    t = r.shape[0]
    idx, gate = _route(logits[:, :NE], lp['router_bias'])
    buf_tok, block_expert, n_used, pos = _dispatch(idx, bm)
    yb = moe_experts(x1b[buf_tok], block_expert, n_used, lp['w_exp_gate'], lp['w_exp_up'], lp['w_exp_down'], bm, BF)
    yg = yb[pos.reshape(t * TOPK)].reshape(t, TOPK * D)
    return combine_ln(yg, gate, r, lp['ln2_g'], lp['ln2_b'], 256)


def _rms(x, g):
    return x * lax.rsqrt(jnp.mean(x * x, -1, keepdims=True) + RMS_EPS) * g


def _prep_layer(l, w_in, p):
    wi = w_in[l]
    small = jnp.concatenate([wi[:, R_DT:R_NQKV], wi[:, R_NB:R_GATE], jnp.zeros((D, LANES - 32), wi.dtype)], axis=1)
    lp = {k: v[l] for k, v in p.items()}
    lp.update(
        w_qkv=wi[:, R_Q:R_Z].astype(BF),
        w_zg=jnp.concatenate([wi[:, R_Z:R_XBC], wi[:, R_NZ:R_NB], wi[:, R_GATE:]], axis=1).astype(BF),
        w_xbc=wi[:, R_XBC:R_DT].astype(BF),
        w_nqkv=wi[:, R_NQKV:R_NZ].astype(BF),
        w_small=small.astype(BF),
        w_router_p=jnp.concatenate([lp['w_router'], jnp.zeros((D, LANES - NE), F32)], axis=1).astype(BF),
    )
    for k in ('w_attn_o', 'w_ssd_o', 'w_dn_o', 'w_out', 'w_sh_gate', 'w_sh_up', 'w_sh_down',
              'w_exp_gate', 'w_exp_up', 'w_exp_down'):
        lp[k] = lp[k].astype(BF)
    return lp


def _merge_weights(lp):
    return [lp['w_attn_o'], lp['w_ssd_o'], lp['w_dn_o'], lp['w_out'], lp['ln1_g'].reshape(1, D),
            lp['ln1_b'].reshape(1, D), lp['w_router_p'], lp['w_sh_gate'], lp['w_sh_up'], lp['w_sh_down']]


def prompt_layer(x, lp, layer_idx, tabs):
    lam_init = 0.8 - 0.6 * math.exp(-0.3 * layer_idx)
    q, kf, kb, vf, vb = qkv_project(x, lp['w_qkv'], tabs, 512)
    zg = matmul(x, lp['w_zg'], BF)
    xbc = matmul(x, lp['w_xbc'], BF)
    nqkv = matmul(x, lp['w_nqkv'], BF)
    small = matmul(x, lp['w_small'], F32)
    ao = flash_diff_attention(q, kb, vb, lp['attn_lambda'], lp['attn_subln'], lam_init, NB, SEQ, ATT_BLK)
    so, ssd_h, ssd_tail = ssd_prompt(xbc, zg, small, lp, NB, SEQ, SSD_CHUNK, 0)
    no, dn_s, dn_tail = gdn_prompt_staged(nqkv.reshape(NB, SEQ, NCONV), zg.reshape(NB, SEQ, -1),
                                          small.reshape(NB, SEQ, LANES), lp, NB, SEQ, GDN_CHUNK, 1)
    no = no.reshape(NB * SEQ, NH * NV)
    x1, x1b, r, logits = merge_post(x, ao, so, no, zg, 2, _merge_weights(lp), 512)
    x2 = moe_finish(r, x1b, logits, lp, MOE_BLK_PROMPT)
    return (x2, kf.reshape(NB, SEQ, KVH, 2 * HD), vf.reshape(NB, SEQ, KVH, 2 * HD),
            ssd_h.reshape(NB, SH, SP, SN), ssd_tail, dn_s, dn_tail)


def _conv_step(hist, new, w, b=None):
    xp = jnp.concatenate([hist, new[:, None, :]], axis=1)
    y = jnp.sum(xp * w[None], axis=1)
    if b is not None:
        y = y + b
    return jax.nn.silu(y), xp[:, 1:]


def decode_layer(x, lp, layer_idx, tabs, cache_k, cache_v, page_table, ssd_h0, ssd_hist, dn_s0, dn_hist):
    lam_init = 0.8 - 0.6 * math.exp(-0.3 * layer_idx)
    q, kf, _, vf, _ = qkv_project(x, lp['w_qkv'], tabs, DB)
    zg = matmul(x, lp['w_zg'], F32, tm=DB)
    xbc_raw = matmul(x, lp['w_xbc'], F32, tm=DB)
    nqkv_raw = matmul(x, lp['w_nqkv'], F32, tm=DB)
    small = matmul(x, lp['w_small'], F32, tm=DB)
    qh = q.reshape(DB, KVH, 2, 2, HD)
    q16 = jnp.einsum('bhgcd,ce->bhgced', qh, jnp.eye(2, dtype=BF)).reshape(DB, 16, 2 * HD)
    k16 = jnp.repeat(kf.reshape(DB, KVH, 2 * HD), 4, axis=1)
    v16 = jnp.repeat(vf.reshape(DB, KVH, 2 * HD), 4, axis=1)
    oc = paged_attention(q16, k16, v16, cache_k, cache_v, page_table, layer_idx, PAGES_PER_STEP)
    oc = oc.reshape(DB, AH, 2, 2 * HD)
    lq = lp['attn_lambda']
    lam = jnp.exp(jnp.sum(lq[0] * lq[1])) - jnp.exp(jnp.sum(lq[2] * lq[3])) + lam_init
    o = oc[:, :, 0] - lam * oc[:, :, 1]
    ao = (_rms(o, lp['attn_subln']) * (1.0 - lam_init)).reshape(DB, AH * 2 * HD)
    xbc, ssd_tail = _conv_step(ssd_hist, xbc_raw, lp['ssd_conv_w'], lp['ssd_conv_b'])
    xs = xbc[:, :SI].reshape(DB, SH, SP)
    bm = jnp.repeat(xbc[:, SI:SI + SG * SN].reshape(DB, SG, SN), SH // SG, axis=1)
    cm = jnp.repeat(xbc[:, SI + SG * SN:].reshape(DB, SG, SN), SH // SG, axis=1)
    dt = jax.nn.softplus(small[:, L_DT:L_DT + SH] + lp['ssd_dt_bias'])
    a_coef = -jnp.exp(lp['ssd_a_log'])
    ssd_h = ssd_h0 * jnp.exp(dt * a_coef)[..., None, None] + (xs * dt[..., None])[..., :, None] * bm[..., None, :]
    ys = jnp.sum(ssd_h * cm[..., None, :], -1) + lp['ssd_d'][:, None] * xs
    gw = SI // SG
    ys = ys.reshape(DB, SG, gw) * jax.nn.silu(zg[:, :SI]).reshape(DB, SG, gw)
    so = _rms(ys, lp['ssd_norm'].reshape(SG, gw)).reshape(DB, SI)
    qkv, dn_tail = _conv_step(dn_hist, nqkv_raw, lp['dn_conv_w'])
    nq = qkv[:, :NKW].reshape(DB, NH, NK)
    nk = qkv[:, NKW:2 * NKW].reshape(DB, NH, NK)
    nv = qkv[:, 2 * NKW:].reshape(DB, NH, NV)
    nq = nq * lax.rsqrt(jnp.sum(nq * nq, -1, keepdims=True) + RMS_EPS) * (NK ** -0.5)
    nk = nk * lax.rsqrt(jnp.sum(nk * nk, -1, keepdims=True) + RMS_EPS)
    beta = jax.nn.sigmoid(small[:, L_NB:L_NB + NH])
    gl = -jnp.exp(lp['dn_a_log']) * jax.nn.softplus(small[:, L_NA:L_NA + NH] + lp['dn_dt_bias'])
    s = dn_s0 * jnp.exp(gl)[..., None, None]
    delta = (nv - jnp.sum(s * nk[..., :, None], -2)) * beta[..., None]
    dn_s = s + nk[..., :, None] * delta[..., None, :]
    on = jnp.sum(dn_s * nq[..., :, None], -2)
    no = (_rms(on, lp['dn_norm']) * jax.nn.silu(zg[:, SI:2 * SI].reshape(DB, NH, NV))).reshape(DB, NH * NV)
    x1, x1b, r, logits = merge_post(x, ao, so, no, zg, 2, _merge_weights(lp), DB)
    x2 = moe_finish(r, x1b, logits, lp, MOE_BLK_DECODE)
    return (x2, kf.reshape(DB, 1, KVH, 2 * HD), vf.reshape(DB, 1, KVH, 2 * HD), ssd_h, ssd_tail, dn_s, dn_tail)


def kernel(x_prompt, x_sample, cache_k, cache_v, state_ssd, state_ssd_conv, state_dn, state_dn_conv, page_table,
           w_in, attn_lambda, attn_subln, w_attn_o, ssd_conv_w, ssd_conv_b, ssd_dt_bias, ssd_a_log, ssd_d, ssd_norm,
           w_ssd_o, dn_conv_w, dn_dt_bias, dn_a_log, dn_norm, w_dn_o, w_out, ln1_g, ln1_b, w_router, router_bias,
           w_exp_gate, w_exp_up, w_exp_down, w_sh_gate, w_sh_up, w_sh_down, ln2_g, ln2_b):
    params = dict(attn_lambda=attn_lambda, attn_subln=attn_subln, w_attn_o=w_attn_o, ssd_conv_w=ssd_conv_w,
                  ssd_conv_b=ssd_conv_b, ssd_dt_bias=ssd_dt_bias, ssd_a_log=ssd_a_log, ssd_d=ssd_d,
                  ssd_norm=ssd_norm, w_ssd_o=w_ssd_o, dn_conv_w=dn_conv_w, dn_dt_bias=dn_dt_bias,
                  dn_a_log=dn_a_log, dn_norm=dn_norm, w_dn_o=w_dn_o, w_out=w_out, ln1_g=ln1_g, ln1_b=ln1_b,
                  w_router=w_router, router_bias=router_bias, w_exp_gate=w_exp_gate, w_exp_up=w_exp_up,
                  w_exp_down=w_exp_down, w_sh_gate=w_sh_gate, w_sh_up=w_sh_up, w_sh_down=w_sh_down,
                  ln2_g=ln2_g, ln2_b=ln2_b)
    tabs_p = _rope_tables(jnp.arange(SEQ, dtype=jnp.int32))
    tabs_d = _rope_tables(jnp.full((DB,), PAST, dtype=jnp.int32))
    xp = x_prompt.reshape(NB * SEQ, D)
    xs = x_sample.reshape(DB, D)
    outs = [[] for _ in range(12)]
    for l in range(DEPTH):
        lp = _prep_layer(l, w_in, params)
        xp, kp, vp, hp, cp, sp, dp = prompt_layer(xp, lp, l, tabs_p)
        xs, ks, vs, hs, cs, ss, ds = decode_layer(xs, lp, l, tabs_d, cache_k, cache_v, page_table,
                                                  state_ssd[l], state_ssd_conv[l], state_dn[l], state_dn_conv[l])
        for lst, val in zip(outs, (kp, vp, ks, vs, hp, cp, hs, cs, sp, dp, ss, ds)):
            lst.append(val)
    return (xp.reshape(NB, SEQ, D), xs.reshape(DB, 1, D)) + tuple(jnp.stack(o) for o in outs)
```

```python
import functools
import math

import jax
import jax.numpy as jnp
from jax import lax
from jax.experimental import pallas as pl
from jax.experimental.pallas import tpu as pltpu

D = 1024
NB, SEQ = 4, 4096
DEPTH = 2
DB = 32
PAST = 16384
PAGE = 128
AH, KVH, HD = 8, 4, 64
ROT = 16
THETA = 500000.0
SH, SP, SI, SG, SN = 16, 64, 1024, 2, 128
SCONV = SI + 2 * SG * SN
NH, NK, NV = 8, 128, 128
NKW = 1024
NCONV = 3072
CW = 4
NE, TOPK, NGRP, TOPG = 64, 8, 8, 4
RSCALE = 2.5
ALPHA = (2 * DEPTH) ** 0.25
LN_EPS = 1e-5
RMS_EPS = 1e-6

R_Q, R_K, R_V, R_Z, R_XBC, R_DT, R_NQKV, R_NZ, R_NB, R_NA, R_GATE = (
    0, 1024, 1536, 2048, 3072, 4608, 4624, 7696, 8720, 8728, 8736)
L_DT, L_NB, L_NA = 0, 16, 24

LANES = 128
VMEM_LIMIT = 56 * 1024 * 1024
SSD_CHUNK = 128
GDN_CHUNK = 64
ATT_BLK = 512
FLASH_ROWS = 32
PAGES_PER_STEP = 16
MOE_BLK_PROMPT = 512
MOE_BLK_DECODE = 16

BF = jnp.bfloat16
F32 = jnp.float32


def _cparams(sem):
    return pltpu.CompilerParams(dimension_semantics=sem, vmem_limit_bytes=VMEM_LIMIT)


def _dot(a, b):
    return jnp.dot(a, b, preferred_element_type=F32)


def _dot_nt(a, b):
    return lax.dot_general(a, b, (((1,), (1,)), ((), ())), preferred_element_type=F32)


def _dot_tn(a, b):
    return lax.dot_general(a, b, (((0,), (0,)), ((), ())), preferred_element_type=F32)


def _split2(x):
    hi = x.astype(BF)
    lo = (x - hi.astype(F32)).astype(BF)
    return hi, lo


def _split3(x):
    hi = x.astype(BF)
    r = x - hi.astype(F32)
    mid = r.astype(BF)
    lo = (r - mid.astype(F32)).astype(BF)
    return hi, mid, lo


def _dot_sel(sel, x):
    hi, mid, lo = _split3(x)
    return _dot(sel, hi) + _dot(sel, mid) + _dot(sel, lo)


def _dot_x_sel(x, sel):
    hi, mid, lo = _split3(x)
    return _dot(hi, sel) + _dot(mid, sel) + _dot(lo, sel)


def _dot3(a, b):
    ah, al = _split2(a)
    bh, bl = _split2(b)
    return _dot(ah, bh) + _dot(ah, bl) + _dot(al, bh)


def _sigmoid(x):
    return 1.0 / (1.0 + jnp.exp(-x))


def _silu(x):
    return x * _sigmoid(x)


def _softplus(x):
    return jnp.maximum(x, 0.0) + jnp.log(1.0 + jnp.exp(-jnp.abs(x)))


def _mm_kernel(x_ref, w_ref, o_ref, xs_ref):
    @pl.when(pl.program_id(1) == 0)
    def _():
        xs_ref[...] = x_ref[...].astype(BF)

    o_ref[...] = _dot(xs_ref[...], w_ref[...]).astype(o_ref.dtype)


def matmul(x, w, out_dtype, tm=512, tn=512):
    m, k = x.shape
    n = w.shape[1]
    tm, tn = min(tm, m), min(tn, n)
    assert m % tm == 0 and n % tn == 0, (m, n, tm, tn)
    return pl.pallas_call(
        _mm_kernel,
        grid=(m // tm, n // tn),
        in_specs=[pl.BlockSpec((tm, k), lambda i, j: (i, 0)),
                  pl.BlockSpec((k, tn), lambda i, j: (0, j))],
        out_specs=pl.BlockSpec((tm, tn), lambda i, j: (i, j)),
        out_shape=jax.ShapeDtypeStruct((m, n), out_dtype),
        scratch_shapes=[pltpu.VMEM((tm, k), BF)],
        compiler_params=_cparams(("parallel", "arbitrary")),
        name="proj_matmul",
    )(x, w)


def _rope_tables(pos):
    half = ROT // 2
    inv = THETA ** (-jnp.arange(half, dtype=F32) * 2.0 / ROT)
    ang = pos.astype(F32)[:, None] * inv
    cos, sin = jnp.cos(ang), jnp.sin(ang)
    t = pos.shape[0]
    one, zero, z8 = jnp.ones((t, HD - ROT), F32), jnp.zeros((t, HD - ROT), F32), jnp.zeros((t, half), F32)
    c = jnp.concatenate([cos, cos, one, cos, cos, one], 1)
    sa = jnp.concatenate([-sin, z8, zero, -sin, z8, zero], 1)
    sb = jnp.concatenate([z8, sin, zero, z8, sin, zero], 1)
    return c, sa, sb


def _qkv_kernel(x_ref, w_ref, c_ref, sa_ref, sb_ref, q_ref, kf_ref, kb_ref, vf_ref, vb_ref):
    acc = _dot(x_ref[...].astype(BF), w_ref[...])
    c, sa, sb = c_ref[...], sa_ref[...], sb_ref[...]

    def rot(xg):
        return xg * c + pltpu.roll(xg, LANES - ROT // 2, 1) * sa + pltpu.roll(xg, ROT // 2, 1) * sb

    for h in range(AH):
        sl = slice(h * LANES, (h + 1) * LANES)
        q_ref[:, sl] = (rot(acc[:, sl]) * (HD ** -0.5)).astype(BF)
    for h in range(KVH):
        sl = slice(h * LANES, (h + 1) * LANES)
        kr = rot(acc[:, R_K + h * LANES:R_K + (h + 1) * LANES])
        kf_ref[:, sl] = kr
        kb_ref[:, sl] = kr.astype(BF)
    v = acc[:, R_V:R_Z]
    vf_ref[...] = v
    vb_ref[...] = v.astype(BF)


def qkv_project(x, w, tabs, tm):
    m = x.shape[0]
    tm = min(tm, m)
    nt = tabs[0].shape[0] // tm
    kw = KVH * 2 * HD
    tab_spec = pl.BlockSpec((tm, LANES), lambda i: (i % nt, 0))
    return pl.pallas_call(
        _qkv_kernel,
        grid=(m // tm,),
        in_specs=[pl.BlockSpec((tm, D), lambda i: (i, 0)),
                  pl.BlockSpec((D, R_Z), lambda i: (0, 0)),
                  tab_spec, tab_spec, tab_spec],
        out_specs=[pl.BlockSpec((tm, AH * 2 * HD), lambda i: (i, 0)),
                   pl.BlockSpec((tm, kw), lambda i: (i, 0)), pl.BlockSpec((tm, kw), lambda i: (i, 0)),
                   pl.BlockSpec((tm, kw), lambda i: (i, 0)), pl.BlockSpec((tm, kw), lambda i: (i, 0))],
        out_shape=[jax.ShapeDtypeStruct((m, AH * 2 * HD), BF),
                   jax.ShapeDtypeStruct((m, kw), F32), jax.ShapeDtypeStruct((m, kw), BF),
                   jax.ShapeDtypeStruct((m, kw), F32), jax.ShapeDtypeStruct((m, kw), BF)],
        compiler_params=_cparams(("parallel",)),
        name="qkv_rope",
    )(x, w, *tabs)


def _lambda_value(lq, lam_init):
    a = jnp.sum(lq[0:1, :] * lq[1:2, :], axis=-1, keepdims=True)
    b = jnp.sum(lq[2:3, :] * lq[3:4, :], axis=-1, keepdims=True)
    return jnp.exp(a) - jnp.exp(b) + lam_init


def _flash_kernel(lq_ref, sub_ref, q_ref, k_ref, v_ref, o_ref, qs_ref, m_ref, acc_ref, s_ref, p_ref, *, blk, lam_init):
    qi, ki = pl.program_id(2), pl.program_id(3)

    @pl.when(ki == 0)
    def _():
        q = q_ref[...]
        lane = lax.broadcasted_iota(jnp.int32, (blk, LANES), 1)
        for c in range(2):
            msk = (lane < HD) if c == 0 else (lane >= HD)
            for g in range(2):
                qs_ref[c, g * blk:(g + 1) * blk, :] = jnp.where(msk, q[:, g * LANES:(g + 1) * LANES], 0)
        m_ref[...] = jnp.full(m_ref.shape, -jnp.inf, F32)
        acc_ref[...] = jnp.zeros(acc_ref.shape, F32)

    def update(masked):
        k = k_ref[...]
        vext = jnp.concatenate([v_ref[...], jnp.ones((blk, LANES), BF)], axis=1)
        units = [(c, g) for g in range(2) for c in range(2)]
        for u, (c, g) in enumerate(units):
            s_ref[u] = _dot_nt(qs_ref[c, g * blk:(g + 1) * blk, :], k)
        if masked:
            keep = (lax.broadcasted_iota(jnp.int32, (blk, blk), 1) <= lax.broadcasted_iota(jnp.int32, (blk, blk), 0))
            diff = (lax.broadcasted_iota(jnp.int32, (FLASH_ROWS, blk), 1)
                    - lax.broadcasted_iota(jnp.int32, (FLASH_ROWS, blk), 0))
        for u, (c, g) in enumerate(units):
            rows = slice(g * blk, (g + 1) * blk)
            s_full = s_ref[u]
            if masked:
                s_full = jnp.where(keep, s_full, -jnp.inf)
            m_old = m_ref[c, rows]
            m_new = jnp.maximum(m_old, jnp.max(s_full, axis=-1, keepdims=True))
            alpha = jnp.exp(m_old - m_new)
            for r in range(0, blk, FLASH_ROWS):
                s = s_ref[u, r:r + FLASH_ROWS, :]
                if masked:
                    s = jnp.where(diff <= r, s, -jnp.inf)
                p_ref[u, r:r + FLASH_ROWS, :] = jnp.exp(s - m_new[r:r + FLASH_ROWS]).astype(BF)
            m_ref[c, rows] = m_new
            acc_ref[c, rows] = alpha * acc_ref[c, rows] + _dot(p_ref[u], vext)

    @pl.when(ki < qi)
    def _():
        update(False)

    @pl.when(ki == qi)
    def _():
        update(True)
        lam = _lambda_value(lq_ref[...], lam_init)
        a0, a1 = acc_ref[0], acc_ref[1]
        o = a0[:, :LANES] / a0[:, LANES:] - lam * (a1[:, :LANES] / a1[:, LANES:])
        o = o * lax.rsqrt(jnp.mean(o * o, axis=-1, keepdims=True) + RMS_EPS) * sub_ref[...] * (1.0 - lam_init)
        o_ref[:, 0:LANES] = o[0:blk].astype(BF)
        o_ref[:, LANES:2 * LANES] = o[blk:2 * blk].astype(BF)


def flash_diff_attention(q, k, v, lq, subln, lam_init, nb, seq, blk):
    nblk = seq // blk
    kern = functools.partial(_flash_kernel, blk=blk, lam_init=lam_init)
    return pl.pallas_call(
        kern,
        grid=(nb, KVH, nblk, nblk),
        in_specs=[pl.BlockSpec((4, HD), lambda b, h, i, j: (0, 0)),
                  pl.BlockSpec((1, LANES), lambda b, h, i, j: (0, 0)),
                  pl.BlockSpec((blk, 2 * LANES), lambda b, h, i, j: (b * nblk + i, h)),
                  pl.BlockSpec((blk, LANES), lambda b, h, i, j: (b * nblk + jnp.minimum(i, j), h)),
                  pl.BlockSpec((blk, LANES), lambda b, h, i, j: (b * nblk + jnp.minimum(i, j), h))],
        out_specs=pl.BlockSpec((blk, 2 * LANES), lambda b, h, i, j: (b * nblk + i, h)),
        out_shape=jax.ShapeDtypeStruct((nb * seq, AH * 2 * HD), BF),
        scratch_shapes=[pltpu.VMEM((2, 2 * blk, LANES), BF), pltpu.VMEM((2, 2 * blk, 1), F32),
                        pltpu.VMEM((2, 2 * blk, 2 * LANES), F32),
                        pltpu.VMEM((4, blk, blk), F32), pltpu.VMEM((4, blk, blk), BF)],
        compiler_params=_cparams(("parallel", "parallel", "parallel", "arbitrary")),
        name="flash_diff_attn",
    )(lq, subln.reshape(1, LANES), q, k, v)


def _paged_kernel(pt_ref, q_ref, ks_ref, vs_ref, *refs, pps):
    kp, vp = refs[:pps], refs[pps:2 * pps]
    o_ref, m_ref, l_ref, acc_ref = refs[2 * pps:]
    j = pl.program_id(1)
    q = q_ref[...]
    pw = PAGE * KVH

    @pl.when(j == 0)
    def _():
        ks = ks_ref[...].astype(BF).astype(F32)
        m_ref[...] = jnp.sum(q.astype(F32) * ks, axis=-1, keepdims=True)
        l_ref[...] = jnp.ones(l_ref.shape, F32)
        acc_ref[...] = vs_ref[...].astype(BF).astype(F32)

    row_head = lax.broadcasted_iota(jnp.int32, (16, pw), 0) // 4
    col_head = lax.broadcasted_iota(jnp.int32, (16, pw), 1) % KVH
    own = row_head == col_head
    s = jnp.concatenate([jnp.where(own, _dot_nt(q, kp[i][...].astype(BF)), -jnp.inf) for i in range(pps)], axis=1)
    m_old = m_ref[...]
    m_new = jnp.maximum(m_old, jnp.max(s, axis=-1, keepdims=True))
    alpha = jnp.exp(m_old - m_new)
    p = jnp.exp(s - m_new)
    l_ref[...] = alpha * l_ref[...] + jnp.sum(p, axis=-1, keepdims=True)
    pv = _dot(p[:, 0:pw].astype(BF), vp[0][...].astype(BF))
    for i in range(1, pps):
        pv = pv + _dot(p[:, i * pw:(i + 1) * pw].astype(BF), vp[i][...].astype(BF))
    acc_ref[...] = alpha * acc_ref[...] + pv
    m_ref[...] = m_new

    @pl.when(j == pl.num_programs(1) - 1)
    def _():
        o_ref[...] = acc_ref[...] / l_ref[...]


def paged_attention(q16, k16, v16, cache_k, cache_v, page_table, layer, pps):
    nseq, npg = page_table.shape
    pw = PAGE * KVH
    ck = cache_k.reshape(cache_k.shape[0], cache_k.shape[1], pw, 2 * HD)
    cv = cache_v.reshape(cache_v.shape[0], cache_v.shape[1], pw, 2 * HD)

    def page_spec(i):
        return pl.BlockSpec((None, None, pw, 2 * HD), lambda b, j, pt: (layer, pt[b, j * pps + i], 0, 0))

    row = pl.BlockSpec((None, 16, 2 * HD), lambda b, j, pt: (b, 0, 0))
    grid_spec = pltpu.PrefetchScalarGridSpec(
        num_scalar_prefetch=1,
        grid=(nseq, npg // pps),
        in_specs=[row, row, row] + [page_spec(i) for i in range(pps)] + [page_spec(i) for i in range(pps)],
        out_specs=row,
        scratch_shapes=[pltpu.VMEM((16, 1), F32), pltpu.VMEM((16, 1), F32), pltpu.VMEM((16, 2 * HD), F32)],
    )
    return pl.pallas_call(
        functools.partial(_paged_kernel, pps=pps),
        grid_spec=grid_spec,
        out_shape=jax.ShapeDtypeStruct((nseq, 16, 2 * HD), F32),
        compiler_params=_cparams(("parallel", "arbitrary")),
        name="paged_diff_attn",
    )(page_table, q16, k16, v16, *([ck] * pps), *([cv] * pps))


def _ssd_kernel(xbc_ref, z_ref, sm_ref, cw_ref, cb_ref, dtb_ref, ac_ref, dx_ref, nw_ref, e_ref,
                y_ref, st_ref, tail_ref, xbuf, s_ref, *, chunk):
    n = pl.program_id(1)
    L = chunk

    @pl.when(n == 0)
    def _():
        xbuf[0:8, :] = jnp.zeros((8, SCONV), F32)
        s_ref[...] = jnp.zeros(s_ref.shape, F32)

    xbuf[8:8 + L, :] = xbc_ref[...].astype(F32)
    y = cb_ref[...] + cw_ref[0:1, :] * xbuf[pl.ds(5, L), :]
    for t in range(1, CW):
        y = y + cw_ref[t:t + 1, :] * xbuf[pl.ds(5 + t, L), :]
    xbc = _silu(y)
    xs, bm, cm = xbc[:, :SI], xbc[:, SI:SI + SG * SN], xbc[:, SI + SG * SN:]

    dt = _softplus(sm_ref[...] + dtb_ref[...])
    a = dt * ac_ref[...]
    row = lax.broadcasted_iota(jnp.int32, (L, L), 0)
    col = lax.broadcasted_iota(jnp.int32, (L, L), 1)
    causal = row >= col
    acum = _dot_sel(causal.astype(BF), a)
    acum_t = acum.T
    alast = acum[L - 1:L, :]
    e = e_ref[...]
    dt_e = _dot_x_sel(dt, e)
    ea_e = _dot_x_sel(jnp.exp(acum), e)
    dend_e = _dot_x_sel(jnp.exp(alast - acum), e)
    xdt = xs * dt_e
    xd = (xdt * dend_e).astype(BF)
    lane = lax.broadcasted_iota(jnp.int32, (L, LANES), 1)
    gw = SI // SG
    hpg = SH // SG
    y_parts = []
    for g in range(SG):
        cg = cm[:, g * SN:(g + 1) * SN].astype(BF)
        bg = bm[:, g * SN:(g + 1) * SN].astype(BF)
        s_old = s_ref[g * gw:(g + 1) * gw, :]
        y_inter = _dot_nt(cg, s_old.astype(BF))
        cb = _dot_nt(cg, bg)
        pairs = []
        for jp in range(hpg // 2):
            h0 = g * hpg + 2 * jp
            xp = xdt[:, h0 * SP:h0 * SP + LANES]
            acc = None
            for d in range(2):
                h = h0 + d
                seg = jnp.where(causal, jnp.exp(acum[:, h:h + 1] - acum_t[h:h + 1, :]), 0.0)
                mh = (cb * seg).astype(BF)
                xm = jnp.where((lane < SP) if d == 0 else (lane >= SP), xp, 0.0).astype(BF)
                t = _dot(mh, xm)
                acc = t if acc is None else acc + t
            pairs.append(acc)
        y_intra = jnp.concatenate(pairs, axis=1)
        y_parts.append(y_intra + y_inter * ea_e[:, g * gw:(g + 1) * gw])
        upd = _dot_tn(xd[:, g * gw:(g + 1) * gw], bg)
        for hh in range(hpg):
            h = g * hpg + hh
            r0 = h * SP
            s_ref[r0:r0 + SP, :] = (s_old[hh * SP:(hh + 1) * SP, :] * jnp.exp(alast[:, h:h + 1])
                                    + upd[hh * SP:(hh + 1) * SP, :])
    yv = jnp.concatenate(y_parts, axis=1) + dx_ref[...] * xs
    yv = yv * _silu(z_ref[...].astype(F32))
    outs = []
    for g in range(SG):
        yg = yv[:, g * gw:(g + 1) * gw]
        outs.append(yg * lax.rsqrt(jnp.mean(yg * yg, axis=-1, keepdims=True) + RMS_EPS) * nw_ref[:, g * gw:(g + 1) * gw])
    y_ref[...] = jnp.concatenate(outs, axis=1).astype(BF)
    xbuf[0:8, :] = xbuf[L:L + 8, :]

    @pl.when(n == pl.num_programs(1) - 1)
    def _():
        st_ref[...] = s_ref[...]
        tail_ref[...] = xbuf[pl.ds(L + 5, CW - 1), :]


def ssd_prompt(xbc, zz, small, lp, nb, seq, chunk, z_col):
    nc = seq // chunk
    heads = jnp.arange(LANES)[:, None]
    e = (heads == (jnp.arange(SI)[None, :] // SP)).astype(BF)

    def pad_lanes(v, off):
        return jnp.zeros((1, LANES), F32).at[0, off:off + v.shape[0]].set(v)

    full = lambda shape: pl.BlockSpec(shape, lambda b, n: (0,) * len(shape))
    return pl.pallas_call(
        functools.partial(_ssd_kernel, chunk=chunk),
        grid=(nb, nc),
        in_specs=[pl.BlockSpec((chunk, SCONV), lambda b, n: (b * nc + n, 0)),
                  pl.BlockSpec((chunk, SI), lambda b, n: (b * nc + n, z_col)),
                  pl.BlockSpec((chunk, LANES), lambda b, n: (b * nc + n, 0)),
                  full((CW, SCONV)), full((1, SCONV)), full((1, LANES)), full((1, LANES)),
                  full((1, SI)), full((1, SI)), full((LANES, SI))],
        out_specs=[pl.BlockSpec((chunk, SI), lambda b, n: (b * nc + n, 0)),
                   pl.BlockSpec((None, SH * SP, SN), lambda b, n: (b, 0, 0)),
                   pl.BlockSpec((None, CW - 1, SCONV), lambda b, n: (b, 0, 0))],
        out_shape=[jax.ShapeDtypeStruct((nb * seq, SI), BF),
                   jax.ShapeDtypeStruct((nb, SH * SP, SN), F32),
                   jax.ShapeDtypeStruct((nb, CW - 1, SCONV), F32)],
        scratch_shapes=[pltpu.VMEM((chunk + 8, SCONV), F32), pltpu.VMEM((SH * SP, SN), F32)],
        compiler_params=_cparams(("parallel", "arbitrary")),
        name="ssd_chunk_scan",
    )(xbc, zz, small, lp['ssd_conv_w'], lp['ssd_conv_b'].reshape(1, SCONV),
      pad_lanes(lp['ssd_dt_bias'], L_DT), pad_lanes(-jnp.exp(lp['ssd_a_log']), L_DT),
      jnp.repeat(lp['ssd_d'], SP).reshape(1, SI), lp['ssd_norm'].reshape(1, SI), e)


def _gdn_stage_kernel(qkv_ref, z_ref, sm_ref, cw_ref, dtb_ref, na_ref, nw_ref,
                      o_ref, st_ref, tail_ref, xbuf, s_ref, *, chunk, nb):
    n = pl.program_id(0)
    C = chunk

    @pl.when(n == 0)
    def _():
        xbuf[:, 0:8, :] = jnp.zeros((nb, 8, NCONV), F32)
        s_ref[...] = jnp.zeros(s_ref.shape, F32)

    r128 = lax.broadcasted_iota(jnp.int32, (LANES, LANES), 0)
    c128 = lax.broadcasted_iota(jnp.int32, (LANES, LANES), 1)
    tri128 = (r128 >= c128).astype(BF)
    row = lax.broadcasted_iota(jnp.int32, (C, C), 0)
    col = lax.broadcasted_iota(jnp.int32, (C, C), 1)
    eye = (row == col).astype(F32)
    ch = []
    for b in range(nb):
        xbuf[b, 8:8 + C, :] = qkv_ref[b].astype(F32)
        y = cw_ref[0:1, :] * xbuf[b, pl.ds(5, C), :]
        for t in range(1, CW):
            y = y + cw_ref[t:t + 1, :] * xbuf[b, pl.ds(5 + t, C), :]
        qkv = _silu(y)
        sm = sm_ref[b]
        beta = _sigmoid(sm)
        g = na_ref[...] * _softplus(sm + dtb_ref[...])
        gc = _dot_sel(tri128, jnp.concatenate([g, jnp.zeros((LANES - C, LANES), F32)], axis=0))
        gc_t = gc.T
        for h in range(NH):
            qh = qkv[:, h * NK:(h + 1) * NK]
            kh = qkv[:, NKW + h * NK:NKW + (h + 1) * NK]
            vh = qkv[:, 2 * NKW + h * NV:2 * NKW + (h + 1) * NV]
            qh = qh * lax.rsqrt(jnp.sum(qh * qh, axis=-1, keepdims=True) + RMS_EPS) * (NK ** -0.5)
            kh = kh * lax.rsqrt(jnp.sum(kh * kh, axis=-1, keepdims=True) + RMS_EPS)
            la = L_NA + h
            gcol = gc[0:C, la:la + 1]
            grow = gc_t[la:la + 1, 0:C]
            glast = gc[C - 1:C, la:la + 1]
            bh = beta[:, L_NB + h:L_NB + h + 1]
            kb = kh * bh
            eg = jnp.exp(gcol)
            ch.append(dict(
                b=b, h=h, glast=glast,
                dmat=jnp.exp(jnp.where(row >= col, gcol - grow, -1e30)),
                kbb=kb.astype(BF), khb=kh.astype(BF), qhb=qh.astype(BF),
                rhs=jnp.concatenate([vh * bh, kb * eg], axis=1),
                qg=(qh * eg).astype(BF), kd=(kh * jnp.exp(glast - gcol)).astype(BF)))
    for c in ch:
        c['x'] = -jnp.where(row > col, _dot_nt(c['kbb'], c['khb']) * c['dmat'], 0.0)
        c['qk'] = (_dot_nt(c['qhb'], c['khb']) * c['dmat']).astype(BF)
    for c in ch:
        c['p'] = _dot3(c['x'], c['x'])
    span = 2
    first = True
    while span < C:
        for c in ch:
            tm = (eye + c['x']) if first else c['t']
            if first:
                c['t'] = tm + _dot3(tm, c['p'])
            else:
                c['t'] = tm + _dot(tm.astype(BF), c['p'].astype(BF))
            if span * 2 < C:
                pb = c['p'].astype(BF)
                c['p'] = _dot(pb, pb)
        first = False
        span *= 2
    for c in ch:
        c['sol'] = _dot3(c['t'], c['rhs'])
    for c in ch:
        s_old = s_ref[c['b'], c['h']]
        sb = s_old.astype(BF)
        c['s_old'] = s_old
        c['vn'] = (c['sol'][:, :NV] - _dot(c['sol'][:, NV:].astype(BF), sb)).astype(BF)
        c['oq'] = _dot(c['qg'], sb)
    for c in ch:
        b, h = c['b'], c['h']
        o = c['oq'] + _dot(c['qk'], c['vn'])
        s_ref[b, h] = c['s_old'] * jnp.exp(c['glast']) + _dot_tn(c['kd'], c['vn'])
        o = o * lax.rsqrt(jnp.mean(o * o, axis=-1, keepdims=True) + RMS_EPS) * nw_ref[...]
        sl = slice(h * NV, (h + 1) * NV)
        o_ref[b, :, sl] = (o * _silu(z_ref[b, :, sl].astype(F32))).astype(BF)
    xbuf[:, 0:8, :] = xbuf[:, C:C + 8, :]

    @pl.when(n == pl.num_programs(0) - 1)
    def _():
        st_ref[...] = s_ref[...]
        tail_ref[...] = xbuf[:, pl.ds(C + 5, CW - 1), :]


def gdn_prompt_staged(qkv, zz, small, lp, nb, seq, chunk, z_col):
    nc = seq // chunk

    def pad_lanes(v, off):
        return jnp.zeros((1, LANES), F32).at[0, off:off + v.shape[0]].set(v)

    full = lambda shape: pl.BlockSpec(shape, lambda n: (0,) * len(shape))
    return pl.pallas_call(
        functools.partial(_gdn_stage_kernel, chunk=chunk, nb=nb),
        grid=(nc,),
        in_specs=[pl.BlockSpec((nb, chunk, NCONV), lambda n: (0, n, 0)),
                  pl.BlockSpec((nb, chunk, NH * NV), lambda n: (0, n, z_col)),
                  pl.BlockSpec((nb, chunk, LANES), lambda n: (0, n, 0)),
                  full((CW, NCONV)), full((1, LANES)), full((1, LANES)), full((1, NV))],
        out_specs=[pl.BlockSpec((nb, chunk, NH * NV), lambda n: (0, n, 0)),
                   full((nb, NH, NK, NV)), full((nb, CW - 1, NCONV))],
        out_shape=[jax.ShapeDtypeStruct((nb, seq, NH * NV), BF),
                   jax.ShapeDtypeStruct((nb, NH, NK, NV), F32),
                   jax.ShapeDtypeStruct((nb, CW - 1, NCONV), F32)],
        scratch_shapes=[pltpu.VMEM((nb, chunk + 8, NCONV), F32), pltpu.VMEM((nb, NH, NK, NV), F32)],
        compiler_params=_cparams(("arbitrary",)),
        name="gdn_chunk_scan",
    )(qkv, zz, small, lp['dn_conv_w'], pad_lanes(lp['dn_dt_bias'], L_NA),
      pad_lanes(-jnp.exp(lp['dn_a_log']), L_NA), lp['dn_norm'].reshape(1, NV))


def _merge_kernel(x_ref, ao_ref, so_ref, no_ref, g0_ref, g1_ref, g2_ref, wa_ref, ws_ref, wn_ref, wo_ref,
                  lg_ref, lb_ref, wr_ref, wsg_ref, wsu_ref, wsd_ref, x1_ref, x1b_ref, r_ref, lgt_ref):
    def gate(ref):
        return _sigmoid(ref[...].astype(F32))

    mixed = (gate(g0_ref) * _dot(ao_ref[...].astype(BF), wa_ref[...])
             + gate(g1_ref) * _dot(so_ref[...].astype(BF), ws_ref[...])
             + gate(g2_ref) * _dot(no_ref[...].astype(BF), wn_ref[...]))
    yv = ALPHA * x_ref[...] + _dot(mixed.astype(BF), wo_ref[...])
    mu = jnp.mean(yv, axis=-1, keepdims=True)
    yc = yv - mu
    var = jnp.mean(yc * yc, axis=-1, keepdims=True)
    x1 = yc * lax.rsqrt(var + LN_EPS) * lg_ref[...] + lb_ref[...]
    x1b = x1.astype(BF)
    x1_ref[...] = x1
    x1b_ref[...] = x1b
    lgt_ref[...] = _dot(x1b, wr_ref[...])
    hsh = _silu(_dot(x1b, wsg_ref[...])) * _dot(x1b, wsu_ref[...])
    r_ref[...] = ALPHA * x1 + _dot(hsh.astype(BF), wsd_ref[...])


def merge_post(x, ao, so, no, gates, gate_col, wts, tm):
    m = x.shape[0]
    tm = min(tm, m)
    rowb = lambda c: pl.BlockSpec((tm, D), lambda i, c=c: (i, c))
    full = lambda a: pl.BlockSpec(a.shape, lambda i: (0,) * a.ndim)
    return pl.pallas_call(
        _merge_kernel,
        grid=(m // tm,),
        in_specs=[rowb(0), rowb(0), rowb(0), rowb(0), rowb(gate_col), rowb(gate_col + 1), rowb(gate_col + 2)]
        + [full(a) for a in wts],
        out_specs=[rowb(0), rowb(0), rowb(0), pl.BlockSpec((tm, LANES), lambda i: (i, 0))],
        out_shape=[jax.ShapeDtypeStruct((m, D), F32), jax.ShapeDtypeStruct((m, D), BF),
                   jax.ShapeDtypeStruct((m, D), F32), jax.ShapeDtypeStruct((m, LANES), F32)],
        compiler_params=_cparams(("parallel",)),
        name="merge_ln_router_shared",
    )(x, ao, so, no, gates, gates, gates, *wts)


def _moe_kernel(be_ref, nu_ref, x_ref, wg_ref, wu_ref, wd_ref, o_ref):
    i = pl.program_id(0)

    @pl.when(i < nu_ref[0])
    def _():
        x = x_ref[...]
        hid = _silu(_dot(x, wg_ref[...])) * _dot(x, wu_ref[...])
        o_ref[...] = _dot(hid.astype(BF), wd_ref[...]).astype(o_ref.dtype)

    @pl.when(i >= nu_ref[0])
    def _():
        o_ref[...] = jnp.zeros(o_ref.shape, o_ref.dtype)


def moe_experts(xb, block_expert, n_used, wg, wu, wd, bm, out_dtype):
    n_rows = xb.shape[0]
    ff = wg.shape[2]
    grid_spec = pltpu.PrefetchScalarGridSpec(
        num_scalar_prefetch=2,
        grid=(n_rows // bm,),
        in_specs=[pl.BlockSpec((bm, D), lambda i, be, nu: (i, 0)),
                  pl.BlockSpec((None, D, ff), lambda i, be, nu: (be[i], 0, 0)),
                  pl.BlockSpec((None, D, ff), lambda i, be, nu: (be[i], 0, 0)),
                  pl.BlockSpec((None, ff, D), lambda i, be, nu: (be[i], 0, 0))],
        out_specs=pl.BlockSpec((bm, D), lambda i, be, nu: (i, 0)),
    )
    return pl.pallas_call(
        _moe_kernel,
        grid_spec=grid_spec,
        out_shape=jax.ShapeDtypeStruct((n_rows, D), out_dtype),
        compiler_params=_cparams(("arbitrary",)),
        name="moe_grouped_ffn",
    )(block_expert, n_used, xb, wg, wu, wd)


def _route_kernel(lg_ref, bias_ref, idx_ref, gate_ref, sel_ref):
    lg = lg_ref[...]
    shape = lg.shape
    lane = lax.broadcasted_iota(jnp.int32, shape, 1)
    lanef = lane.astype(F32)
    neg = -jnp.inf
    scores = _sigmoid(lg)
    biased = jnp.where(lane < NE, scores + bias_ref[...], neg)

    def first_max(x):
        m = jnp.max(x, axis=-1, keepdims=True)
        first = jnp.min(jnp.where(x == m, lanef, 2.0 * LANES), axis=-1, keepdims=True)
        return m, first

    gsz = NE // NGRP
    in_group = [(lane >= g * gsz) & (lane < (g + 1) * gsz) for g in range(NGRP)]
    gscore = []
    for g in range(NGRP):
        xg = jnp.where(in_group[g], biased, neg)
        m1, first = first_max(xg)
        m2 = jnp.max(jnp.where(lanef == first, neg, xg), axis=-1, keepdims=True)
        gscore.append(m1 + m2)
    allowed = jnp.zeros(shape, jnp.bool_)
    for g in range(NGRP):
        rank = jnp.zeros_like(gscore[g])
        for g2 in range(NGRP):
            if g2 != g:
                ahead = (gscore[g2] > gscore[g]) | ((gscore[g2] == gscore[g]) & (g2 < g))
                rank = rank + jnp.where(ahead, 1.0, 0.0)
        allowed = allowed | ((rank < TOPG) & in_group[g])
    work = jnp.where(allowed, biased, neg)
    idx_out = jnp.zeros(shape, F32)
    gate_out = jnp.zeros(shape, F32)
    chosen = jnp.zeros(shape, jnp.bool_)
    for j in range(TOPK):
        _, first = first_max(work)
        sel = lanef == first
        sc = jnp.sum(jnp.where(sel, scores, 0.0), axis=-1, keepdims=True)
        idx_out = jnp.where(lane == j, first, idx_out)
        gate_out = jnp.where(lane == j, sc, gate_out)
        chosen = chosen | sel
        work = jnp.where(sel, neg, work)
    gate_ref[...] = gate_out / jnp.sum(gate_out, axis=-1, keepdims=True) * RSCALE
    sel_ref[...] = jnp.where(chosen, 1, 0).astype(jnp.int32)
    idx_ref[...] = idx_out.T[0:TOPK, :].astype(jnp.int32)


def _route(logits, router_bias):
    t = logits.shape[0]
    tm = min(256, t)
    bias = jnp.concatenate([router_bias, jnp.zeros((LANES - NE,), F32)]).reshape(1, LANES)
    rows = pl.BlockSpec((tm, LANES), lambda i: (i, 0))
    return pl.pallas_call(
        _route_kernel,
        grid=(t // tm,),
        in_specs=[rows, pl.BlockSpec((1, LANES), lambda i: (0, 0))],
        out_specs=[pl.BlockSpec((TOPK, tm), lambda i: (0, i)), rows, rows],
        out_shape=[jax.ShapeDtypeStruct((TOPK, t), jnp.int32), jax.ShapeDtypeStruct((t, LANES), F32),
                   jax.ShapeDtypeStruct((t, LANES), jnp.int32)],
        compiler_params=_cparams(("parallel",)),
        name="moe_route_topk",
    )(logits, bias)


def _dispatch(idx_t, sel, bm):
    t = sel.shape[0]
    na = t * TOPK
    assert na % bm == 0
    cum = jnp.cumsum(sel, axis=0)
    counts = cum[-1]
    padded = (counts + bm - 1) // bm * bm
    pad_end = jnp.cumsum(padded)
    dest_t = ((pad_end - padded)[None, :] + cum - sel).T
    experts = jnp.arange(NE, dtype=jnp.int32)[None, :, None]
    pos_t = jnp.sum(jnp.where(idx_t[:, None, :] == experts, dest_t[None, :NE, :], 0), axis=1).astype(jnp.int32)
    n_blocks = na // bm + NE
    tok = jnp.broadcast_to(jnp.arange(t, dtype=jnp.int32)[None, :], (TOPK, t))
    buf_tok = jnp.zeros((n_blocks * bm,), jnp.int32).at[pos_t.reshape(na)].set(tok.reshape(na), unique_indices=True)
    block_expert = jnp.minimum(
        jnp.searchsorted(pad_end[:NE], jnp.arange(n_blocks, dtype=jnp.int32) * bm, side='right'),
        NE - 1).astype(jnp.int32)
    n_used = (pad_end[NE - 1:NE] // bm).astype(jnp.int32)
    return buf_tok, block_expert, n_used, pos_t


def _combine_kernel(yg_ref, gate_ref, r_ref, g_ref, b_ref, o_ref):
    acc = r_ref[...]
    gate = gate_ref[...]
    for j in range(TOPK):
        acc = acc + yg_ref[j].astype(F32) * gate[:, j:j + 1]
    mu = jnp.mean(acc, axis=-1, keepdims=True)
    yc = acc - mu
    var = jnp.mean(yc * yc, axis=-1, keepdims=True)
    o_ref[...] = yc * lax.rsqrt(var + LN_EPS) * g_ref[...] + b_ref[...]


def combine_ln(yg, gate, r, g, b, tm):
    t = r.shape[0]
    tm = min(tm, t)
    return pl.pallas_call(
        _combine_kernel,
        grid=(t // tm,),
        in_specs=[pl.BlockSpec((TOPK, tm, D), lambda i: (0, i, 0)), pl.BlockSpec((tm, LANES), lambda i: (i, 0)),
                  pl.BlockSpec((tm, D), lambda i: (i, 0)), pl.BlockSpec((1, D), lambda i: (0, 0)),
                  pl.BlockSpec((1, D), lambda i: (0, 0))],
        out_specs=pl.BlockSpec((tm, D), lambda i: (i, 0)),
        out_shape=jax.ShapeDtypeStruct((t, D), F32),
        compiler_params=_cparams(("parallel",)),
        name="moe_combine_ln",
    )(yg, gate, r, g.reshape(1, D), b.reshape(1, D))


def moe_finish(r, x1b, logits, lp, bm):
    t = r.shape[0]
    idx_t, gate, sel = _route(logits, lp['router_bias'])
    buf_tok, block_expert, n_used, pos_t = _dispatch(idx_t, sel, bm)
    yb = moe_experts(x1b[buf_tok], block_expert, n_used, lp['w_exp_gate'], lp['w_exp_up'], lp['w_exp_down'], bm, BF)
    yg = yb[pos_t.reshape(TOPK * t)].reshape(TOPK, t, D)
    return combine_ln(yg, gate, r, lp['ln2_g'], lp['ln2_b'], 256)


def _rms(x, g):
    return x * lax.rsqrt(jnp.mean(x * x, -1, keepdims=True) + RMS_EPS) * g


def _prep_layer(l, w_in, p):
    wi = w_in[l]
    small = jnp.concatenate([wi[:, R_DT:R_NQKV], wi[:, R_NB:R_GATE], jnp.zeros((D, LANES - 32), wi.dtype)], axis=1)
    lp = {k: v[l] for k, v in p.items()}
    lp.update(
        w_qkv=wi[:, R_Q:R_Z].astype(BF),
        w_zg=jnp.concatenate([wi[:, R_Z:R_XBC], wi[:, R_NZ:R_NB], wi[:, R_GATE:]], axis=1).astype(BF),
        w_xbc=wi[:, R_XBC:R_DT].astype(BF),
        w_nqkv=wi[:, R_NQKV:R_NZ].astype(BF),
        w_small=small.astype(BF),
        w_router_p=jnp.concatenate([lp['w_router'], jnp.zeros((D, LANES - NE), F32)], axis=1).astype(BF),
    )
    for k in ('w_attn_o', 'w_ssd_o', 'w_dn_o', 'w_out', 'w_sh_gate', 'w_sh_up', 'w_sh_down',
              'w_exp_gate', 'w_exp_up', 'w_exp_down'):
        lp[k] = lp[k].astype(BF)
    return lp


def _merge_weights(lp):
    return [lp['w_attn_o'], lp['w_ssd_o'], lp['w_dn_o'], lp['w_out'], lp['ln1_g'].reshape(1, D),
            lp['ln1_b'].reshape(1, D), lp['w_router_p'], lp['w_sh_gate'], lp['w_sh_up'], lp['w_sh_down']]


def prompt_layer(x, lp, layer_idx, tabs):
    lam_init = 0.8 - 0.6 * math.exp(-0.3 * layer_idx)
    q, kf, kb, vf, vb = qkv_project(x, lp['w_qkv'], tabs, 512)
    zg = matmul(x, lp['w_zg'], BF)
    xbc = matmul(x, lp['w_xbc'], BF)
    nqkv = matmul(x, lp['w_nqkv'], BF)
    small = matmul(x, lp['w_small'], F32)
    ao = flash_diff_attention(q, kb, vb, lp['attn_lambda'], lp['attn_subln'], lam_init, NB, SEQ, ATT_BLK)
    so, ssd_h, ssd_tail = ssd_prompt(xbc, zg, small, lp, NB, SEQ, SSD_CHUNK, 0)
    no, dn_s, dn_tail = gdn_prompt_staged(nqkv.reshape(NB, SEQ, NCONV), zg.reshape(NB, SEQ, -1),
                                          small.reshape(NB, SEQ, LANES), lp, NB, SEQ, GDN_CHUNK, 1)
    no = no.reshape(NB * SEQ, NH * NV)
    x1, x1b, r, logits = merge_post(x, ao, so, no, zg, 2, _merge_weights(lp), 512)
    x2 = moe_finish(r, x1b, logits, lp, MOE_BLK_PROMPT)
    return (x2, kf.reshape(NB, SEQ, KVH, 2 * HD), vf.reshape(NB, SEQ, KVH, 2 * HD),
            ssd_h.reshape(NB, SH, SP, SN), ssd_tail, dn_s, dn_tail)


def _conv_step(hist, new, w, b=None):
    xp = jnp.concatenate([hist, new[:, None, :]], axis=1)
    y = jnp.sum(xp * w[None], axis=1)
    if b is not None:
        y = y + b
    return jax.nn.silu(y), xp[:, 1:]


def decode_layer(x, lp, layer_idx, tabs, cache_k, cache_v, page_table, ssd_h0, ssd_hist, dn_s0, dn_hist):
    lam_init = 0.8 - 0.6 * math.exp(-0.3 * layer_idx)
    q, kf, _, vf, _ = qkv_project(x, lp['w_qkv'], tabs, DB)
    zg = matmul(x, lp['w_zg'], F32, tm=DB)
    xbc_raw = matmul(x, lp['w_xbc'], F32, tm=DB)
    nqkv_raw = matmul(x, lp['w_nqkv'], F32, tm=DB)
    small = matmul(x, lp['w_small'], F32, tm=DB)
    qh = q.reshape(DB, KVH, 2, 2, HD)
    q16 = jnp.einsum('bhgcd,ce->bhgced', qh, jnp.eye(2, dtype=BF)).reshape(DB, 16, 2 * HD)
    k16 = jnp.repeat(kf.reshape(DB, KVH, 2 * HD), 4, axis=1)
    v16 = jnp.repeat(vf.reshape(DB, KVH, 2 * HD), 4, axis=1)
    oc = paged_attention(q16, k16, v16, cache_k, cache_v, page_table, layer_idx, PAGES_PER_STEP)
    oc = oc.reshape(DB, AH, 2, 2 * HD)
    lq = lp['attn_lambda']
    lam = jnp.exp(jnp.sum(lq[0] * lq[1])) - jnp.exp(jnp.sum(lq[2] * lq[3])) + lam_init
    o = oc[:, :, 0] - lam * oc[:, :, 1]
    ao = (_rms(o, lp['attn_subln']) * (1.0 - lam_init)).reshape(DB, AH * 2 * HD)
    xbc, ssd_tail = _conv_step(ssd_hist, xbc_raw, lp['ssd_conv_w'], lp['ssd_conv_b'])
    xs = xbc[:, :SI].reshape(DB, SH, SP)
    bm = jnp.repeat(xbc[:, SI:SI + SG * SN].reshape(DB, SG, SN), SH // SG, axis=1)
    cm = jnp.repeat(xbc[:, SI + SG * SN:].reshape(DB, SG, SN), SH // SG, axis=1)
    dt = jax.nn.softplus(small[:, L_DT:L_DT + SH] + lp['ssd_dt_bias'])
    a_coef = -jnp.exp(lp['ssd_a_log'])
    ssd_h = ssd_h0 * jnp.exp(dt * a_coef)[..., None, None] + (xs * dt[..., None])[..., :, None] * bm[..., None, :]
    ys = jnp.sum(ssd_h * cm[..., None, :], -1) + lp['ssd_d'][:, None] * xs
    gw = SI // SG
    ys = ys.reshape(DB, SG, gw) * jax.nn.silu(zg[:, :SI]).reshape(DB, SG, gw)
    so = _rms(ys, lp['ssd_norm'].reshape(SG, gw)).reshape(DB, SI)
    qkv, dn_tail = _conv_step(dn_hist, nqkv_raw, lp['dn_conv_w'])
    nq = qkv[:, :NKW].reshape(DB, NH, NK)
    nk = qkv[:, NKW:2 * NKW].reshape(DB, NH, NK)
    nv = qkv[:, 2 * NKW:].reshape(DB, NH, NV)
    nq = nq * lax.rsqrt(jnp.sum(nq * nq, -1, keepdims=True) + RMS_EPS) * (NK ** -0.5)
    nk = nk * lax.rsqrt(jnp.sum(nk * nk, -1, keepdims=True) + RMS_EPS)
    beta = jax.nn.sigmoid(small[:, L_NB:L_NB + NH])
    gl = -jnp.exp(lp['dn_a_log']) * jax.nn.softplus(small[:, L_NA:L_NA + NH] + lp['dn_dt_bias'])
    s = dn_s0 * jnp.exp(gl)[..., None, None]
    delta = (nv - jnp.sum(s * nk[..., :, None], -2)) * beta[..., None]
    dn_s = s + nk[..., :, None] * delta[..., None, :]
    on = jnp.sum(dn_s * nq[..., :, None], -2)
    no = (_rms(on, lp['dn_norm']) * jax.nn.silu(zg[:, SI:2 * SI].reshape(DB, NH, NV))).reshape(DB, NH * NV)
    x1, x1b, r, logits = merge_post(x, ao, so, no, zg, 2, _merge_weights(lp), DB)
    x2 = moe_finish(r, x1b, logits, lp, MOE_BLK_DECODE)
    return (x2, kf.reshape(DB, 1, KVH, 2 * HD), vf.reshape(DB, 1, KVH, 2 * HD), ssd_h, ssd_tail, dn_s, dn_tail)


def kernel(x_prompt, x_sample, cache_k, cache_v, state_ssd, state_ssd_conv, state_dn, state_dn_conv, page_table,
           w_in, attn_lambda, attn_subln, w_attn_o, ssd_conv_w, ssd_conv_b, ssd_dt_bias, ssd_a_log, ssd_d, ssd_norm,
           w_ssd_o, dn_conv_w, dn_dt_bias, dn_a_log, dn_norm, w_dn_o, w_out, ln1_g, ln1_b, w_router, router_bias,
           w_exp_gate, w_exp_up, w_exp_down, w_sh_gate, w_sh_up, w_sh_down, ln2_g, ln2_b):
    params = dict(attn_lambda=attn_lambda, attn_subln=attn_subln, w_attn_o=w_attn_o, ssd_conv_w=ssd_conv_w,
                  ssd_conv_b=ssd_conv_b, ssd_dt_bias=ssd_dt_bias, ssd_a_log=ssd_a_log, ssd_d=ssd_d,
                  ssd_norm=ssd_norm, w_ssd_o=w_ssd_o, dn_conv_w=dn_conv_w, dn_dt_bias=dn_dt_bias,
                  dn_a_log=dn_a_log, dn_norm=dn_norm, w_dn_o=w_dn_o, w_out=w_out, ln1_g=ln1_g, ln1_b=ln1_b,
                  w_router=w_router, router_bias=router_bias, w_exp_gate=w_exp_gate, w_exp_up=w_exp_up,
                  w_exp_down=w_exp_down, w_sh_gate=w_sh_gate, w_sh_up=w_sh_up, w_sh_down=w_sh_down,
                  ln2_g=ln2_g, ln2_b=ln2_b)
    tabs_p = _rope_tables(jnp.arange(SEQ, dtype=jnp.int32))
    tabs_d = _rope_tables(jnp.full((DB,), PAST, dtype=jnp.int32))
    xp = x_prompt.reshape(NB * SEQ, D)
    xs = x_sample.reshape(DB, D)
    outs = [[] for _ in range(12)]
    for l in range(DEPTH):
        lp = _prep_layer(l, w_in, params)
        xp, kp, vp, hp, cp, sp, dp = prompt_layer(xp, lp, l, tabs_p)
        xs, ks, vs, hs, cs, ss, ds = decode_layer(xs, lp, l, tabs_d, cache_k, cache_v, page_table,
                                                  state_ssd[l], state_ssd_conv[l], state_dn[l], state_dn_conv[l])
        for lst, val in zip(outs, (kp, vp, ks, vs, hp, cp, hs, cs, sp, dp, ss, ds)):
            lst.append(val)
    return (xp.reshape(NB, SEQ, D), xs.reshape(DB, 1, D)) + tuple(jnp.stack(o) for o in outs)
```

```python
import functools
import math

import jax
import jax.numpy as jnp
from jax import lax
from jax.experimental import pallas as pl
from jax.experimental.pallas import tpu as pltpu

D = 1024
NB, SEQ = 4, 4096
DEPTH = 2
DB = 32
PAST = 16384
PAGE = 128
AH, KVH, HD = 8, 4, 64
ROT = 16
THETA = 500000.0
SH, SP, SI, SG, SN = 16, 64, 1024, 2, 128
SCONV = SI + 2 * SG * SN
NH, NK, NV = 8, 128, 128
NKW = 1024
NCONV = 3072
CW = 4
NE, TOPK, NGRP, TOPG = 64, 8, 8, 4
RSCALE = 2.5
ALPHA = (2 * DEPTH) ** 0.25
LN_EPS = 1e-5
RMS_EPS = 1e-6

R_Q, R_K, R_V, R_Z, R_XBC, R_DT, R_NQKV, R_NZ, R_NB, R_NA, R_GATE = (
    0, 1024, 1536, 2048, 3072, 4608, 4624, 7696, 8720, 8728, 8736)
L_DT, L_NB, L_NA = 0, 16, 24

LANES = 128
VMEM_LIMIT = 56 * 1024 * 1024
SSD_CHUNK = 128
GDN_CHUNK = 64
ATT_BLK = 512
FLASH_ROWS = 32
PAGES_PER_STEP = 16
PROJ_ROWS = 2048
MOE_BLK_PROMPT = 512
MOE_BLK_DECODE = 16

BF = jnp.bfloat16
F32 = jnp.float32


def _cparams(sem):
    return pltpu.CompilerParams(dimension_semantics=sem, vmem_limit_bytes=VMEM_LIMIT)


def _dot(a, b):
    return jnp.dot(a, b, preferred_element_type=F32)


def _dot_nt(a, b):
    return lax.dot_general(a, b, (((1,), (1,)), ((), ())), preferred_element_type=F32)


def _dot_tn(a, b):
    return lax.dot_general(a, b, (((0,), (0,)), ((), ())), preferred_element_type=F32)


def _split2(x):
    hi = x.astype(BF)
    lo = (x - hi.astype(F32)).astype(BF)
    return hi, lo


def _split3(x):
    hi = x.astype(BF)
    r = x - hi.astype(F32)
    mid = r.astype(BF)
    lo = (r - mid.astype(F32)).astype(BF)
    return hi, mid, lo


def _dot_sel(sel, x):
    hi, mid, lo = _split3(x)
    return _dot(sel, hi) + _dot(sel, mid) + _dot(sel, lo)


def _dot_x_sel(x, sel):
    hi, mid, lo = _split3(x)
    return _dot(hi, sel) + _dot(mid, sel) + _dot(lo, sel)


def _dot3(a, b):
    ah, al = _split2(a)
    bh, bl = _split2(b)
    return _dot(ah, bh) + _dot(ah, bl) + _dot(al, bh)


def _sigmoid(x):
    return 1.0 / (1.0 + jnp.exp(-x))


def _silu(x):
    return x * _sigmoid(x)


def _softplus(x):
    return jnp.maximum(x, 0.0) + jnp.log(1.0 + jnp.exp(-jnp.abs(x)))


def _mm_kernel(x_ref, w_ref, o_ref, xs_ref):
    @pl.when(pl.program_id(1) == 0)
    def _():
        xs_ref[...] = x_ref[...].astype(BF)

    o_ref[...] = _dot(xs_ref[...], w_ref[...]).astype(o_ref.dtype)


def matmul(x, w, out_dtype, tm=512, tn=512):
    m, k = x.shape
    n = w.shape[1]
    tm, tn = min(tm, m), min(tn, n)
    assert m % tm == 0 and n % tn == 0, (m, n, tm, tn)
    return pl.pallas_call(
        _mm_kernel,
        grid=(m // tm, n // tn),
        in_specs=[pl.BlockSpec((tm, k), lambda i, j: (i, 0)),
                  pl.BlockSpec((k, tn), lambda i, j: (0, j))],
        out_specs=pl.BlockSpec((tm, tn), lambda i, j: (i, j)),
        out_shape=jax.ShapeDtypeStruct((m, n), out_dtype),
        scratch_shapes=[pltpu.VMEM((tm, k), BF)],
        compiler_params=_cparams(("parallel", "arbitrary")),
        name="proj_matmul",
    )(x, w)


def _rope_tables(pos):
    half = ROT // 2
    inv = THETA ** (-jnp.arange(half, dtype=F32) * 2.0 / ROT)
    ang = pos.astype(F32)[:, None] * inv
    cos, sin = jnp.cos(ang), jnp.sin(ang)
    t = pos.shape[0]
    one, zero, z8 = jnp.ones((t, HD - ROT), F32), jnp.zeros((t, HD - ROT), F32), jnp.zeros((t, half), F32)
    c = jnp.concatenate([cos, cos, one, cos, cos, one], 1)
    sa = jnp.concatenate([-sin, z8, zero, -sin, z8, zero], 1)
    sb = jnp.concatenate([z8, sin, zero, z8, sin, zero], 1)
    return c, sa, sb


def _qkv_kernel(x_ref, w_ref, c_ref, sa_ref, sb_ref, q_ref, kf_ref, kb_ref, vf_ref, vb_ref):
    acc = _dot(x_ref[...].astype(BF), w_ref[...])
    c, sa, sb = c_ref[...], sa_ref[...], sb_ref[...]

    def rot(xg):
        return xg * c + pltpu.roll(xg, LANES - ROT // 2, 1) * sa + pltpu.roll(xg, ROT // 2, 1) * sb

    for h in range(AH):
        sl = slice(h * LANES, (h + 1) * LANES)
        q_ref[:, sl] = (rot(acc[:, sl]) * (HD ** -0.5)).astype(BF)
    for h in range(KVH):
        sl = slice(h * LANES, (h + 1) * LANES)
        kr = rot(acc[:, R_K + h * LANES:R_K + (h + 1) * LANES])
        kf_ref[:, sl] = kr
        kb_ref[:, sl] = kr.astype(BF)
    v = acc[:, R_V:R_Z]
    vf_ref[...] = v
    vb_ref[...] = v.astype(BF)


def qkv_project(x, w, tabs, tm):
    m = x.shape[0]
    tm = min(tm, m)
    nt = tabs[0].shape[0] // tm
    kw = KVH * 2 * HD
    tab_spec = pl.BlockSpec((tm, LANES), lambda i: (i % nt, 0))
    return pl.pallas_call(
        _qkv_kernel,
        grid=(m // tm,),
        in_specs=[pl.BlockSpec((tm, D), lambda i: (i, 0)),
                  pl.BlockSpec((D, R_Z), lambda i: (0, 0)),
                  tab_spec, tab_spec, tab_spec],
        out_specs=[pl.BlockSpec((tm, AH * 2 * HD), lambda i: (i, 0)),
                   pl.BlockSpec((tm, kw), lambda i: (i, 0)), pl.BlockSpec((tm, kw), lambda i: (i, 0)),
                   pl.BlockSpec((tm, kw), lambda i: (i, 0)), pl.BlockSpec((tm, kw), lambda i: (i, 0))],
        out_shape=[jax.ShapeDtypeStruct((m, AH * 2 * HD), BF),
                   jax.ShapeDtypeStruct((m, kw), F32), jax.ShapeDtypeStruct((m, kw), BF),
                   jax.ShapeDtypeStruct((m, kw), F32), jax.ShapeDtypeStruct((m, kw), BF)],
        compiler_params=_cparams(("parallel",)),
        name="qkv_rope",
    )(x, w, *tabs)


def _lambda_value(lq, lam_init):
    a = jnp.sum(lq[0:1, :] * lq[1:2, :], axis=-1, keepdims=True)
    b = jnp.sum(lq[2:3, :] * lq[3:4, :], axis=-1, keepdims=True)
    return jnp.exp(a) - jnp.exp(b) + lam_init


def _flash_kernel(lq_ref, sub_ref, q_ref, k_ref, v_ref, o_ref, qs_ref, m_ref, acc_ref, s_ref, p_ref, *, blk, lam_init):
    qi, ki = pl.program_id(2), pl.program_id(3)

    @pl.when(ki == 0)
    def _():
        q = q_ref[...]
        lane = lax.broadcasted_iota(jnp.int32, (blk, LANES), 1)
        for c in range(2):
            msk = (lane < HD) if c == 0 else (lane >= HD)
            for g in range(2):
                qs_ref[c, g * blk:(g + 1) * blk, :] = jnp.where(msk, q[:, g * LANES:(g + 1) * LANES], 0)
        m_ref[...] = jnp.full(m_ref.shape, -jnp.inf, F32)
        acc_ref[...] = jnp.zeros(acc_ref.shape, F32)

    def update(masked):
        k = k_ref[...]
        vext = jnp.concatenate([v_ref[...], jnp.ones((blk, LANES), BF)], axis=1)
        units = [(c, g) for g in range(2) for c in range(2)]
        for u, (c, g) in enumerate(units):
            s_ref[u] = _dot_nt(qs_ref[c, g * blk:(g + 1) * blk, :], k)
        if masked:
            keep = (lax.broadcasted_iota(jnp.int32, (blk, blk), 1) <= lax.broadcasted_iota(jnp.int32, (blk, blk), 0))
            diff = (lax.broadcasted_iota(jnp.int32, (FLASH_ROWS, blk), 1)
                    - lax.broadcasted_iota(jnp.int32, (FLASH_ROWS, blk), 0))
        for u, (c, g) in enumerate(units):
            rows = slice(g * blk, (g + 1) * blk)
            s_full = s_ref[u]
            if masked:
                s_full = jnp.where(keep, s_full, -jnp.inf)
            m_old = m_ref[c, rows]
            m_new = jnp.maximum(m_old, jnp.max(s_full, axis=-1, keepdims=True))
            alpha = jnp.exp(m_old - m_new)
            for r in range(0, blk, FLASH_ROWS):
                s = s_ref[u, r:r + FLASH_ROWS, :]
                if masked:
                    s = jnp.where(diff <= r, s, -jnp.inf)
                p_ref[u, r:r + FLASH_ROWS, :] = jnp.exp(s - m_new[r:r + FLASH_ROWS]).astype(BF)
            m_ref[c, rows] = m_new
            acc_ref[c, rows] = alpha * acc_ref[c, rows] + _dot(p_ref[u], vext)

    @pl.when(ki < qi)
    def _():
        update(False)

    @pl.when(ki == qi)
    def _():
        update(True)
        lam = _lambda_value(lq_ref[...], lam_init)
        a0, a1 = acc_ref[0], acc_ref[1]
        o = a0[:, :LANES] / a0[:, LANES:] - lam * (a1[:, :LANES] / a1[:, LANES:])
        o = o * lax.rsqrt(jnp.mean(o * o, axis=-1, keepdims=True) + RMS_EPS) * sub_ref[...] * (1.0 - lam_init)
        o_ref[:, 0:LANES] = o[0:blk].astype(BF)
        o_ref[:, LANES:2 * LANES] = o[blk:2 * blk].astype(BF)


def flash_diff_attention(q, k, v, lq, subln, lam_init, nb, seq, blk):
    nblk = seq // blk
    kern = functools.partial(_flash_kernel, blk=blk, lam_init=lam_init)
    return pl.pallas_call(
        kern,
        grid=(nb, KVH, nblk, nblk),
        in_specs=[pl.BlockSpec((4, HD), lambda b, h, i, j: (0, 0)),
                  pl.BlockSpec((1, LANES), lambda b, h, i, j: (0, 0)),
                  pl.BlockSpec((blk, 2 * LANES), lambda b, h, i, j: (b * nblk + i, h)),
                  pl.BlockSpec((blk, LANES), lambda b, h, i, j: (b * nblk + jnp.minimum(i, j), h)),
                  pl.BlockSpec((blk, LANES), lambda b, h, i, j: (b * nblk + jnp.minimum(i, j), h))],
        out_specs=pl.BlockSpec((blk, 2 * LANES), lambda b, h, i, j: (b * nblk + i, h)),
        out_shape=jax.ShapeDtypeStruct((nb * seq, AH * 2 * HD), BF),
        scratch_shapes=[pltpu.VMEM((2, 2 * blk, LANES), BF), pltpu.VMEM((2, 2 * blk, 1), F32),
                        pltpu.VMEM((2, 2 * blk, 2 * LANES), F32),
                        pltpu.VMEM((4, blk, blk), F32), pltpu.VMEM((4, blk, blk), BF)],
        compiler_params=_cparams(("parallel", "parallel", "parallel", "arbitrary")),
        name="flash_diff_attn",
    )(lq, subln.reshape(1, LANES), q, k, v)


def _paged_kernel(pt_ref, q_ref, ks_ref, vs_ref, *refs, pps):
    kp, vp = refs[:pps], refs[pps:2 * pps]
    o_ref, m_ref, l_ref, acc_ref = refs[2 * pps:]
    j = pl.program_id(1)
    q = q_ref[...]
    pw = PAGE * KVH

    @pl.when(j == 0)
    def _():
        ks = ks_ref[...].astype(BF).astype(F32)
        m_ref[...] = jnp.sum(q.astype(F32) * ks, axis=-1, keepdims=True)
        l_ref[...] = jnp.ones(l_ref.shape, F32)
        acc_ref[...] = vs_ref[...].astype(BF).astype(F32)

    row_head = lax.broadcasted_iota(jnp.int32, (16, pw), 0) // 4
    col_head = lax.broadcasted_iota(jnp.int32, (16, pw), 1) % KVH
    own = row_head == col_head
    s = jnp.concatenate([jnp.where(own, _dot_nt(q, kp[i][...].astype(BF)), -jnp.inf) for i in range(pps)], axis=1)
    m_old = m_ref[...]
    m_new = jnp.maximum(m_old, jnp.max(s, axis=-1, keepdims=True))
    alpha = jnp.exp(m_old - m_new)
    p = jnp.exp(s - m_new)
    l_ref[...] = alpha * l_ref[...] + jnp.sum(p, axis=-1, keepdims=True)
    pv = _dot(p[:, 0:pw].astype(BF), vp[0][...].astype(BF))
    for i in range(1, pps):
        pv = pv + _dot(p[:, i * pw:(i + 1) * pw].astype(BF), vp[i][...].astype(BF))
    acc_ref[...] = alpha * acc_ref[...] + pv
    m_ref[...] = m_new

    @pl.when(j == pl.num_programs(1) - 1)
    def _():
        o_ref[...] = acc_ref[...] / l_ref[...]


def paged_attention(q16, k16, v16, cache_k, cache_v, page_table, layer, pps):
    nseq, npg = page_table.shape
    pw = PAGE * KVH
    ck = cache_k.reshape(cache_k.shape[0], cache_k.shape[1], pw, 2 * HD)
    cv = cache_v.reshape(cache_v.shape[0], cache_v.shape[1], pw, 2 * HD)

    def page_spec(i):
        return pl.BlockSpec((None, None, pw, 2 * HD), lambda b, j, pt: (layer, pt[b, j * pps + i], 0, 0))

    row = pl.BlockSpec((None, 16, 2 * HD), lambda b, j, pt: (b, 0, 0))
    grid_spec = pltpu.PrefetchScalarGridSpec(
        num_scalar_prefetch=1,
        grid=(nseq, npg // pps),
        in_specs=[row, row, row] + [page_spec(i) for i in range(pps)] + [page_spec(i) for i in range(pps)],
        out_specs=row,
        scratch_shapes=[pltpu.VMEM((16, 1), F32), pltpu.VMEM((16, 1), F32), pltpu.VMEM((16, 2 * HD), F32)],
    )
    return pl.pallas_call(
        functools.partial(_paged_kernel, pps=pps),
        grid_spec=grid_spec,
        out_shape=jax.ShapeDtypeStruct((nseq, 16, 2 * HD), F32),
        compiler_params=_cparams(("parallel", "arbitrary")),
        name="paged_diff_attn",
    )(page_table, q16, k16, v16, *([ck] * pps), *([cv] * pps))


def _ssd_kernel(xbc_ref, z_ref, sm_ref, cw_ref, cb_ref, dtb_ref, ac_ref, dx_ref, nw_ref, e_ref,
                y_ref, st_ref, tail_ref, xbuf, s_ref, *, chunk):
    n = pl.program_id(1)
    L = chunk

    @pl.when(n == 0)
    def _():
        xbuf[0:8, :] = jnp.zeros((8, SCONV), F32)
        s_ref[...] = jnp.zeros(s_ref.shape, F32)

    xbuf[8:8 + L, :] = xbc_ref[...].astype(F32)
    y = cb_ref[...] + cw_ref[0:1, :] * xbuf[pl.ds(5, L), :]
    for t in range(1, CW):
        y = y + cw_ref[t:t + 1, :] * xbuf[pl.ds(5 + t, L), :]
    xbc = _silu(y)
    xs, bm, cm = xbc[:, :SI], xbc[:, SI:SI + SG * SN], xbc[:, SI + SG * SN:]

    dt = _softplus(sm_ref[...] + dtb_ref[...])
    a = dt * ac_ref[...]
    row = lax.broadcasted_iota(jnp.int32, (L, L), 0)
    col = lax.broadcasted_iota(jnp.int32, (L, L), 1)
    causal = row >= col
    acum = _dot_sel(causal.astype(BF), a)
    acum_t = acum.T
    alast = acum[L - 1:L, :]
    e = e_ref[...]
    dt_e = _dot_x_sel(dt, e)
    ea_e = _dot_x_sel(jnp.exp(acum), e)
    dend_e = _dot_x_sel(jnp.exp(alast - acum), e)
    xdt = xs * dt_e
    xd = (xdt * dend_e).astype(BF)
    lane = lax.broadcasted_iota(jnp.int32, (L, LANES), 1)
    gw = SI // SG
    hpg = SH // SG
    y_parts = []
    for g in range(SG):
        cg = cm[:, g * SN:(g + 1) * SN].astype(BF)
        bg = bm[:, g * SN:(g + 1) * SN].astype(BF)
        s_old = s_ref[g * gw:(g + 1) * gw, :]
        y_inter = _dot_nt(cg, s_old.astype(BF))
        cb = _dot_nt(cg, bg)
        pairs = []
        for jp in range(hpg // 2):
            h0 = g * hpg + 2 * jp
            xp = xdt[:, h0 * SP:h0 * SP + LANES]
            acc = None
            for d in range(2):
                h = h0 + d
                seg = jnp.where(causal, jnp.exp(acum[:, h:h + 1] - acum_t[h:h + 1, :]), 0.0)
                mh = (cb * seg).astype(BF)
                xm = jnp.where((lane < SP) if d == 0 else (lane >= SP), xp, 0.0).astype(BF)
                t = _dot(mh, xm)
                acc = t if acc is None else acc + t
            pairs.append(acc)
        y_intra = jnp.concatenate(pairs, axis=1)
        y_parts.append(y_intra + y_inter * ea_e[:, g * gw:(g + 1) * gw])
        upd = _dot_tn(xd[:, g * gw:(g + 1) * gw], bg)
        for hh in range(hpg):
            h = g * hpg + hh
            r0 = h * SP
            s_ref[r0:r0 + SP, :] = (s_old[hh * SP:(hh + 1) * SP, :] * jnp.exp(alast[:, h:h + 1])
                                    + upd[hh * SP:(hh + 1) * SP, :])
    yv = jnp.concatenate(y_parts, axis=1) + dx_ref[...] * xs
    yv = yv * _silu(z_ref[...].astype(F32))
    outs = []
    for g in range(SG):
        yg = yv[:, g * gw:(g + 1) * gw]
        outs.append(yg * lax.rsqrt(jnp.mean(yg * yg, axis=-1, keepdims=True) + RMS_EPS) * nw_ref[:, g * gw:(g + 1) * gw])
    y_ref[...] = jnp.concatenate(outs, axis=1).astype(BF)
    xbuf[0:8, :] = xbuf[L:L + 8, :]

    @pl.when(n == pl.num_programs(1) - 1)
    def _():
        st_ref[...] = s_ref[...]
        tail_ref[...] = xbuf[pl.ds(L + 5, CW - 1), :]


def ssd_prompt(xbc, zz, small, lp, nb, seq, chunk, z_col):
    nc = seq // chunk
    heads = jnp.arange(LANES)[:, None]
    e = (heads == (jnp.arange(SI)[None, :] // SP)).astype(BF)

    def pad_lanes(v, off):
        return jnp.zeros((1, LANES), F32).at[0, off:off + v.shape[0]].set(v)

    full = lambda shape: pl.BlockSpec(shape, lambda b, n: (0,) * len(shape))
    return pl.pallas_call(
        functools.partial(_ssd_kernel, chunk=chunk),
        grid=(nb, nc),
        in_specs=[pl.BlockSpec((chunk, SCONV), lambda b, n: (b * nc + n, 0)),
                  pl.BlockSpec((chunk, SI), lambda b, n: (b * nc + n, z_col)),
                  pl.BlockSpec((chunk, LANES), lambda b, n: (b * nc + n, 0)),
                  full((CW, SCONV)), full((1, SCONV)), full((1, LANES)), full((1, LANES)),
                  full((1, SI)), full((1, SI)), full((LANES, SI))],
        out_specs=[pl.BlockSpec((chunk, SI), lambda b, n: (b * nc + n, 0)),
                   pl.BlockSpec((None, SH * SP, SN), lambda b, n: (b, 0, 0)),
                   pl.BlockSpec((None, CW - 1, SCONV), lambda b, n: (b, 0, 0))],
        out_shape=[jax.ShapeDtypeStruct((nb * seq, SI), BF),
                   jax.ShapeDtypeStruct((nb, SH * SP, SN), F32),
                   jax.ShapeDtypeStruct((nb, CW - 1, SCONV), F32)],
        scratch_shapes=[pltpu.VMEM((chunk + 8, SCONV), F32), pltpu.VMEM((SH * SP, SN), F32)],
        compiler_params=_cparams(("parallel", "arbitrary")),
        name="ssd_chunk_scan",
    )(xbc, zz, small, lp['ssd_conv_w'], lp['ssd_conv_b'].reshape(1, SCONV),
      pad_lanes(lp['ssd_dt_bias'], L_DT), pad_lanes(-jnp.exp(lp['ssd_a_log']), L_DT),
      jnp.repeat(lp['ssd_d'], SP).reshape(1, SI), lp['ssd_norm'].reshape(1, SI), e)


def _gdn_stage_kernel(qkv_ref, z_ref, sm_ref, cw_ref, dtb_ref, na_ref, nw_ref,
                      o_ref, st_ref, tail_ref, xbuf, s_ref, *, chunk, nb):
    n = pl.program_id(0)
    C = chunk

    @pl.when(n == 0)
    def _():
        xbuf[:, 0:8, :] = jnp.zeros((nb, 8, NCONV), F32)
        s_ref[...] = jnp.zeros(s_ref.shape, F32)

    r128 = lax.broadcasted_iota(jnp.int32, (LANES, LANES), 0)
    c128 = lax.broadcasted_iota(jnp.int32, (LANES, LANES), 1)
    tri128 = (r128 >= c128).astype(BF)
    row = lax.broadcasted_iota(jnp.int32, (C, C), 0)
    col = lax.broadcasted_iota(jnp.int32, (C, C), 1)
    eye = (row == col).astype(F32)
    ch = []
    for b in range(nb):
        xbuf[b, 8:8 + C, :] = qkv_ref[b].astype(F32)
        y = cw_ref[0:1, :] * xbuf[b, pl.ds(5, C), :]
        for t in range(1, CW):
            y = y + cw_ref[t:t + 1, :] * xbuf[b, pl.ds(5 + t, C), :]
        qkv = _silu(y)
        sm = sm_ref[b]
        beta = _sigmoid(sm)
        g = na_ref[...] * _softplus(sm + dtb_ref[...])
        gc = _dot_sel(tri128, jnp.concatenate([g, jnp.zeros((LANES - C, LANES), F32)], axis=0))
        gc_t = gc.T
        for h in range(NH):
            qh = qkv[:, h * NK:(h + 1) * NK]
            kh = qkv[:, NKW + h * NK:NKW + (h + 1) * NK]
            vh = qkv[:, 2 * NKW + h * NV:2 * NKW + (h + 1) * NV]
            qh = qh * lax.rsqrt(jnp.sum(qh * qh, axis=-1, keepdims=True) + RMS_EPS) * (NK ** -0.5)
            kh = kh * lax.rsqrt(jnp.sum(kh * kh, axis=-1, keepdims=True) + RMS_EPS)
            la = L_NA + h
            gcol = gc[0:C, la:la + 1]
            grow = gc_t[la:la + 1, 0:C]
            glast = gc[C - 1:C, la:la + 1]
            bh = beta[:, L_NB + h:L_NB + h + 1]
            kb = kh * bh
            eg = jnp.exp(gcol)
            ch.append(dict(
                b=b, h=h, glast=glast,
                dmat=jnp.exp(jnp.where(row >= col, gcol - grow, -1e30)),
                kbb=kb.astype(BF), khb=kh.astype(BF), qhb=qh.astype(BF),
                rhs=jnp.concatenate([vh * bh, kb * eg], axis=1),
                qg=(qh * eg).astype(BF), kd=(kh * jnp.exp(glast - gcol)).astype(BF)))
    for c in ch:
        c['x'] = -jnp.where(row > col, _dot_nt(c['kbb'], c['khb']) * c['dmat'], 0.0)
        c['qk'] = (_dot_nt(c['qhb'], c['khb']) * c['dmat']).astype(BF)
    for c in ch:
        c['p'] = _dot3(c['x'], c['x'])
    span = 2
    first = True
    while span < C:
        for c in ch:
            tm = (eye + c['x']) if first else c['t']
            if first:
                c['t'] = tm + _dot3(tm, c['p'])
            else:
                c['t'] = tm + _dot(tm.astype(BF), c['p'].astype(BF))
            if span * 2 < C:
                pb = c['p'].astype(BF)
                c['p'] = _dot(pb, pb)
        first = False
        span *= 2
    for c in ch:
        c['sol'] = _dot3(c['t'], c['rhs'])
    for c in ch:
        s_old = s_ref[c['b'], c['h']]
        sb = s_old.astype(BF)
        c['s_old'] = s_old
        c['vn'] = (c['sol'][:, :NV] - _dot(c['sol'][:, NV:].astype(BF), sb)).astype(BF)
        c['oq'] = _dot(c['qg'], sb)
    for c in ch:
        b, h = c['b'], c['h']
        o = c['oq'] + _dot(c['qk'], c['vn'])
        s_ref[b, h] = c['s_old'] * jnp.exp(c['glast']) + _dot_tn(c['kd'], c['vn'])
        o = o * lax.rsqrt(jnp.mean(o * o, axis=-1, keepdims=True) + RMS_EPS) * nw_ref[...]
        sl = slice(h * NV, (h + 1) * NV)
        o_ref[b, :, sl] = (o * _silu(z_ref[b, :, sl].astype(F32))).astype(BF)
    xbuf[:, 0:8, :] = xbuf[:, C:C + 8, :]

    @pl.when(n == pl.num_programs(0) - 1)
    def _():
        st_ref[...] = s_ref[...]
        tail_ref[...] = xbuf[:, pl.ds(C + 5, CW - 1), :]


def gdn_prompt_staged(qkv, zz, small, lp, nb, seq, chunk, z_col):
    nc = seq // chunk

    def pad_lanes(v, off):
        return jnp.zeros((1, LANES), F32).at[0, off:off + v.shape[0]].set(v)

    full = lambda shape: pl.BlockSpec(shape, lambda n: (0,) * len(shape))
    return pl.pallas_call(
        functools.partial(_gdn_stage_kernel, chunk=chunk, nb=nb),
        grid=(nc,),
        in_specs=[pl.BlockSpec((nb, chunk, NCONV), lambda n: (0, n, 0)),
                  pl.BlockSpec((nb, chunk, NH * NV), lambda n: (0, n, z_col)),
                  pl.BlockSpec((nb, chunk, LANES), lambda n: (0, n, 0)),
                  full((CW, NCONV)), full((1, LANES)), full((1, LANES)), full((1, NV))],
        out_specs=[pl.BlockSpec((nb, chunk, NH * NV), lambda n: (0, n, 0)),
                   full((nb, NH, NK, NV)), full((nb, CW - 1, NCONV))],
        out_shape=[jax.ShapeDtypeStruct((nb, seq, NH * NV), BF),
                   jax.ShapeDtypeStruct((nb, NH, NK, NV), F32),
                   jax.ShapeDtypeStruct((nb, CW - 1, NCONV), F32)],
        scratch_shapes=[pltpu.VMEM((nb, chunk + 8, NCONV), F32), pltpu.VMEM((nb, NH, NK, NV), F32)],
        compiler_params=_cparams(("arbitrary",)),
        name="gdn_chunk_scan",
    )(qkv, zz, small, lp['dn_conv_w'], pad_lanes(lp['dn_dt_bias'], L_NA),
      pad_lanes(-jnp.exp(lp['dn_a_log']), L_NA), lp['dn_norm'].reshape(1, NV))


def _merge_kernel(x_ref, ao_ref, so_ref, no_ref, g0_ref, g1_ref, g2_ref, wa_ref, ws_ref, wn_ref, wo_ref,
                  lg_ref, lb_ref, wr_ref, wsg_ref, wsu_ref, wsd_ref, x1_ref, r_ref, lgt_ref):
    def gate(ref):
        return _sigmoid(ref[...].astype(F32))

    mixed = (gate(g0_ref) * _dot(ao_ref[...].astype(BF), wa_ref[...])
             + gate(g1_ref) * _dot(so_ref[...].astype(BF), ws_ref[...])
             + gate(g2_ref) * _dot(no_ref[...].astype(BF), wn_ref[...]))
    yv = ALPHA * x_ref[...] + _dot(mixed.astype(BF), wo_ref[...])
    mu = jnp.mean(yv, axis=-1, keepdims=True)
    yc = yv - mu
    var = jnp.mean(yc * yc, axis=-1, keepdims=True)
    x1 = yc * lax.rsqrt(var + LN_EPS) * lg_ref[...] + lb_ref[...]
    x1b = x1.astype(BF)
    x1_ref[...] = x1
    lgt_ref[...] = _dot(x1b, wr_ref[...])
    hsh = _silu(_dot(x1b, wsg_ref[...])) * _dot(x1b, wsu_ref[...])
    r_ref[...] = ALPHA * x1 + _dot(hsh.astype(BF), wsd_ref[...])


def merge_post(x, ao, so, no, gates, gate_col, wts, tm):
    m = x.shape[0]
    tm = min(tm, m)
    rowb = lambda c: pl.BlockSpec((tm, D), lambda i, c=c: (i, c))
    full = lambda a: pl.BlockSpec(a.shape, lambda i: (0,) * a.ndim)
    return pl.pallas_call(
        _merge_kernel,
        grid=(m // tm,),
        in_specs=[rowb(0), rowb(0), rowb(0), rowb(0), rowb(gate_col), rowb(gate_col + 1), rowb(gate_col + 2)]
        + [full(a) for a in wts],
        out_specs=[rowb(0), rowb(0), pl.BlockSpec((tm, LANES), lambda i: (i, 0))],
        out_shape=[jax.ShapeDtypeStruct((m, D), F32), jax.ShapeDtypeStruct((m, D), F32),
                   jax.ShapeDtypeStruct((m, LANES), F32)],
        compiler_params=_cparams(("parallel",)),
        name="merge_ln_router_shared",
    )(x, ao, so, no, gates, gates, gates, *wts)


def _moe_kernel(be_ref, nu_ref, x_ref, wg_ref, wu_ref, wd_ref, o_ref):
    i = pl.program_id(0)

    @pl.when(i < nu_ref[0])
    def _():
        x = x_ref[...].astype(BF)
        hid = _silu(_dot(x, wg_ref[...])) * _dot(x, wu_ref[...])
        o_ref[...] = _dot(hid.astype(BF), wd_ref[...]).astype(o_ref.dtype)

    @pl.when(i >= nu_ref[0])
    def _():
        o_ref[...] = jnp.zeros(o_ref.shape, o_ref.dtype)


def moe_experts(xb, block_expert, n_used, wg, wu, wd, bm, out_dtype):
    n_rows = xb.shape[0]
    ff = wg.shape[2]
    grid_spec = pltpu.PrefetchScalarGridSpec(
        num_scalar_prefetch=2,
        grid=(n_rows // bm,),
        in_specs=[pl.BlockSpec((bm, D), lambda i, be, nu: (i, 0)),
                  pl.BlockSpec((None, D, ff), lambda i, be, nu: (be[i], 0, 0)),
                  pl.BlockSpec((None, D, ff), lambda i, be, nu: (be[i], 0, 0)),
                  pl.BlockSpec((None, ff, D), lambda i, be, nu: (be[i], 0, 0))],
        out_specs=pl.BlockSpec((bm, D), lambda i, be, nu: (i, 0)),
    )
    return pl.pallas_call(
        _moe_kernel,
        grid_spec=grid_spec,
        out_shape=jax.ShapeDtypeStruct((n_rows, D), out_dtype),
        compiler_params=_cparams(("arbitrary",)),
        name="moe_grouped_ffn",
    )(block_expert, n_used, xb, wg, wu, wd)


def _route_kernel(lg_ref, bias_ref, idx_ref, gate_ref, sel_ref):
    lg = lg_ref[...]
    shape = lg.shape
    lane = lax.broadcasted_iota(jnp.int32, shape, 1)
    lanef = lane.astype(F32)
    neg = -jnp.inf
    scores = _sigmoid(lg)
    biased = jnp.where(lane < NE, scores + bias_ref[...], neg)

    def first_max(x):
        m = jnp.max(x, axis=-1, keepdims=True)
        first = jnp.min(jnp.where(x == m, lanef, 2.0 * LANES), axis=-1, keepdims=True)
        return m, first

    gsz = NE // NGRP
    in_group = [(lane >= g * gsz) & (lane < (g + 1) * gsz) for g in range(NGRP)]
    gscore = []
    for g in range(NGRP):
        xg = jnp.where(in_group[g], biased, neg)
        m1, first = first_max(xg)
        m2 = jnp.max(jnp.where(lanef == first, neg, xg), axis=-1, keepdims=True)
        gscore.append(m1 + m2)
    allowed = jnp.zeros(shape, jnp.bool_)
    for g in range(NGRP):
        rank = jnp.zeros_like(gscore[g])
        for g2 in range(NGRP):
            if g2 != g:
                ahead = (gscore[g2] > gscore[g]) | ((gscore[g2] == gscore[g]) & (g2 < g))
                rank = rank + jnp.where(ahead, 1.0, 0.0)
        allowed = allowed | ((rank < TOPG) & in_group[g])
    work = jnp.where(allowed, biased, neg)
    idx_out = jnp.zeros(shape, F32)
    gate_out = jnp.zeros(shape, F32)
    chosen = jnp.zeros(shape, jnp.bool_)
    for j in range(TOPK):
        _, first = first_max(work)
        sel = lanef == first
        sc = jnp.sum(jnp.where(sel, scores, 0.0), axis=-1, keepdims=True)
        idx_out = jnp.where(lane == j, first, idx_out)
        gate_out = jnp.where(lane == j, sc, gate_out)
        chosen = chosen | sel
        work = jnp.where(sel, neg, work)
    gate_ref[...] = gate_out / jnp.sum(gate_out, axis=-1, keepdims=True) * RSCALE
    sel_ref[...] = jnp.where(chosen, 1, 0).astype(jnp.int32)
    idx_ref[...] = idx_out.T[0:TOPK, :].astype(jnp.int32)


def _route(logits, router_bias):
    t = logits.shape[0]
    tm = min(256, t)
    bias = jnp.concatenate([router_bias, jnp.zeros((LANES - NE,), F32)]).reshape(1, LANES)
    rows = pl.BlockSpec((tm, LANES), lambda i: (i, 0))
    return pl.pallas_call(
        _route_kernel,
        grid=(t // tm,),
        in_specs=[rows, pl.BlockSpec((1, LANES), lambda i: (0, 0))],
        out_specs=[pl.BlockSpec((TOPK, tm), lambda i: (0, i)), rows, rows],
        out_shape=[jax.ShapeDtypeStruct((TOPK, t), jnp.int32), jax.ShapeDtypeStruct((t, LANES), F32),
                   jax.ShapeDtypeStruct((t, LANES), jnp.int32)],
        compiler_params=_cparams(("parallel",)),
        name="moe_route_topk",
    )(logits, bias)


def _dispatch(idx_t, sel, bm):
    t = sel.shape[0]
    na = t * TOPK
    assert na % bm == 0
    cum = jnp.cumsum(sel, axis=0)
    counts = cum[-1]
    padded = (counts + bm - 1) // bm * bm
    pad_end = jnp.cumsum(padded)
    dest_t = ((pad_end - padded)[None, :] + cum - sel).T
    experts = jnp.arange(NE, dtype=jnp.int32)[None, :, None]
    pos_t = jnp.sum(jnp.where(idx_t[:, None, :] == experts, dest_t[None, :NE, :], 0), axis=1).astype(jnp.int32)
    n_blocks = na // bm + NE
    starts = jnp.arange(n_blocks, dtype=jnp.int32)[:, None] * bm
    block_expert = jnp.minimum(jnp.sum((pad_end[None, :NE] <= starts).astype(jnp.int32), axis=1), NE - 1)
    n_used = (pad_end[NE - 1:NE] // bm).astype(jnp.int32)
    return block_expert, n_used, pos_t


def _scatter_rows_kernel(pos_ref, x_ref, zeros_ref, o_ref, sem, *, tm):
    del zeros_ref

    def row_copy(t, j):
        return pltpu.make_async_copy(x_ref.at[pl.ds(t, 1), :], o_ref.at[pl.ds(pos_ref[j, t], 1), :], sem)

    def issue(t, carry):
        for j in range(TOPK):
            row_copy(t, j).start()
        return carry

    def drain(t, carry):
        for j in range(TOPK):
            row_copy(t, j).wait()
        return carry

    lax.fori_loop(0, tm, issue, 0)
    lax.fori_loop(0, tm, drain, 0)


def scatter_rows(x, pos_t, n_rows):
    t = x.shape[0]
    tm = min(256, t)
    return pl.pallas_call(
        functools.partial(_scatter_rows_kernel, tm=tm),
        grid=(t // tm,),
        in_specs=[pl.BlockSpec((TOPK, tm), lambda i: (0, i), memory_space=pltpu.SMEM),
                  pl.BlockSpec((tm, D), lambda i: (i, 0)),
                  pl.BlockSpec(memory_space=pl.ANY)],
        out_specs=pl.BlockSpec(memory_space=pl.ANY),
        out_shape=jax.ShapeDtypeStruct((n_rows, D), F32),
        scratch_shapes=[pltpu.SemaphoreType.DMA],
        input_output_aliases={2: 0},
        compiler_params=_cparams(("arbitrary",)),
        name="moe_dispatch_rows",
    )(pos_t, x, jnp.zeros((n_rows, D), F32))


def _combine_kernel(yg_ref, gate_ref, r_ref, g_ref, b_ref, o_ref):
    acc = r_ref[...]
    gate = gate_ref[...]
    for j in range(TOPK):
        acc = acc + yg_ref[j].astype(F32) * gate[:, j:j + 1]
    mu = jnp.mean(acc, axis=-1, keepdims=True)
    yc = acc - mu
    var = jnp.mean(yc * yc, axis=-1, keepdims=True)
    o_ref[...] = yc * lax.rsqrt(var + LN_EPS) * g_ref[...] + b_ref[...]


def combine_ln(yg, gate, r, g, b, tm):
    t = r.shape[0]
    tm = min(tm, t)
    return pl.pallas_call(
        _combine_kernel,
        grid=(t // tm,),
        in_specs=[pl.BlockSpec((TOPK, tm, D), lambda i: (0, i, 0)), pl.BlockSpec((tm, LANES), lambda i: (i, 0)),
                  pl.BlockSpec((tm, D), lambda i: (i, 0)), pl.BlockSpec((1, D), lambda i: (0, 0)),
                  pl.BlockSpec((1, D), lambda i: (0, 0))],
        out_specs=pl.BlockSpec((tm, D), lambda i: (i, 0)),
        out_shape=jax.ShapeDtypeStruct((t, D), F32),
        compiler_params=_cparams(("parallel",)),
        name="moe_combine_ln",
    )(yg, gate, r, g.reshape(1, D), b.reshape(1, D))


def moe_finish(r, x1, logits, lp, bm):
    t = r.shape[0]
    idx_t, gate, sel = _route(logits, lp['router_bias'])
    block_expert, n_used, pos_t = _dispatch(idx_t, sel, bm)
    xb = scatter_rows(x1, pos_t, (t * TOPK // bm + NE) * bm)
    yb = moe_experts(xb, block_expert, n_used, lp['w_exp_gate'], lp['w_exp_up'], lp['w_exp_down'], bm, BF)
    yg = yb[pos_t.reshape(TOPK * t)].reshape(TOPK, t, D)
    return combine_ln(yg, gate, r, lp['ln2_g'], lp['ln2_b'], 256)


def _rms(x, g):
    return x * lax.rsqrt(jnp.mean(x * x, -1, keepdims=True) + RMS_EPS) * g


def _prep_layer(l, w_in, p):
    wi = w_in[l]
    small = jnp.concatenate([wi[:, R_DT:R_NQKV], wi[:, R_NB:R_GATE], jnp.zeros((D, LANES - 32), wi.dtype)], axis=1)
    lp = {k: v[l] for k, v in p.items()}
    lp.update(
        w_qkv=wi[:, R_Q:R_Z].astype(BF),
        w_zg=jnp.concatenate([wi[:, R_Z:R_XBC], wi[:, R_NZ:R_NB], wi[:, R_GATE:]], axis=1).astype(BF),
        w_xbc=wi[:, R_XBC:R_DT].astype(BF),
        w_nqkv=wi[:, R_NQKV:R_NZ].astype(BF),
        w_small=small.astype(BF),
        w_router_p=jnp.concatenate([lp['w_router'], jnp.zeros((D, LANES - NE), F32)], axis=1).astype(BF),
    )
    for k in ('w_attn_o', 'w_ssd_o', 'w_dn_o', 'w_out', 'w_sh_gate', 'w_sh_up', 'w_sh_down',
              'w_exp_gate', 'w_exp_up', 'w_exp_down'):
        lp[k] = lp[k].astype(BF)
    return lp


def _merge_weights(lp):
    return [lp['w_attn_o'], lp['w_ssd_o'], lp['w_dn_o'], lp['w_out'], lp['ln1_g'].reshape(1, D),
            lp['ln1_b'].reshape(1, D), lp['w_router_p'], lp['w_sh_gate'], lp['w_sh_up'], lp['w_sh_down']]


def prompt_layer(x, lp, layer_idx, tabs):
    lam_init = 0.8 - 0.6 * math.exp(-0.3 * layer_idx)
    q, kf, kb, vf, vb = qkv_project(x, lp['w_qkv'], tabs, 512)
    zg = matmul(x, lp['w_zg'], BF, tm=PROJ_ROWS)
    xbc = matmul(x, lp['w_xbc'], BF, tm=PROJ_ROWS)
    nqkv = matmul(x, lp['w_nqkv'], BF, tm=PROJ_ROWS)
    small = matmul(x, lp['w_small'], F32, tm=PROJ_ROWS)
    ao = flash_diff_attention(q, kb, vb, lp['attn_lambda'], lp['attn_subln'], lam_init, NB, SEQ, ATT_BLK)
    so, ssd_h, ssd_tail = ssd_prompt(xbc, zg, small, lp, NB, SEQ, SSD_CHUNK, 0)
    no, dn_s, dn_tail = gdn_prompt_staged(nqkv.reshape(NB, SEQ, NCONV), zg.reshape(NB, SEQ, -1),
                                          small.reshape(NB, SEQ, LANES), lp, NB, SEQ, GDN_CHUNK, 1)
    no = no.reshape(NB * SEQ, NH * NV)
    x1, r, logits = merge_post(x, ao, so, no, zg, 2, _merge_weights(lp), 512)
    x2 = moe_finish(r, x1, logits, lp, MOE_BLK_PROMPT)
    return (x2, kf.reshape(NB, SEQ, KVH, 2 * HD), vf.reshape(NB, SEQ, KVH, 2 * HD),
            ssd_h.reshape(NB, SH, SP, SN), ssd_tail, dn_s, dn_tail)


def _conv_step(hist, new, w, b=None):
    xp = jnp.concatenate([hist, new[:, None, :]], axis=1)
    y = jnp.sum(xp * w[None], axis=1)
    if b is not None:
        y = y + b
    return jax.nn.silu(y), xp[:, 1:]


def decode_layer(x, lp, layer_idx, tabs, cache_k, cache_v, page_table, ssd_h0, ssd_hist, dn_s0, dn_hist):
    lam_init = 0.8 - 0.6 * math.exp(-0.3 * layer_idx)
    q, kf, _, vf, _ = qkv_project(x, lp['w_qkv'], tabs, DB)
    zg = matmul(x, lp['w_zg'], F32, tm=DB)
    xbc_raw = matmul(x, lp['w_xbc'], F32, tm=DB)
    nqkv_raw = matmul(x, lp['w_nqkv'], F32, tm=DB)
    small = matmul(x, lp['w_small'], F32, tm=DB)
    qh = q.reshape(DB, KVH, 2, 2, HD)
    q16 = jnp.einsum('bhgcd,ce->bhgced', qh, jnp.eye(2, dtype=BF)).reshape(DB, 16, 2 * HD)
    k16 = jnp.repeat(kf.reshape(DB, KVH, 2 * HD), 4, axis=1)
    v16 = jnp.repeat(vf.reshape(DB, KVH, 2 * HD), 4, axis=1)
    oc = paged_attention(q16, k16, v16, cache_k, cache_v, page_table, layer_idx, PAGES_PER_STEP)
    oc = oc.reshape(DB, AH, 2, 2 * HD)
    lq = lp['attn_lambda']
    lam = jnp.exp(jnp.sum(lq[0] * lq[1])) - jnp.exp(jnp.sum(lq[2] * lq[3])) + lam_init
    o = oc[:, :, 0] - lam * oc[:, :, 1]
    ao = (_rms(o, lp['attn_subln']) * (1.0 - lam_init)).reshape(DB, AH * 2 * HD)
    xbc, ssd_tail = _conv_step(ssd_hist, xbc_raw, lp['ssd_conv_w'], lp['ssd_conv_b'])
    xs = xbc[:, :SI].reshape(DB, SH, SP)
    bm = jnp.repeat(xbc[:, SI:SI + SG * SN].reshape(DB, SG, SN), SH // SG, axis=1)
    cm = jnp.repeat(xbc[:, SI + SG * SN:].reshape(DB, SG, SN), SH // SG, axis=1)
    dt = jax.nn.softplus(small[:, L_DT:L_DT + SH] + lp['ssd_dt_bias'])
    a_coef = -jnp.exp(lp['ssd_a_log'])
    ssd_h = ssd_h0 * jnp.exp(dt * a_coef)[..., None, None] + (xs * dt[..., None])[..., :, None] * bm[..., None, :]
    ys = jnp.sum(ssd_h * cm[..., None, :], -1) + lp['ssd_d'][:, None] * xs
    gw = SI // SG
    ys = ys.reshape(DB, SG, gw) * jax.nn.silu(zg[:, :SI]).reshape(DB, SG, gw)
    so = _rms(ys, lp['ssd_norm'].reshape(SG, gw)).reshape(DB, SI)
    qkv, dn_tail = _conv_step(dn_hist, nqkv_raw, lp['dn_conv_w'])
    nq = qkv[:, :NKW].reshape(DB, NH, NK)
    nk = qkv[:, NKW:2 * NKW].reshape(DB, NH, NK)
    nv = qkv[:, 2 * NKW:].reshape(DB, NH, NV)
    nq = nq * lax.rsqrt(jnp.sum(nq * nq, -1, keepdims=True) + RMS_EPS) * (NK ** -0.5)
    nk = nk * lax.rsqrt(jnp.sum(nk * nk, -1, keepdims=True) + RMS_EPS)
    beta = jax.nn.sigmoid(small[:, L_NB:L_NB + NH])
    gl = -jnp.exp(lp['dn_a_log']) * jax.nn.softplus(small[:, L_NA:L_NA + NH] + lp['dn_dt_bias'])
    s = dn_s0 * jnp.exp(gl)[..., None, None]
    delta = (nv - jnp.sum(s * nk[..., :, None], -2)) * beta[..., None]
    dn_s = s + nk[..., :, None] * delta[..., None, :]
    on = jnp.sum(dn_s * nq[..., :, None], -2)
    no = (_rms(on, lp['dn_norm']) * jax.nn.silu(zg[:, SI:2 * SI].reshape(DB, NH, NV))).reshape(DB, NH * NV)
    x1, r, logits = merge_post(x, ao, so, no, zg, 2, _merge_weights(lp), DB)
    x2 = moe_finish(r, x1, logits, lp, MOE_BLK_DECODE)
    return (x2, kf.reshape(DB, 1, KVH, 2 * HD), vf.reshape(DB, 1, KVH, 2 * HD), ssd_h, ssd_tail, dn_s, dn_tail)


def kernel(x_prompt, x_sample, cache_k, cache_v, state_ssd, state_ssd_conv, state_dn, state_dn_conv, page_table,
           w_in, attn_lambda, attn_subln, w_attn_o, ssd_conv_w, ssd_conv_b, ssd_dt_bias, ssd_a_log, ssd_d, ssd_norm,
           w_ssd_o, dn_conv_w, dn_dt_bias, dn_a_log, dn_norm, w_dn_o, w_out, ln1_g, ln1_b, w_router, router_bias,
           w_exp_gate, w_exp_up, w_exp_down, w_sh_gate, w_sh_up, w_sh_down, ln2_g, ln2_b):
    params = dict(attn_lambda=attn_lambda, attn_subln=attn_subln, w_attn_o=w_attn_o, ssd_conv_w=ssd_conv_w,
                  ssd_conv_b=ssd_conv_b, ssd_dt_bias=ssd_dt_bias, ssd_a_log=ssd_a_log, ssd_d=ssd_d,
                  ssd_norm=ssd_norm, w_ssd_o=w_ssd_o, dn_conv_w=dn_conv_w, dn_dt_bias=dn_dt_bias,
                  dn_a_log=dn_a_log, dn_norm=dn_norm, w_dn_o=w_dn_o, w_out=w_out, ln1_g=ln1_g, ln1_b=ln1_b,
                  w_router=w_router, router_bias=router_bias, w_exp_gate=w_exp_gate, w_exp_up=w_exp_up,
                  w_exp_down=w_exp_down, w_sh_gate=w_sh_gate, w_sh_up=w_sh_up, w_sh_down=w_sh_down,
                  ln2_g=ln2_g, ln2_b=ln2_b)
    tabs_p = _rope_tables(jnp.arange(SEQ, dtype=jnp.int32))
    tabs_d = _rope_tables(jnp.full((DB,), PAST, dtype=jnp.int32))
    xp = x_prompt.reshape(NB * SEQ, D)
    xs = x_sample.reshape(DB, D)
    outs = [[] for _ in range(12)]
    for l in range(DEPTH):
        lp = _prep_layer(l, w_in, params)
        xp, kp, vp, hp, cp, sp, dp = prompt_layer(xp, lp, l, tabs_p)
        xs, ks, vs, hs, cs, ss, ds = decode_layer(xs, lp, l, tabs_d, cache_k, cache_v, page_table,
                                                  state_ssd[l], state_ssd_conv[l], state_dn[l], state_dn_conv[l])
        for lst, val in zip(outs, (kp, vp, ks, vs, hp, cp, hs, cs, sp, dp, ss, ds)):
            lst.append(val)
    return (xp.reshape(NB, SEQ, D), xs.reshape(DB, 1, D)) + tuple(jnp.stack(o) for o in outs)
```

```python
import functools
import math

import jax
import jax.numpy as jnp
from jax import lax
from jax.experimental import pallas as pl
from jax.experimental.pallas import tpu as pltpu

D = 1024
NB, SEQ = 4, 4096
DEPTH = 2
DB = 32
PAST = 16384
PAGE = 128
AH, KVH, HD = 8, 4, 64
ROT = 16
THETA = 500000.0
SH, SP, SI, SG, SN = 16, 64, 1024, 2, 128
SCONV = SI + 2 * SG * SN
NH, NK, NV = 8, 128, 128
NKW = 1024
NCONV = 3072
CW = 4
NE, TOPK, NGRP, TOPG = 64, 8, 8, 4
RSCALE = 2.5
ALPHA = (2 * DEPTH) ** 0.25
LN_EPS = 1e-5
RMS_EPS = 1e-6

R_Q, R_K, R_V, R_Z, R_XBC, R_DT, R_NQKV, R_NZ, R_NB, R_NA, R_GATE = (
    0, 1024, 1536, 2048, 3072, 4608, 4624, 7696, 8720, 8728, 8736)
L_DT, L_NB, L_NA = 0, 16, 24

LANES = 128
VMEM_LIMIT = 56 * 1024 * 1024
SSD_CHUNK = 128
GDN_CHUNK = 64
ATT_BLK = 512
FLASH_ROWS = 32
PAGES_PER_STEP = 16
ROUTE_ROWS = 1024
PROJ_ROWS = 2048
MOE_BLK_PROMPT = 512
MOE_BLK_DECODE = 16

BF = jnp.bfloat16
F32 = jnp.float32


def _cparams(sem):
    return pltpu.CompilerParams(dimension_semantics=sem, vmem_limit_bytes=VMEM_LIMIT)


def _dot(a, b):
    return jnp.dot(a, b, preferred_element_type=F32)


def _dot_nt(a, b):
    return lax.dot_general(a, b, (((1,), (1,)), ((), ())), preferred_element_type=F32)


def _dot_tn(a, b):
    return lax.dot_general(a, b, (((0,), (0,)), ((), ())), preferred_element_type=F32)


def _split2(x):
    hi = x.astype(BF)
    lo = (x - hi.astype(F32)).astype(BF)
    return hi, lo


def _split3(x):
    hi = x.astype(BF)
    r = x - hi.astype(F32)
    mid = r.astype(BF)
    lo = (r - mid.astype(F32)).astype(BF)
    return hi, mid, lo


def _dot_sel(sel, x):
    hi, mid, lo = _split3(x)
    return _dot(sel, hi) + _dot(sel, mid) + _dot(sel, lo)


def _dot_x_sel(x, sel):
    hi, mid, lo = _split3(x)
    return _dot(hi, sel) + _dot(mid, sel) + _dot(lo, sel)


def _dot3(a, b):
    ah, al = _split2(a)
    bh, bl = _split2(b)
    return _dot(ah, bh) + _dot(ah, bl) + _dot(al, bh)


def _sigmoid(x):
    return 1.0 / (1.0 + jnp.exp(-x))


def _silu(x):
    return x * _sigmoid(x)


def _softplus(x):
    return jnp.maximum(x, 0.0) + jnp.log(1.0 + jnp.exp(-jnp.abs(x)))


def _mm_kernel(x_ref, w_ref, o_ref, xs_ref):
    @pl.when(pl.program_id(1) == 0)
    def _():
        xs_ref[...] = x_ref[...].astype(BF)

    o_ref[...] = _dot(xs_ref[...], w_ref[...]).astype(o_ref.dtype)


def matmul(x, w, out_dtype, tm=512, tn=512):
    m, k = x.shape
    n = w.shape[1]
    tm, tn = min(tm, m), min(tn, n)
    assert m % tm == 0 and n % tn == 0, (m, n, tm, tn)
    return pl.pallas_call(
        _mm_kernel,
        grid=(m // tm, n // tn),
        in_specs=[pl.BlockSpec((tm, k), lambda i, j: (i, 0)),
                  pl.BlockSpec((k, tn), lambda i, j: (0, j))],
        out_specs=pl.BlockSpec((tm, tn), lambda i, j: (i, j)),
        out_shape=jax.ShapeDtypeStruct((m, n), out_dtype),
        scratch_shapes=[pltpu.VMEM((tm, k), BF)],
        compiler_params=_cparams(("parallel", "arbitrary")),
        name="proj_matmul",
    )(x, w)


def _rope_tables(pos):
    half = ROT // 2
    inv = THETA ** (-jnp.arange(half, dtype=F32) * 2.0 / ROT)
    ang = pos.astype(F32)[:, None] * inv
    cos, sin = jnp.cos(ang), jnp.sin(ang)
    t = pos.shape[0]
    one, zero, z8 = jnp.ones((t, HD - ROT), F32), jnp.zeros((t, HD - ROT), F32), jnp.zeros((t, half), F32)
    c = jnp.concatenate([cos, cos, one, cos, cos, one], 1)
    sa = jnp.concatenate([-sin, z8, zero, -sin, z8, zero], 1)
    sb = jnp.concatenate([z8, sin, zero, z8, sin, zero], 1)
    return c, sa, sb


def _qkv_kernel(x_ref, w_ref, c_ref, sa_ref, sb_ref, q_ref, kf_ref, kb_ref, vf_ref, vb_ref):
    acc = _dot(x_ref[...].astype(BF), w_ref[...])
    c, sa, sb = c_ref[...], sa_ref[...], sb_ref[...]

    def rot(xg):
        return xg * c + pltpu.roll(xg, LANES - ROT // 2, 1) * sa + pltpu.roll(xg, ROT // 2, 1) * sb

    for h in range(AH):
        sl = slice(h * LANES, (h + 1) * LANES)
        q_ref[:, sl] = (rot(acc[:, sl]) * (HD ** -0.5)).astype(BF)
    for h in range(KVH):
        sl = slice(h * LANES, (h + 1) * LANES)
        kr = rot(acc[:, R_K + h * LANES:R_K + (h + 1) * LANES])
        kf_ref[:, sl] = kr
        kb_ref[:, sl] = kr.astype(BF)
    v = acc[:, R_V:R_Z]
    vf_ref[...] = v
    vb_ref[...] = v.astype(BF)


def qkv_project(x, w, tabs, tm):
    m = x.shape[0]
    tm = min(tm, m)
    nt = tabs[0].shape[0] // tm
    kw = KVH * 2 * HD
    tab_spec = pl.BlockSpec((tm, LANES), lambda i: (i % nt, 0))
    return pl.pallas_call(
        _qkv_kernel,
        grid=(m // tm,),
        in_specs=[pl.BlockSpec((tm, D), lambda i: (i, 0)),
                  pl.BlockSpec((D, R_Z), lambda i: (0, 0)),
                  tab_spec, tab_spec, tab_spec],
        out_specs=[pl.BlockSpec((tm, AH * 2 * HD), lambda i: (i, 0)),
                   pl.BlockSpec((tm, kw), lambda i: (i, 0)), pl.BlockSpec((tm, kw), lambda i: (i, 0)),
                   pl.BlockSpec((tm, kw), lambda i: (i, 0)), pl.BlockSpec((tm, kw), lambda i: (i, 0))],
        out_shape=[jax.ShapeDtypeStruct((m, AH * 2 * HD), BF),
                   jax.ShapeDtypeStruct((m, kw), F32), jax.ShapeDtypeStruct((m, kw), BF),
                   jax.ShapeDtypeStruct((m, kw), F32), jax.ShapeDtypeStruct((m, kw), BF)],
        compiler_params=_cparams(("parallel",)),
        name="qkv_rope",
    )(x, w, *tabs)


def _lambda_value(lq, lam_init):
    a = jnp.sum(lq[0:1, :] * lq[1:2, :], axis=-1, keepdims=True)
    b = jnp.sum(lq[2:3, :] * lq[3:4, :], axis=-1, keepdims=True)
    return jnp.exp(a) - jnp.exp(b) + lam_init


def _flash_kernel(qi_ref, ki_ref, lq_ref, sub_ref, q_ref, k_ref, v_ref, o_ref, qs_ref, m_ref, acc_ref, s_ref, p_ref,
                  *, blk, lam_init):
    qi, ki = qi_ref[pl.program_id(2)], ki_ref[pl.program_id(2)]

    @pl.when(ki == 0)
    def _():
        q = q_ref[...]
        lane = lax.broadcasted_iota(jnp.int32, (blk, LANES), 1)
        for c in range(2):
            msk = (lane < HD) if c == 0 else (lane >= HD)
            for g in range(2):
                qs_ref[c, g * blk:(g + 1) * blk, :] = jnp.where(msk, q[:, g * LANES:(g + 1) * LANES], 0)
        m_ref[...] = jnp.full(m_ref.shape, -jnp.inf, F32)
        acc_ref[...] = jnp.zeros(acc_ref.shape, F32)

    def update(masked):
        k = k_ref[...]
        vext = jnp.concatenate([v_ref[...], jnp.ones((blk, LANES), BF)], axis=1)
        units = [(c, g) for g in range(2) for c in range(2)]
        for u, (c, g) in enumerate(units):
            s_ref[u] = _dot_nt(qs_ref[c, g * blk:(g + 1) * blk, :], k)
        if masked:
            keep = (lax.broadcasted_iota(jnp.int32, (blk, blk), 1) <= lax.broadcasted_iota(jnp.int32, (blk, blk), 0))
            diff = (lax.broadcasted_iota(jnp.int32, (FLASH_ROWS, blk), 1)
                    - lax.broadcasted_iota(jnp.int32, (FLASH_ROWS, blk), 0))
        for u, (c, g) in enumerate(units):
            rows = slice(g * blk, (g + 1) * blk)
            s_full = s_ref[u]
            if masked:
                s_full = jnp.where(keep, s_full, -jnp.inf)
            m_old = m_ref[c, rows]
            m_new = jnp.maximum(m_old, jnp.max(s_full, axis=-1, keepdims=True))
            alpha = jnp.exp(m_old - m_new)
            for r in range(0, blk, FLASH_ROWS):
                s = s_ref[u, r:r + FLASH_ROWS, :]
                if masked:
                    s = jnp.where(diff <= r, s, -jnp.inf)
                p_ref[u, r:r + FLASH_ROWS, :] = jnp.exp(s - m_new[r:r + FLASH_ROWS]).astype(BF)
            m_ref[c, rows] = m_new
            acc_ref[c, rows] = alpha * acc_ref[c, rows] + _dot(p_ref[u], vext)

    @pl.when(ki < qi)
    def _():
        update(False)

    @pl.when(ki == qi)
    def _():
        update(True)
        lam = _lambda_value(lq_ref[...], lam_init)
        a0, a1 = acc_ref[0], acc_ref[1]
        o = a0[:, :LANES] / a0[:, LANES:] - lam * (a1[:, :LANES] / a1[:, LANES:])
        o = o * lax.rsqrt(jnp.mean(o * o, axis=-1, keepdims=True) + RMS_EPS) * sub_ref[...] * (1.0 - lam_init)
        o_ref[:, 0:LANES] = o[0:blk].astype(BF)
        o_ref[:, LANES:2 * LANES] = o[blk:2 * blk].astype(BF)


def flash_diff_attention(q, k, v, lq, subln, lam_init, nb, seq, blk):
    nblk = seq // blk
    pairs = [(i, j) for i in range(nblk) for j in range(i + 1)]
    qi_arr = jnp.asarray([p[0] for p in pairs], jnp.int32)
    ki_arr = jnp.asarray([p[1] for p in pairs], jnp.int32)
    kern = functools.partial(_flash_kernel, blk=blk, lam_init=lam_init)
    grid_spec = pltpu.PrefetchScalarGridSpec(
        num_scalar_prefetch=2,
        grid=(nb, KVH, len(pairs)),
        in_specs=[pl.BlockSpec((4, HD), lambda b, h, s, qa, ka: (0, 0)),
                  pl.BlockSpec((1, LANES), lambda b, h, s, qa, ka: (0, 0)),
                  pl.BlockSpec((blk, 2 * LANES), lambda b, h, s, qa, ka: (b * nblk + qa[s], h)),
                  pl.BlockSpec((blk, LANES), lambda b, h, s, qa, ka: (b * nblk + ka[s], h)),
                  pl.BlockSpec((blk, LANES), lambda b, h, s, qa, ka: (b * nblk + ka[s], h))],
        out_specs=pl.BlockSpec((blk, 2 * LANES), lambda b, h, s, qa, ka: (b * nblk + qa[s], h)),
        scratch_shapes=[pltpu.VMEM((2, 2 * blk, LANES), BF), pltpu.VMEM((2, 2 * blk, 1), F32),
                        pltpu.VMEM((2, 2 * blk, 2 * LANES), F32),
                        pltpu.VMEM((4, blk, blk), F32), pltpu.VMEM((4, blk, blk), BF)],
    )
    return pl.pallas_call(
        kern,
        grid_spec=grid_spec,
        out_shape=jax.ShapeDtypeStruct((nb * seq, AH * 2 * HD), BF),
        compiler_params=_cparams(("parallel", "parallel", "arbitrary")),
        name="flash_diff_attn",
    )(qi_arr, ki_arr, lq, subln.reshape(1, LANES), q, k, v)


def _paged_kernel(pt_ref, q_ref, ks_ref, vs_ref, *refs, pps):
    kp, vp = refs[:pps], refs[pps:2 * pps]
    o_ref, m_ref, l_ref, acc_ref = refs[2 * pps:]
    j = pl.program_id(1)
    q = q_ref[...]
    pw = PAGE * KVH

    @pl.when(j == 0)
    def _():
        ks = ks_ref[...].astype(BF).astype(F32)
        m_ref[...] = jnp.sum(q.astype(F32) * ks, axis=-1, keepdims=True)
        l_ref[...] = jnp.ones(l_ref.shape, F32)
        acc_ref[...] = vs_ref[...].astype(BF).astype(F32)

    row_head = lax.broadcasted_iota(jnp.int32, (16, pw), 0) // 4
    col_head = lax.broadcasted_iota(jnp.int32, (16, pw), 1) % KVH
    own = row_head == col_head
    s = jnp.concatenate([jnp.where(own, _dot_nt(q, kp[i][...].astype(BF)), -jnp.inf) for i in range(pps)], axis=1)
    m_old = m_ref[...]
    m_new = jnp.maximum(m_old, jnp.max(s, axis=-1, keepdims=True))
    alpha = jnp.exp(m_old - m_new)
    p = jnp.exp(s - m_new)
    l_ref[...] = alpha * l_ref[...] + jnp.sum(p, axis=-1, keepdims=True)
    pv = _dot(p[:, 0:pw].astype(BF), vp[0][...].astype(BF))
    for i in range(1, pps):
        pv = pv + _dot(p[:, i * pw:(i + 1) * pw].astype(BF), vp[i][...].astype(BF))
    acc_ref[...] = alpha * acc_ref[...] + pv
    m_ref[...] = m_new

    @pl.when(j == pl.num_programs(1) - 1)
    def _():
        o_ref[...] = acc_ref[...] / l_ref[...]


def paged_attention(q16, k16, v16, cache_k, cache_v, page_table, layer, pps):
    nseq, npg = page_table.shape
    pw = PAGE * KVH
    ck = cache_k.reshape(cache_k.shape[0], cache_k.shape[1], pw, 2 * HD)
    cv = cache_v.reshape(cache_v.shape[0], cache_v.shape[1], pw, 2 * HD)

    def page_spec(i):
        return pl.BlockSpec((None, None, pw, 2 * HD), lambda b, j, pt: (layer, pt[b, j * pps + i], 0, 0))

    row = pl.BlockSpec((None, 16, 2 * HD), lambda b, j, pt: (b, 0, 0))
    grid_spec = pltpu.PrefetchScalarGridSpec(
        num_scalar_prefetch=1,
        grid=(nseq, npg // pps),
        in_specs=[row, row, row] + [page_spec(i) for i in range(pps)] + [page_spec(i) for i in range(pps)],
        out_specs=row,
        scratch_shapes=[pltpu.VMEM((16, 1), F32), pltpu.VMEM((16, 1), F32), pltpu.VMEM((16, 2 * HD), F32)],
    )
    return pl.pallas_call(
        functools.partial(_paged_kernel, pps=pps),
        grid_spec=grid_spec,
        out_shape=jax.ShapeDtypeStruct((nseq, 16, 2 * HD), F32),
        compiler_params=_cparams(("parallel", "arbitrary")),
        name="paged_diff_attn",
    )(page_table, q16, k16, v16, *([ck] * pps), *([cv] * pps))


def _ssd_kernel(xbc_ref, z_ref, sm_ref, cw_ref, cb_ref, dtb_ref, ac_ref, dx_ref, nw_ref, e_ref,
                y_ref, st_ref, tail_ref, xbuf, s_ref, *, chunk):
    n = pl.program_id(1)
    L = chunk

    @pl.when(n == 0)
    def _():
        xbuf[0:8, :] = jnp.zeros((8, SCONV), F32)
        s_ref[...] = jnp.zeros(s_ref.shape, F32)

    xbuf[8:8 + L, :] = xbc_ref[...].astype(F32)
    y = cb_ref[...] + cw_ref[0:1, :] * xbuf[pl.ds(5, L), :]
    for t in range(1, CW):
        y = y + cw_ref[t:t + 1, :] * xbuf[pl.ds(5 + t, L), :]
    xbc = _silu(y)
    xs, bm, cm = xbc[:, :SI], xbc[:, SI:SI + SG * SN], xbc[:, SI + SG * SN:]

    dt = _softplus(sm_ref[...] + dtb_ref[...])
    a = dt * ac_ref[...]
    row = lax.broadcasted_iota(jnp.int32, (L, L), 0)
    col = lax.broadcasted_iota(jnp.int32, (L, L), 1)
    causal = row >= col
    acum = _dot_sel(causal.astype(BF), a)
    acum_t = acum.T
    alast = acum[L - 1:L, :]
    e = e_ref[...]
    dt_e = _dot_x_sel(dt, e)
    ea_e = _dot_x_sel(jnp.exp(acum), e)
    dend_e = _dot_x_sel(jnp.exp(alast - acum), e)
    xdt = xs * dt_e
    xd = (xdt * dend_e).astype(BF)
    lane = lax.broadcasted_iota(jnp.int32, (L, LANES), 1)
    gw = SI // SG
    hpg = SH // SG
    y_parts = []
    for g in range(SG):
        cg = cm[:, g * SN:(g + 1) * SN].astype(BF)
        bg = bm[:, g * SN:(g + 1) * SN].astype(BF)
        s_old = s_ref[g * gw:(g + 1) * gw, :]
        y_inter = _dot_nt(cg, s_old.astype(BF))
        cb = _dot_nt(cg, bg)
        pairs = []
        for jp in range(hpg // 2):
            h0 = g * hpg + 2 * jp
            xp = xdt[:, h0 * SP:h0 * SP + LANES]
            acc = None
            for d in range(2):
                h = h0 + d
                seg = jnp.where(causal, jnp.exp(acum[:, h:h + 1] - acum_t[h:h + 1, :]), 0.0)
                mh = (cb * seg).astype(BF)
                xm = jnp.where((lane < SP) if d == 0 else (lane >= SP), xp, 0.0).astype(BF)
                t = _dot(mh, xm)
                acc = t if acc is None else acc + t
            pairs.append(acc)
        y_intra = jnp.concatenate(pairs, axis=1)
        y_parts.append(y_intra + y_inter * ea_e[:, g * gw:(g + 1) * gw])
        upd = _dot_tn(xd[:, g * gw:(g + 1) * gw], bg)
        for hh in range(hpg):
            h = g * hpg + hh
            r0 = h * SP
            s_ref[r0:r0 + SP, :] = (s_old[hh * SP:(hh + 1) * SP, :] * jnp.exp(alast[:, h:h + 1])
                                    + upd[hh * SP:(hh + 1) * SP, :])
    yv = jnp.concatenate(y_parts, axis=1) + dx_ref[...] * xs
    yv = yv * _silu(z_ref[...].astype(F32))
    outs = []
    for g in range(SG):
        yg = yv[:, g * gw:(g + 1) * gw]
        outs.append(yg * lax.rsqrt(jnp.mean(yg * yg, axis=-1, keepdims=True) + RMS_EPS) * nw_ref[:, g * gw:(g + 1) * gw])
    y_ref[...] = jnp.concatenate(outs, axis=1).astype(BF)
    xbuf[0:8, :] = xbuf[L:L + 8, :]

    @pl.when(n == pl.num_programs(1) - 1)
    def _():
        st_ref[...] = s_ref[...]
        tail_ref[...] = xbuf[pl.ds(L + 5, CW - 1), :]


def ssd_prompt(xbc, zz, small, lp, nb, seq, chunk, z_col):
    nc = seq // chunk
    heads = jnp.arange(LANES)[:, None]
    e = (heads == (jnp.arange(SI)[None, :] // SP)).astype(BF)

    def pad_lanes(v, off):
        return jnp.zeros((1, LANES), F32).at[0, off:off + v.shape[0]].set(v)

    full = lambda shape: pl.BlockSpec(shape, lambda b, n: (0,) * len(shape))
    return pl.pallas_call(
        functools.partial(_ssd_kernel, chunk=chunk),
        grid=(nb, nc),
        in_specs=[pl.BlockSpec((chunk, SCONV), lambda b, n: (b * nc + n, 0)),
                  pl.BlockSpec((chunk, SI), lambda b, n: (b * nc + n, z_col)),
                  pl.BlockSpec((chunk, LANES), lambda b, n: (b * nc + n, 0)),
                  full((CW, SCONV)), full((1, SCONV)), full((1, LANES)), full((1, LANES)),
                  full((1, SI)), full((1, SI)), full((LANES, SI))],
        out_specs=[pl.BlockSpec((chunk, SI), lambda b, n: (b * nc + n, 0)),
                   pl.BlockSpec((None, SH * SP, SN), lambda b, n: (b, 0, 0)),
                   pl.BlockSpec((None, CW - 1, SCONV), lambda b, n: (b, 0, 0))],
        out_shape=[jax.ShapeDtypeStruct((nb * seq, SI), BF),
                   jax.ShapeDtypeStruct((nb, SH * SP, SN), F32),
                   jax.ShapeDtypeStruct((nb, CW - 1, SCONV), F32)],
        scratch_shapes=[pltpu.VMEM((chunk + 8, SCONV), F32), pltpu.VMEM((SH * SP, SN), F32)],
        compiler_params=_cparams(("parallel", "arbitrary")),
        name="ssd_chunk_scan",
    )(xbc, zz, small, lp['ssd_conv_w'], lp['ssd_conv_b'].reshape(1, SCONV),
      pad_lanes(lp['ssd_dt_bias'], L_DT), pad_lanes(-jnp.exp(lp['ssd_a_log']), L_DT),
      jnp.repeat(lp['ssd_d'], SP).reshape(1, SI), lp['ssd_norm'].reshape(1, SI), e)


def _gdn_stage_kernel(qkv_ref, z_ref, sm_ref, cw_ref, dtb_ref, na_ref, nw_ref,
                      o_ref, st_ref, tail_ref, xbuf, s_ref, *, chunk, nb):
    n = pl.program_id(0)
    C = chunk

    @pl.when(n == 0)
    def _():
        xbuf[:, 0:8, :] = jnp.zeros((nb, 8, NCONV), F32)
        s_ref[...] = jnp.zeros(s_ref.shape, F32)

    r128 = lax.broadcasted_iota(jnp.int32, (LANES, LANES), 0)
    c128 = lax.broadcasted_iota(jnp.int32, (LANES, LANES), 1)
    tri128 = (r128 >= c128).astype(BF)
    row = lax.broadcasted_iota(jnp.int32, (C, C), 0)
    col = lax.broadcasted_iota(jnp.int32, (C, C), 1)
    eye = (row == col).astype(F32)
    ch = []
    for b in range(nb):
        xbuf[b, 8:8 + C, :] = qkv_ref[b].astype(F32)
        y = cw_ref[0:1, :] * xbuf[b, pl.ds(5, C), :]
        for t in range(1, CW):
            y = y + cw_ref[t:t + 1, :] * xbuf[b, pl.ds(5 + t, C), :]
        qkv = _silu(y)
        sm = sm_ref[b]
        beta = _sigmoid(sm)
        g = na_ref[...] * _softplus(sm + dtb_ref[...])
        gc = _dot_sel(tri128, jnp.concatenate([g, jnp.zeros((LANES - C, LANES), F32)], axis=0))
        gc_t = gc.T
        for h in range(NH):
            qh = qkv[:, h * NK:(h + 1) * NK]
            kh = qkv[:, NKW + h * NK:NKW + (h + 1) * NK]
            vh = qkv[:, 2 * NKW + h * NV:2 * NKW + (h + 1) * NV]
            qh = qh * lax.rsqrt(jnp.sum(qh * qh, axis=-1, keepdims=True) + RMS_EPS) * (NK ** -0.5)
            kh = kh * lax.rsqrt(jnp.sum(kh * kh, axis=-1, keepdims=True) + RMS_EPS)
            la = L_NA + h
            gcol = gc[0:C, la:la + 1]
            grow = gc_t[la:la + 1, 0:C]
            glast = gc[C - 1:C, la:la + 1]
            bh = beta[:, L_NB + h:L_NB + h + 1]
            kb = kh * bh
            eg = jnp.exp(gcol)
            ch.append(dict(
                b=b, h=h, glast=glast,
                dmat=jnp.exp(jnp.where(row >= col, gcol - grow, -1e30)),
                kbb=kb.astype(BF), khb=kh.astype(BF), qhb=qh.astype(BF),
                rhs=jnp.concatenate([vh * bh, kb * eg], axis=1),
                qg=(qh * eg).astype(BF), kd=(kh * jnp.exp(glast - gcol)).astype(BF)))
    for c in ch:
        c['x'] = -jnp.where(row > col, _dot_nt(c['kbb'], c['khb']) * c['dmat'], 0.0)
        c['qk'] = (_dot_nt(c['qhb'], c['khb']) * c['dmat']).astype(BF)
    for c in ch:
        c['p'] = _dot3(c['x'], c['x'])
    span = 2
    first = True
    while span < C:
        for c in ch:
            tm = (eye + c['x']) if first else c['t']
            if first:
                c['t'] = tm + _dot3(tm, c['p'])
            else:
                c['t'] = tm + _dot(tm.astype(BF), c['p'].astype(BF))
            if span * 2 < C:
                pb = c['p'].astype(BF)
                c['p'] = _dot(pb, pb)
        first = False
        span *= 2
    for c in ch:
        c['sol'] = _dot3(c['t'], c['rhs'])
    for c in ch:
        s_old = s_ref[c['b'], c['h']]
        sb = s_old.astype(BF)
        c['s_old'] = s_old
        c['vn'] = (c['sol'][:, :NV] - _dot(c['sol'][:, NV:].astype(BF), sb)).astype(BF)
        c['oq'] = _dot(c['qg'], sb)
    for c in ch:
        b, h = c['b'], c['h']
        o = c['oq'] + _dot(c['qk'], c['vn'])
        s_ref[b, h] = c['s_old'] * jnp.exp(c['glast']) + _dot_tn(c['kd'], c['vn'])
        o = o * lax.rsqrt(jnp.mean(o * o, axis=-1, keepdims=True) + RMS_EPS) * nw_ref[...]
        sl = slice(h * NV, (h + 1) * NV)
        o_ref[b, :, sl] = (o * _silu(z_ref[b, :, sl].astype(F32))).astype(BF)
    xbuf[:, 0:8, :] = xbuf[:, C:C + 8, :]

    @pl.when(n == pl.num_programs(0) - 1)
    def _():
        st_ref[...] = s_ref[...]
        tail_ref[...] = xbuf[:, pl.ds(C + 5, CW - 1), :]


def gdn_prompt_staged(qkv, zz, small, lp, nb, seq, chunk, z_col):
    nc = seq // chunk

    def pad_lanes(v, off):
        return jnp.zeros((1, LANES), F32).at[0, off:off + v.shape[0]].set(v)

    full = lambda shape: pl.BlockSpec(shape, lambda n: (0,) * len(shape))
    return pl.pallas_call(
        functools.partial(_gdn_stage_kernel, chunk=chunk, nb=nb),
        grid=(nc,),
        in_specs=[pl.BlockSpec((nb, chunk, NCONV), lambda n: (0, n, 0)),
                  pl.BlockSpec((nb, chunk, NH * NV), lambda n: (0, n, z_col)),
                  pl.BlockSpec((nb, chunk, LANES), lambda n: (0, n, 0)),
                  full((CW, NCONV)), full((1, LANES)), full((1, LANES)), full((1, NV))],
        out_specs=[pl.BlockSpec((nb, chunk, NH * NV), lambda n: (0, n, 0)),
                   full((nb, NH, NK, NV)), full((nb, CW - 1, NCONV))],
        out_shape=[jax.ShapeDtypeStruct((nb, seq, NH * NV), BF),
                   jax.ShapeDtypeStruct((nb, NH, NK, NV), F32),
                   jax.ShapeDtypeStruct((nb, CW - 1, NCONV), F32)],
        scratch_shapes=[pltpu.VMEM((nb, chunk + 8, NCONV), F32), pltpu.VMEM((nb, NH, NK, NV), F32)],
        compiler_params=_cparams(("arbitrary",)),
        name="gdn_chunk_scan",
    )(qkv, zz, small, lp['dn_conv_w'], pad_lanes(lp['dn_dt_bias'], L_NA),
      pad_lanes(-jnp.exp(lp['dn_a_log']), L_NA), lp['dn_norm'].reshape(1, NV))


def _merge_kernel(x_ref, ao_ref, so_ref, no_ref, g0_ref, g1_ref, g2_ref, wa_ref, ws_ref, wn_ref, wo_ref,
                  lg_ref, lb_ref, wr_ref, wsg_ref, wsu_ref, wsd_ref, x1_ref, r_ref, lgt_ref):
    def gate(ref):
        return _sigmoid(ref[...].astype(F32))

    mixed = (gate(g0_ref) * _dot(ao_ref[...].astype(BF), wa_ref[...])
             + gate(g1_ref) * _dot(so_ref[...].astype(BF), ws_ref[...])
             + gate(g2_ref) * _dot(no_ref[...].astype(BF), wn_ref[...]))
    yv = ALPHA * x_ref[...] + _dot(mixed.astype(BF), wo_ref[...])
    mu = jnp.mean(yv, axis=-1, keepdims=True)
    yc = yv - mu
    var = jnp.mean(yc * yc, axis=-1, keepdims=True)
    x1 = yc * lax.rsqrt(var + LN_EPS) * lg_ref[...] + lb_ref[...]
    x1b = x1.astype(BF)
    x1_ref[...] = x1
    lgt_ref[...] = _dot(x1b, wr_ref[...])
    hsh = _silu(_dot(x1b, wsg_ref[...])) * _dot(x1b, wsu_ref[...])
    r_ref[...] = ALPHA * x1 + _dot(hsh.astype(BF), wsd_ref[...])


def merge_post(x, ao, so, no, gates, gate_col, wts, tm):
    m = x.shape[0]
    tm = min(tm, m)
    rowb = lambda c: pl.BlockSpec((tm, D), lambda i, c=c: (i, c))
    full = lambda a: pl.BlockSpec(a.shape, lambda i: (0,) * a.ndim)
    return pl.pallas_call(
        _merge_kernel,
        grid=(m // tm,),
        in_specs=[rowb(0), rowb(0), rowb(0), rowb(0), rowb(gate_col), rowb(gate_col + 1), rowb(gate_col + 2)]
        + [full(a) for a in wts],
        out_specs=[rowb(0), rowb(0), pl.BlockSpec((tm, LANES), lambda i: (i, 0))],
        out_shape=[jax.ShapeDtypeStruct((m, D), F32), jax.ShapeDtypeStruct((m, D), F32),
                   jax.ShapeDtypeStruct((m, LANES), F32)],
        compiler_params=_cparams(("parallel",)),
        name="merge_ln_router_shared",
    )(x, ao, so, no, gates, gates, gates, *wts)


def _moe_kernel(be_ref, nu_ref, x_ref, wg_ref, wu_ref, wd_ref, o_ref):
    i = pl.program_id(0)

    @pl.when(i < nu_ref[0])
    def _():
        x = x_ref[...].astype(BF)
        hid = _silu(_dot(x, wg_ref[...])) * _dot(x, wu_ref[...])
        o_ref[...] = _dot(hid.astype(BF), wd_ref[...]).astype(o_ref.dtype)

    @pl.when(i >= nu_ref[0])
    def _():
        o_ref[...] = jnp.zeros(o_ref.shape, o_ref.dtype)


def moe_experts(xb, block_expert, n_used, wg, wu, wd, bm, out_dtype):
    n_rows = xb.shape[0]
    ff = wg.shape[2]
    grid_spec = pltpu.PrefetchScalarGridSpec(
        num_scalar_prefetch=2,
        grid=(n_rows // bm,),
        in_specs=[pl.BlockSpec((bm, D), lambda i, be, nu: (i, 0)),
                  pl.BlockSpec((None, D, ff), lambda i, be, nu: (be[i], 0, 0)),
                  pl.BlockSpec((None, D, ff), lambda i, be, nu: (be[i], 0, 0)),
                  pl.BlockSpec((None, ff, D), lambda i, be, nu: (be[i], 0, 0))],
        out_specs=pl.BlockSpec((bm, D), lambda i, be, nu: (i, 0)),
    )
    return pl.pallas_call(
        _moe_kernel,
        grid_spec=grid_spec,
        out_shape=jax.ShapeDtypeStruct((n_rows, D), out_dtype),
        compiler_params=_cparams(("arbitrary",)),
        name="moe_grouped_ffn",
    )(block_expert, n_used, xb, wg, wu, wd)


def _route_kernel(lg_ref, bias_ref, idx_ref, gate_ref, sel_ref):
    lg = lg_ref[...]
    shape = lg.shape
    lane = lax.broadcasted_iota(jnp.int32, shape, 1)
    lanef = lane.astype(F32)
    neg = -jnp.inf
    scores = _sigmoid(lg)
    biased = jnp.where(lane < NE, scores + bias_ref[...], neg)

    def first_max(x):
        m = jnp.max(x, axis=-1, keepdims=True)
        first = jnp.min(jnp.where(x == m, lanef, 2.0 * LANES), axis=-1, keepdims=True)
        return m, first

    gsz = NE // NGRP
    in_group = [(lane >= g * gsz) & (lane < (g + 1) * gsz) for g in range(NGRP)]
    gscore = []
    for g in range(NGRP):
        xg = jnp.where(in_group[g], biased, neg)
        m1, first = first_max(xg)
        m2 = jnp.max(jnp.where(lanef == first, neg, xg), axis=-1, keepdims=True)
        gscore.append(m1 + m2)
    allowed = jnp.zeros(shape, jnp.bool_)
    for g in range(NGRP):
        rank = jnp.zeros_like(gscore[g])
        for g2 in range(NGRP):
            if g2 != g:
                ahead = (gscore[g2] > gscore[g]) | ((gscore[g2] == gscore[g]) & (g2 < g))
                rank = rank + jnp.where(ahead, 1.0, 0.0)
        allowed = allowed | ((rank < TOPG) & in_group[g])
    work = jnp.where(allowed, biased, neg)
    idx_out = jnp.zeros(shape, F32)
    gate_out = jnp.zeros(shape, F32)
    chosen = jnp.zeros(shape, jnp.bool_)
    for j in range(TOPK):
        _, first = first_max(work)
        sel = lanef == first
        sc = jnp.sum(jnp.where(sel, scores, 0.0), axis=-1, keepdims=True)
        idx_out = jnp.where(lane == j, first, idx_out)
        gate_out = jnp.where(lane == j, sc, gate_out)
        chosen = chosen | sel
        work = jnp.where(sel, neg, work)
    gate_ref[...] = gate_out / jnp.sum(gate_out, axis=-1, keepdims=True) * RSCALE
    sel_ref[...] = jnp.where(chosen, 1, 0).astype(jnp.int32)
    idx_ref[...] = idx_out.T[0:TOPK, :].astype(jnp.int32)


def _route(logits, router_bias):
    t = logits.shape[0]
    tm = min(ROUTE_ROWS, t)
    bias = jnp.concatenate([router_bias, jnp.zeros((LANES - NE,), F32)]).reshape(1, LANES)
    rows = pl.BlockSpec((tm, LANES), lambda i: (i, 0))
    return pl.pallas_call(
        _route_kernel,
        grid=(t // tm,),
        in_specs=[rows, pl.BlockSpec((1, LANES), lambda i: (0, 0))],
        out_specs=[pl.BlockSpec((TOPK, tm), lambda i: (0, i)), rows, rows],
        out_shape=[jax.ShapeDtypeStruct((TOPK, t), jnp.int32), jax.ShapeDtypeStruct((t, LANES), F32),
                   jax.ShapeDtypeStruct((t, LANES), jnp.int32)],
        compiler_params=_cparams(("parallel",)),
        name="moe_route_topk",
    )(logits, bias)


def _dispatch(idx_t, sel, bm):
    t = sel.shape[0]
    na = t * TOPK
    assert na % bm == 0
    cum = jnp.cumsum(sel, axis=0)
    counts = cum[-1]
    padded = (counts + bm - 1) // bm * bm
    pad_end = jnp.cumsum(padded)
    dest_t = ((pad_end - padded)[None, :] + cum - sel).T
    experts = jnp.arange(NE, dtype=jnp.int32)[None, :, None]
    pos_t = jnp.sum(jnp.where(idx_t[:, None, :] == experts, dest_t[None, :NE, :], 0), axis=1).astype(jnp.int32)
    n_blocks = na // bm + NE
    starts = jnp.arange(n_blocks, dtype=jnp.int32)[:, None] * bm
    block_expert = jnp.minimum(jnp.sum((pad_end[None, :NE] <= starts).astype(jnp.int32), axis=1), NE - 1)
    n_used = (pad_end[NE - 1:NE] // bm).astype(jnp.int32)
    return block_expert, n_used, pos_t


def _scatter_rows_kernel(pos_ref, x_ref, zeros_ref, o_ref, sem, *, tm):
    del zeros_ref

    def row_copy(t, j):
        return pltpu.make_async_copy(x_ref.at[pl.ds(t, 1), :], o_ref.at[pl.ds(pos_ref[j, t], 1), :], sem)

    def issue(t, carry):
        for j in range(TOPK):
            row_copy(t, j).start()
        return carry

    def drain(t, carry):
        for j in range(TOPK):
            row_copy(t, j).wait()
        return carry

    lax.fori_loop(0, tm, issue, 0)
    lax.fori_loop(0, tm, drain, 0)


def scatter_rows(x, pos_t, n_rows):
    t = x.shape[0]
    tm = min(256, t)
    return pl.pallas_call(
        functools.partial(_scatter_rows_kernel, tm=tm),
        grid=(t // tm,),
        in_specs=[pl.BlockSpec((TOPK, tm), lambda i: (0, i), memory_space=pltpu.SMEM),
                  pl.BlockSpec((tm, D), lambda i: (i, 0)),
                  pl.BlockSpec(memory_space=pl.ANY)],
        out_specs=pl.BlockSpec(memory_space=pl.ANY),
        out_shape=jax.ShapeDtypeStruct((n_rows, D), F32),
        scratch_shapes=[pltpu.SemaphoreType.DMA],
        input_output_aliases={2: 0},
        compiler_params=_cparams(("arbitrary",)),
        name="moe_dispatch_rows",
    )(pos_t, x, jnp.zeros((n_rows, D), F32))


def _combine_kernel(yg_ref, gate_ref, r_ref, g_ref, b_ref, o_ref):
    acc = r_ref[...]
    gate = gate_ref[...]
    for j in range(TOPK):
        acc = acc + yg_ref[j].astype(F32) * gate[:, j:j + 1]
    mu = jnp.mean(acc, axis=-1, keepdims=True)
    yc = acc - mu
    var = jnp.mean(yc * yc, axis=-1, keepdims=True)
    o_ref[...] = yc * lax.rsqrt(var + LN_EPS) * g_ref[...] + b_ref[...]


def combine_ln(yg, gate, r, g, b, tm):
    t = r.shape[0]
    tm = min(tm, t)
    return pl.pallas_call(
        _combine_kernel,
        grid=(t // tm,),
        in_specs=[pl.BlockSpec((TOPK, tm, D), lambda i: (0, i, 0)), pl.BlockSpec((tm, LANES), lambda i: (i, 0)),
                  pl.BlockSpec((tm, D), lambda i: (i, 0)), pl.BlockSpec((1, D), lambda i: (0, 0)),
                  pl.BlockSpec((1, D), lambda i: (0, 0))],
        out_specs=pl.BlockSpec((tm, D), lambda i: (i, 0)),
        out_shape=jax.ShapeDtypeStruct((t, D), F32),
        compiler_params=_cparams(("parallel",)),
        name="moe_combine_ln",
    )(yg, gate, r, g.reshape(1, D), b.reshape(1, D))


def moe_finish(r, x1, logits, lp, bm):
    t = r.shape[0]
    idx_t, gate, sel = _route(logits, lp['router_bias'])
    block_expert, n_used, pos_t = _dispatch(idx_t, sel, bm)
    xb = scatter_rows(x1, pos_t, (t * TOPK // bm + NE) * bm)
    yb = moe_experts(xb, block_expert, n_used, lp['w_exp_gate'], lp['w_exp_up'], lp['w_exp_down'], bm, BF)
    yg = yb[pos_t.reshape(TOPK * t)].reshape(TOPK, t, D)
    return combine_ln(yg, gate, r, lp['ln2_g'], lp['ln2_b'], 256)


def _rms(x, g):
    return x * lax.rsqrt(jnp.mean(x * x, -1, keepdims=True) + RMS_EPS) * g


def _prep_layer(l, w_in, p):
    wi = w_in[l]
    small = jnp.concatenate([wi[:, R_DT:R_NQKV], wi[:, R_NB:R_GATE], jnp.zeros((D, LANES - 32), wi.dtype)], axis=1)
    lp = {k: v[l] for k, v in p.items()}
    lp.update(
        w_qkv=wi[:, R_Q:R_Z].astype(BF),
        w_zg=jnp.concatenate([wi[:, R_Z:R_XBC], wi[:, R_NZ:R_NB], wi[:, R_GATE:]], axis=1).astype(BF),
        w_xbc=wi[:, R_XBC:R_DT].astype(BF),
        w_nqkv=wi[:, R_NQKV:R_NZ].astype(BF),
        w_small=small.astype(BF),
        w_router_p=jnp.concatenate([lp['w_router'], jnp.zeros((D, LANES - NE), F32)], axis=1).astype(BF),
    )
    for k in ('w_attn_o', 'w_ssd_o', 'w_dn_o', 'w_out', 'w_sh_gate', 'w_sh_up', 'w_sh_down',
              'w_exp_gate', 'w_exp_up', 'w_exp_down'):
        lp[k] = lp[k].astype(BF)
    return lp


def _merge_weights(lp):
    return [lp['w_attn_o'], lp['w_ssd_o'], lp['w_dn_o'], lp['w_out'], lp['ln1_g'].reshape(1, D),
            lp['ln1_b'].reshape(1, D), lp['w_router_p'], lp['w_sh_gate'], lp['w_sh_up'], lp['w_sh_down']]


def prompt_layer(x, lp, layer_idx, tabs):
    lam_init = 0.8 - 0.6 * math.exp(-0.3 * layer_idx)
    q, kf, kb, vf, vb = qkv_project(x, lp['w_qkv'], tabs, 512)
    zg = matmul(x, lp['w_zg'], BF, tm=PROJ_ROWS)
    xbc = matmul(x, lp['w_xbc'], BF, tm=PROJ_ROWS)
    nqkv = matmul(x, lp['w_nqkv'], BF, tm=PROJ_ROWS)
    small = matmul(x, lp['w_small'], F32, tm=PROJ_ROWS)
    ao = flash_diff_attention(q, kb, vb, lp['attn_lambda'], lp['attn_subln'], lam_init, NB, SEQ, ATT_BLK)
    so, ssd_h, ssd_tail = ssd_prompt(xbc, zg, small, lp, NB, SEQ, SSD_CHUNK, 0)
    no, dn_s, dn_tail = gdn_prompt_staged(nqkv.reshape(NB, SEQ, NCONV), zg.reshape(NB, SEQ, -1),
                                          small.reshape(NB, SEQ, LANES), lp, NB, SEQ, GDN_CHUNK, 1)
    no = no.reshape(NB * SEQ, NH * NV)
    x1, r, logits = merge_post(x, ao, so, no, zg, 2, _merge_weights(lp), 512)
    x2 = moe_finish(r, x1, logits, lp, MOE_BLK_PROMPT)
    return (x2, kf.reshape(NB, SEQ, KVH, 2 * HD), vf.reshape(NB, SEQ, KVH, 2 * HD),
            ssd_h.reshape(NB, SH, SP, SN), ssd_tail, dn_s, dn_tail)


def _conv_step(hist, new, w, b=None):
    xp = jnp.concatenate([hist, new[:, None, :]], axis=1)
    y = jnp.sum(xp * w[None], axis=1)
    if b is not None:
        y = y + b
    return jax.nn.silu(y), xp[:, 1:]


def decode_layer(x, lp, layer_idx, tabs, cache_k, cache_v, page_table, ssd_h0, ssd_hist, dn_s0, dn_hist):
    lam_init = 0.8 - 0.6 * math.exp(-0.3 * layer_idx)
    q, kf, _, vf, _ = qkv_project(x, lp['w_qkv'], tabs, DB)
    zg = matmul(x, lp['w_zg'], F32, tm=DB)
    xbc_raw = matmul(x, lp['w_xbc'], F32, tm=DB)
    nqkv_raw = matmul(x, lp['w_nqkv'], F32, tm=DB)
    small = matmul(x, lp['w_small'], F32, tm=DB)
    qh = q.reshape(DB, KVH, 2, 2, HD)
    q16 = jnp.einsum('bhgcd,ce->bhgced', qh, jnp.eye(2, dtype=BF)).reshape(DB, 16, 2 * HD)
    k16 = jnp.repeat(kf.reshape(DB, KVH, 2 * HD), 4, axis=1)
    v16 = jnp.repeat(vf.reshape(DB, KVH, 2 * HD), 4, axis=1)
    oc = paged_attention(q16, k16, v16, cache_k, cache_v, page_table, layer_idx, PAGES_PER_STEP)
    oc = oc.reshape(DB, AH, 2, 2 * HD)
    lq = lp['attn_lambda']
    lam = jnp.exp(jnp.sum(lq[0] * lq[1])) - jnp.exp(jnp.sum(lq[2] * lq[3])) + lam_init
    o = oc[:, :, 0] - lam * oc[:, :, 1]
    ao = (_rms(o, lp['attn_subln']) * (1.0 - lam_init)).reshape(DB, AH * 2 * HD)
    xbc, ssd_tail = _conv_step(ssd_hist, xbc_raw, lp['ssd_conv_w'], lp['ssd_conv_b'])
    xs = xbc[:, :SI].reshape(DB, SH, SP)
    bm = jnp.repeat(xbc[:, SI:SI + SG * SN].reshape(DB, SG, SN), SH // SG, axis=1)
    cm = jnp.repeat(xbc[:, SI + SG * SN:].reshape(DB, SG, SN), SH // SG, axis=1)
    dt = jax.nn.softplus(small[:, L_DT:L_DT + SH] + lp['ssd_dt_bias'])
    a_coef = -jnp.exp(lp['ssd_a_log'])
    ssd_h = ssd_h0 * jnp.exp(dt * a_coef)[..., None, None] + (xs * dt[..., None])[..., :, None] * bm[..., None, :]
    ys = jnp.sum(ssd_h * cm[..., None, :], -1) + lp['ssd_d'][:, None] * xs
    gw = SI // SG
    ys = ys.reshape(DB, SG, gw) * jax.nn.silu(zg[:, :SI]).reshape(DB, SG, gw)
    so = _rms(ys, lp['ssd_norm'].reshape(SG, gw)).reshape(DB, SI)
    qkv, dn_tail = _conv_step(dn_hist, nqkv_raw, lp['dn_conv_w'])
    nq = qkv[:, :NKW].reshape(DB, NH, NK)
    nk = qkv[:, NKW:2 * NKW].reshape(DB, NH, NK)
    nv = qkv[:, 2 * NKW:].reshape(DB, NH, NV)
    nq = nq * lax.rsqrt(jnp.sum(nq * nq, -1, keepdims=True) + RMS_EPS) * (NK ** -0.5)
    nk = nk * lax.rsqrt(jnp.sum(nk * nk, -1, keepdims=True) + RMS_EPS)
    beta = jax.nn.sigmoid(small[:, L_NB:L_NB + NH])
    gl = -jnp.exp(lp['dn_a_log']) * jax.nn.softplus(small[:, L_NA:L_NA + NH] + lp['dn_dt_bias'])
    s = dn_s0 * jnp.exp(gl)[..., None, None]
    delta = (nv - jnp.sum(s * nk[..., :, None], -2)) * beta[..., None]
    dn_s = s + nk[..., :, None] * delta[..., None, :]
    on = jnp.sum(dn_s * nq[..., :, None], -2)
    no = (_rms(on, lp['dn_norm']) * jax.nn.silu(zg[:, SI:2 * SI].reshape(DB, NH, NV))).reshape(DB, NH * NV)
    x1, r, logits = merge_post(x, ao, so, no, zg, 2, _merge_weights(lp), DB)
    x2 = moe_finish(r, x1, logits, lp, MOE_BLK_DECODE)
    return (x2, kf.reshape(DB, 1, KVH, 2 * HD), vf.reshape(DB, 1, KVH, 2 * HD), ssd_h, ssd_tail, dn_s, dn_tail)


def kernel(x_prompt, x_sample, cache_k, cache_v, state_ssd, state_ssd_conv, state_dn, state_dn_conv, page_table,
           w_in, attn_lambda, attn_subln, w_attn_o, ssd_conv_w, ssd_conv_b, ssd_dt_bias, ssd_a_log, ssd_d, ssd_norm,
           w_ssd_o, dn_conv_w, dn_dt_bias, dn_a_log, dn_norm, w_dn_o, w_out, ln1_g, ln1_b, w_router, router_bias,
           w_exp_gate, w_exp_up, w_exp_down, w_sh_gate, w_sh_up, w_sh_down, ln2_g, ln2_b):
    params = dict(attn_lambda=attn_lambda, attn_subln=attn_subln, w_attn_o=w_attn_o, ssd_conv_w=ssd_conv_w,
                  ssd_conv_b=ssd_conv_b, ssd_dt_bias=ssd_dt_bias, ssd_a_log=ssd_a_log, ssd_d=ssd_d,
                  ssd_norm=ssd_norm, w_ssd_o=w_ssd_o, dn_conv_w=dn_conv_w, dn_dt_bias=dn_dt_bias,
                  dn_a_log=dn_a_log, dn_norm=dn_norm, w_dn_o=w_dn_o, w_out=w_out, ln1_g=ln1_g, ln1_b=ln1_b,
                  w_router=w_router, router_bias=router_bias, w_exp_gate=w_exp_gate, w_exp_up=w_exp_up,
                  w_exp_down=w_exp_down, w_sh_gate=w_sh_gate, w_sh_up=w_sh_up, w_sh_down=w_sh_down,
                  ln2_g=ln2_g, ln2_b=ln2_b)
    tabs_p = _rope_tables(jnp.arange(SEQ, dtype=jnp.int32))
    tabs_d = _rope_tables(jnp.full((DB,), PAST, dtype=jnp.int32))
    xp = x_prompt.reshape(NB * SEQ, D)
    xs = x_sample.reshape(DB, D)
    outs = [[] for _ in range(12)]
    for l in range(DEPTH):
        lp = _prep_layer(l, w_in, params)
        xp, kp, vp, hp, cp, sp, dp = prompt_layer(xp, lp, l, tabs_p)
        xs, ks, vs, hs, cs, ss, ds = decode_layer(xs, lp, l, tabs_d, cache_k, cache_v, page_table,
                                                  state_ssd[l], state_ssd_conv[l], state_dn[l], state_dn_conv[l])
        for lst, val in zip(outs, (kp, vp, ks, vs, hp, cp, hs, cs, sp, dp, ss, ds)):
            lst.append(val)
    return (xp.reshape(NB, SEQ, D), xs.reshape(DB, 1, D)) + tuple(jnp.stack(o) for o in outs)
```

```python
import functools
import math

import jax
import jax.numpy as jnp
from jax import lax
from jax.experimental import pallas as pl
from jax.experimental.pallas import tpu as pltpu

D = 1024
NB, SEQ = 4, 4096
DEPTH = 2
DB = 32
PAST = 16384
PAGE = 128
AH, KVH, HD = 8, 4, 64
ROT = 16
THETA = 500000.0
SH, SP, SI, SG, SN = 16, 64, 1024, 2, 128
SCONV = SI + 2 * SG * SN
NH, NK, NV = 8, 128, 128
NKW = 1024
NCONV = 3072
CW = 4
NE, TOPK, NGRP, TOPG = 64, 8, 8, 4
RSCALE = 2.5
ALPHA = (2 * DEPTH) ** 0.25
LN_EPS = 1e-5
RMS_EPS = 1e-6

R_Q, R_K, R_V, R_Z, R_XBC, R_DT, R_NQKV, R_NZ, R_NB, R_NA, R_GATE = (
    0, 1024, 1536, 2048, 3072, 4608, 4624, 7696, 8720, 8728, 8736)
L_DT, L_NB, L_NA = 0, 16, 24

LANES = 128
VMEM_LIMIT = 56 * 1024 * 1024
SSD_CHUNK = 128
GDN_CHUNK = 64
ATT_BLK = 512
FLASH_ROWS = 32
PAGES_PER_STEP = 16
ROUTE_ROWS = 1024
PROJ_ROWS = 2048
MOE_BLK_PROMPT = 512
MOE_BLK_DECODE = 16

BF = jnp.bfloat16
F32 = jnp.float32
PACK_HI = -65536


def _cparams(sem):
    return pltpu.CompilerParams(dimension_semantics=sem, vmem_limit_bytes=VMEM_LIMIT)


def _dot(a, b):
    return jnp.dot(a, b, preferred_element_type=F32)


def _dot_nt(a, b):
    return lax.dot_general(a, b, (((1,), (1,)), ((), ())), preferred_element_type=F32)


def _dot_tn(a, b):
    return lax.dot_general(a, b, (((0,), (0,)), ((), ())), preferred_element_type=F32)


def _split2(x):
    hi = x.astype(BF)
    lo = (x - hi.astype(F32)).astype(BF)
    return hi, lo


def _split3(x):
    hi = x.astype(BF)
    r = x - hi.astype(F32)
    mid = r.astype(BF)
    lo = (r - mid.astype(F32)).astype(BF)
    return hi, mid, lo


def _dot_sel(sel, x):
    hi, mid, lo = _split3(x)
    return _dot(sel, hi) + _dot(sel, mid) + _dot(sel, lo)


def _dot_x_sel(x, sel):
    hi, mid, lo = _split3(x)
    return _dot(hi, sel) + _dot(mid, sel) + _dot(lo, sel)


def _dot3(a, b):
    ah, al = _split2(a)
    bh, bl = _split2(b)
    return _dot(ah, bh) + _dot(ah, bl) + _dot(al, bh)


def _sigmoid(x):
    return 1.0 / (1.0 + jnp.exp(-x))


def _silu(x):
    return x * _sigmoid(x)


def _softplus(x):
    return jnp.maximum(x, 0.0) + jnp.log(1.0 + jnp.exp(-jnp.abs(x)))


def _mm_kernel(x_ref, w_ref, o_ref, xs_ref):
    @pl.when(pl.program_id(1) == 0)
    def _():
        xs_ref[...] = x_ref[...].astype(BF)

    o_ref[...] = _dot(xs_ref[...], w_ref[...]).astype(o_ref.dtype)


def matmul(x, w, out_dtype, tm=512, tn=512):
    m, k = x.shape
    n = w.shape[1]
    tm, tn = min(tm, m), min(tn, n)
    assert m % tm == 0 and n % tn == 0, (m, n, tm, tn)
    return pl.pallas_call(
        _mm_kernel,
        grid=(m // tm, n // tn),
        in_specs=[pl.BlockSpec((tm, k), lambda i, j: (i, 0)),
                  pl.BlockSpec((k, tn), lambda i, j: (0, j))],
        out_specs=pl.BlockSpec((tm, tn), lambda i, j: (i, j)),
        out_shape=jax.ShapeDtypeStruct((m, n), out_dtype),
        scratch_shapes=[pltpu.VMEM((tm, k), BF)],
        compiler_params=_cparams(("parallel", "arbitrary")),
        name="proj_matmul",
    )(x, w)


def _rope_tables(pos):
    half = ROT // 2
    inv = THETA ** (-jnp.arange(half, dtype=F32) * 2.0 / ROT)
    ang = pos.astype(F32)[:, None] * inv
    cos, sin = jnp.cos(ang), jnp.sin(ang)
    t = pos.shape[0]
    one, zero, z8 = jnp.ones((t, HD - ROT), F32), jnp.zeros((t, HD - ROT), F32), jnp.zeros((t, half), F32)
    c = jnp.concatenate([cos, cos, one, cos, cos, one], 1)
    sa = jnp.concatenate([-sin, z8, zero, -sin, z8, zero], 1)
    sb = jnp.concatenate([z8, sin, zero, z8, sin, zero], 1)
    return c, sa, sb


def _qkv_kernel(x_ref, w_ref, c_ref, sa_ref, sb_ref, q_ref, kf_ref, kb_ref, vf_ref, vb_ref):
    acc = _dot(x_ref[...].astype(BF), w_ref[...])
    c, sa, sb = c_ref[...], sa_ref[...], sb_ref[...]

    def rot(xg):
        return xg * c + pltpu.roll(xg, LANES - ROT // 2, 1) * sa + pltpu.roll(xg, ROT // 2, 1) * sb

    for h in range(AH):
        sl = slice(h * LANES, (h + 1) * LANES)
        q_ref[:, sl] = (rot(acc[:, sl]) * (HD ** -0.5)).astype(BF)
    for h in range(KVH):
        sl = slice(h * LANES, (h + 1) * LANES)
        kr = rot(acc[:, R_K + h * LANES:R_K + (h + 1) * LANES])
        kf_ref[:, sl] = kr
        kb_ref[:, sl] = kr.astype(BF)
    v = acc[:, R_V:R_Z]
    vf_ref[...] = v
    vb_ref[...] = v.astype(BF)


def qkv_project(x, w, tabs, tm):
    m = x.shape[0]
    tm = min(tm, m)
    nt = tabs[0].shape[0] // tm
    kw = KVH * 2 * HD
    tab_spec = pl.BlockSpec((tm, LANES), lambda i: (i % nt, 0))
    return pl.pallas_call(
        _qkv_kernel,
        grid=(m // tm,),
        in_specs=[pl.BlockSpec((tm, D), lambda i: (i, 0)),
                  pl.BlockSpec((D, R_Z), lambda i: (0, 0)),
                  tab_spec, tab_spec, tab_spec],
        out_specs=[pl.BlockSpec((tm, AH * 2 * HD), lambda i: (i, 0)),
                   pl.BlockSpec((tm, kw), lambda i: (i, 0)), pl.BlockSpec((tm, kw), lambda i: (i, 0)),
                   pl.BlockSpec((tm, kw), lambda i: (i, 0)), pl.BlockSpec((tm, kw), lambda i: (i, 0))],
        out_shape=[jax.ShapeDtypeStruct((m, AH * 2 * HD), BF),
                   jax.ShapeDtypeStruct((m, kw), F32), jax.ShapeDtypeStruct((m, kw), BF),
                   jax.ShapeDtypeStruct((m, kw), F32), jax.ShapeDtypeStruct((m, kw), BF)],
        compiler_params=_cparams(("parallel",)),
        name="qkv_rope",
    )(x, w, *tabs)


def _lambda_value(lq, lam_init):
    a = jnp.sum(lq[0:1, :] * lq[1:2, :], axis=-1, keepdims=True)
    b = jnp.sum(lq[2:3, :] * lq[3:4, :], axis=-1, keepdims=True)
    return jnp.exp(a) - jnp.exp(b) + lam_init


def _flash_kernel(qi_ref, ki_ref, lq_ref, sub_ref, q_ref, k_ref, v_ref, o_ref, qs_ref, m_ref, acc_ref, s_ref, p_ref,
                  *, blk, lam_init):
    qi, ki = qi_ref[pl.program_id(2)], ki_ref[pl.program_id(2)]

    @pl.when(ki == 0)
    def _():
        q = q_ref[...]
        lane = lax.broadcasted_iota(jnp.int32, (blk, LANES), 1)
        for c in range(2):
            msk = (lane < HD) if c == 0 else (lane >= HD)
            for g in range(2):
                qs_ref[c, g * blk:(g + 1) * blk, :] = jnp.where(msk, q[:, g * LANES:(g + 1) * LANES], 0)
        m_ref[...] = jnp.full(m_ref.shape, -jnp.inf, F32)
        acc_ref[...] = jnp.zeros(acc_ref.shape, F32)

    def update(masked):
        k = k_ref[...]
        vext = jnp.concatenate([v_ref[...], jnp.ones((blk, LANES), BF)], axis=1)
        units = [(c, g) for g in range(2) for c in range(2)]
        for u, (c, g) in enumerate(units):
            s_ref[u] = _dot_nt(qs_ref[c, g * blk:(g + 1) * blk, :], k)
        if masked:
            keep = (lax.broadcasted_iota(jnp.int32, (blk, blk), 1) <= lax.broadcasted_iota(jnp.int32, (blk, blk), 0))
            diff = (lax.broadcasted_iota(jnp.int32, (FLASH_ROWS, blk), 1)
                    - lax.broadcasted_iota(jnp.int32, (FLASH_ROWS, blk), 0))
        for u, (c, g) in enumerate(units):
            rows = slice(g * blk, (g + 1) * blk)
            s_full = s_ref[u]
            if masked:
                s_full = jnp.where(keep, s_full, -jnp.inf)
            m_old = m_ref[c, rows]
            m_new = jnp.maximum(m_old, jnp.max(s_full, axis=-1, keepdims=True))
            alpha = jnp.exp(m_old - m_new)
            for r in range(0, blk, FLASH_ROWS):
                s = s_ref[u, r:r + FLASH_ROWS, :]
                if masked:
                    s = jnp.where(diff <= r, s, -jnp.inf)
                p_ref[u, r:r + FLASH_ROWS, :] = jnp.exp(s - m_new[r:r + FLASH_ROWS]).astype(BF)
            m_ref[c, rows] = m_new
            acc_ref[c, rows] = alpha * acc_ref[c, rows] + _dot(p_ref[u], vext)

    @pl.when(ki < qi)
    def _():
        update(False)

    @pl.when(ki == qi)
    def _():
        update(True)
        lam = _lambda_value(lq_ref[...], lam_init)
        a0, a1 = acc_ref[0], acc_ref[1]
        o = a0[:, :LANES] / a0[:, LANES:] - lam * (a1[:, :LANES] / a1[:, LANES:])
        o = o * lax.rsqrt(jnp.mean(o * o, axis=-1, keepdims=True) + RMS_EPS) * sub_ref[...] * (1.0 - lam_init)
        o_ref[:, 0:LANES] = o[0:blk].astype(BF)
        o_ref[:, LANES:2 * LANES] = o[blk:2 * blk].astype(BF)


def flash_diff_attention(q, k, v, lq, subln, lam_init, nb, seq, blk):
    nblk = seq // blk
    pairs = [(i, j) for i in range(nblk) for j in range(i + 1)]
    qi_arr = jnp.asarray([p[0] for p in pairs], jnp.int32)
    ki_arr = jnp.asarray([p[1] for p in pairs], jnp.int32)
    kern = functools.partial(_flash_kernel, blk=blk, lam_init=lam_init)
    grid_spec = pltpu.PrefetchScalarGridSpec(
        num_scalar_prefetch=2,
        grid=(nb, KVH, len(pairs)),
        in_specs=[pl.BlockSpec((4, HD), lambda b, h, s, qa, ka: (0, 0)),
                  pl.BlockSpec((1, LANES), lambda b, h, s, qa, ka: (0, 0)),
                  pl.BlockSpec((blk, 2 * LANES), lambda b, h, s, qa, ka: (b * nblk + qa[s], h)),
                  pl.BlockSpec((blk, LANES), lambda b, h, s, qa, ka: (b * nblk + ka[s], h)),
                  pl.BlockSpec((blk, LANES), lambda b, h, s, qa, ka: (b * nblk + ka[s], h))],
        out_specs=pl.BlockSpec((blk, 2 * LANES), lambda b, h, s, qa, ka: (b * nblk + qa[s], h)),
        scratch_shapes=[pltpu.VMEM((2, 2 * blk, LANES), BF), pltpu.VMEM((2, 2 * blk, 1), F32),
                        pltpu.VMEM((2, 2 * blk, 2 * LANES), F32),
                        pltpu.VMEM((4, blk, blk), F32), pltpu.VMEM((4, blk, blk), BF)],
    )
    return pl.pallas_call(
        kern,
        grid_spec=grid_spec,
        out_shape=jax.ShapeDtypeStruct((nb * seq, AH * 2 * HD), BF),
        compiler_params=_cparams(("parallel", "parallel", "arbitrary")),
        name="flash_diff_attn",
    )(qi_arr, ki_arr, lq, subln.reshape(1, LANES), q, k, v)


def _paged_kernel(pt_ref, q_ref, ks_ref, vs_ref, *refs, pps):
    kp, vp = refs[:pps], refs[pps:2 * pps]
    o_ref, m_ref, l_ref, acc_ref = refs[2 * pps:]
    j = pl.program_id(1)
    q = q_ref[...]
    pw = PAGE * KVH

    @pl.when(j == 0)
    def _():
        ks = ks_ref[...].astype(BF).astype(F32)
        m_ref[...] = jnp.sum(q.astype(F32) * ks, axis=-1, keepdims=True)
        l_ref[...] = jnp.ones(l_ref.shape, F32)
        acc_ref[...] = vs_ref[...].astype(BF).astype(F32)

    row_head = lax.broadcasted_iota(jnp.int32, (16, pw), 0) // 4
    col_head = lax.broadcasted_iota(jnp.int32, (16, pw), 1) % KVH
    own = row_head == col_head
    s = jnp.concatenate([jnp.where(own, _dot_nt(q, kp[i][...].astype(BF)), -jnp.inf) for i in range(pps)], axis=1)
    m_old = m_ref[...]
    m_new = jnp.maximum(m_old, jnp.max(s, axis=-1, keepdims=True))
    alpha = jnp.exp(m_old - m_new)
    p = jnp.exp(s - m_new)
    l_ref[...] = alpha * l_ref[...] + jnp.sum(p, axis=-1, keepdims=True)
    pv = _dot(p[:, 0:pw].astype(BF), vp[0][...].astype(BF))
    for i in range(1, pps):
        pv = pv + _dot(p[:, i * pw:(i + 1) * pw].astype(BF), vp[i][...].astype(BF))
    acc_ref[...] = alpha * acc_ref[...] + pv
    m_ref[...] = m_new

    @pl.when(j == pl.num_programs(1) - 1)
    def _():
        o_ref[...] = acc_ref[...] / l_ref[...]


def paged_attention(q16, k16, v16, cache_k, cache_v, page_table, layer, pps):
    nseq, npg = page_table.shape
    pw = PAGE * KVH
    ck = cache_k.reshape(cache_k.shape[0], cache_k.shape[1], pw, 2 * HD)
    cv = cache_v.reshape(cache_v.shape[0], cache_v.shape[1], pw, 2 * HD)

    def page_spec(i):
        return pl.BlockSpec((None, None, pw, 2 * HD), lambda b, j, pt: (layer, pt[b, j * pps + i], 0, 0))

    row = pl.BlockSpec((None, 16, 2 * HD), lambda b, j, pt: (b, 0, 0))
    grid_spec = pltpu.PrefetchScalarGridSpec(
        num_scalar_prefetch=1,
        grid=(nseq, npg // pps),
        in_specs=[row, row, row] + [page_spec(i) for i in range(pps)] + [page_spec(i) for i in range(pps)],
        out_specs=row,
        scratch_shapes=[pltpu.VMEM((16, 1), F32), pltpu.VMEM((16, 1), F32), pltpu.VMEM((16, 2 * HD), F32)],
    )
    return pl.pallas_call(
        functools.partial(_paged_kernel, pps=pps),
        grid_spec=grid_spec,
        out_shape=jax.ShapeDtypeStruct((nseq, 16, 2 * HD), F32),
        compiler_params=_cparams(("parallel", "arbitrary")),
        name="paged_diff_attn",
    )(page_table, q16, k16, v16, *([ck] * pps), *([cv] * pps))


def _ssd_kernel(xbc_ref, z_ref, sm_ref, cw_ref, cb_ref, dtb_ref, ac_ref, dx_ref, nw_ref, e_ref,
                y_ref, st_ref, tail_ref, xbuf, s_ref, *, chunk):
    n = pl.program_id(1)
    L = chunk

    @pl.when(n == 0)
    def _():
        xbuf[0:8, :] = jnp.zeros((8, SCONV), F32)
        s_ref[...] = jnp.zeros(s_ref.shape, F32)

    xbuf[8:8 + L, :] = xbc_ref[...].astype(F32)
    y = cb_ref[...] + cw_ref[0:1, :] * xbuf[pl.ds(5, L), :]
    for t in range(1, CW):
        y = y + cw_ref[t:t + 1, :] * xbuf[pl.ds(5 + t, L), :]
    xbc = _silu(y)
    xs, bm, cm = xbc[:, :SI], xbc[:, SI:SI + SG * SN], xbc[:, SI + SG * SN:]

    dt = _softplus(sm_ref[...] + dtb_ref[...])
    a = dt * ac_ref[...]
    row = lax.broadcasted_iota(jnp.int32, (L, L), 0)
    col = lax.broadcasted_iota(jnp.int32, (L, L), 1)
    causal = row >= col
    acum = _dot_sel(causal.astype(BF), a)
    acum_t = acum.T
    alast = acum[L - 1:L, :]
    e = e_ref[...]
    dt_e = _dot_x_sel(dt, e)
    ea_e = _dot_x_sel(jnp.exp(acum), e)
    dend_e = _dot_x_sel(jnp.exp(alast - acum), e)
    xdt = xs * dt_e
    xd = (xdt * dend_e).astype(BF)
    lane = lax.broadcasted_iota(jnp.int32, (L, LANES), 1)
    gw = SI // SG
    hpg = SH // SG
    y_parts = []
    for g in range(SG):
        cg = cm[:, g * SN:(g + 1) * SN].astype(BF)
        bg = bm[:, g * SN:(g + 1) * SN].astype(BF)
        s_old = s_ref[g * gw:(g + 1) * gw, :]
        y_inter = _dot_nt(cg, s_old.astype(BF))
        cb = _dot_nt(cg, bg)
        pairs = []
        for jp in range(hpg // 2):
            h0 = g * hpg + 2 * jp
            xp = xdt[:, h0 * SP:h0 * SP + LANES]
            acc = None
            for d in range(2):
                h = h0 + d
                seg = jnp.where(causal, jnp.exp(acum[:, h:h + 1] - acum_t[h:h + 1, :]), 0.0)
                mh = (cb * seg).astype(BF)
                xm = jnp.where((lane < SP) if d == 0 else (lane >= SP), xp, 0.0).astype(BF)
                t = _dot(mh, xm)
                acc = t if acc is None else acc + t
            pairs.append(acc)
        y_intra = jnp.concatenate(pairs, axis=1)
        y_parts.append(y_intra + y_inter * ea_e[:, g * gw:(g + 1) * gw])
        upd = _dot_tn(xd[:, g * gw:(g + 1) * gw], bg)
        for hh in range(hpg):
            h = g * hpg + hh
            r0 = h * SP
            s_ref[r0:r0 + SP, :] = (s_old[hh * SP:(hh + 1) * SP, :] * jnp.exp(alast[:, h:h + 1])
                                    + upd[hh * SP:(hh + 1) * SP, :])
    yv = jnp.concatenate(y_parts, axis=1) + dx_ref[...] * xs
    yv = yv * _silu(z_ref[...].astype(F32))
    outs = []
    for g in range(SG):
        yg = yv[:, g * gw:(g + 1) * gw]
        outs.append(yg * lax.rsqrt(jnp.mean(yg * yg, axis=-1, keepdims=True) + RMS_EPS) * nw_ref[:, g * gw:(g + 1) * gw])
    y_ref[...] = jnp.concatenate(outs, axis=1).astype(BF)
    xbuf[0:8, :] = xbuf[L:L + 8, :]

    @pl.when(n == pl.num_programs(1) - 1)
    def _():
        st_ref[...] = s_ref[...]
        tail_ref[...] = xbuf[pl.ds(L + 5, CW - 1), :]


def ssd_prompt(xbc, zz, small, lp, nb, seq, chunk, z_col):
    nc = seq // chunk
    heads = jnp.arange(LANES)[:, None]
    e = (heads == (jnp.arange(SI)[None, :] // SP)).astype(BF)

    def pad_lanes(v, off):
        return jnp.zeros((1, LANES), F32).at[0, off:off + v.shape[0]].set(v)

    full = lambda shape: pl.BlockSpec(shape, lambda b, n: (0,) * len(shape))
    return pl.pallas_call(
        functools.partial(_ssd_kernel, chunk=chunk),
        grid=(nb, nc),
        in_specs=[pl.BlockSpec((chunk, SCONV), lambda b, n: (b * nc + n, 0)),
                  pl.BlockSpec((chunk, SI), lambda b, n: (b * nc + n, z_col)),
                  pl.BlockSpec((chunk, LANES), lambda b, n: (b * nc + n, 0)),
                  full((CW, SCONV)), full((1, SCONV)), full((1, LANES)), full((1, LANES)),
                  full((1, SI)), full((1, SI)), full((LANES, SI))],
        out_specs=[pl.BlockSpec((chunk, SI), lambda b, n: (b * nc + n, 0)),
                   pl.BlockSpec((None, SH * SP, SN), lambda b, n: (b, 0, 0)),
                   pl.BlockSpec((None, CW - 1, SCONV), lambda b, n: (b, 0, 0))],
        out_shape=[jax.ShapeDtypeStruct((nb * seq, SI), BF),
                   jax.ShapeDtypeStruct((nb, SH * SP, SN), F32),
                   jax.ShapeDtypeStruct((nb, CW - 1, SCONV), F32)],
        scratch_shapes=[pltpu.VMEM((chunk + 8, SCONV), F32), pltpu.VMEM((SH * SP, SN), F32)],
        compiler_params=_cparams(("parallel", "arbitrary")),
        name="ssd_chunk_scan",
    )(xbc, zz, small, lp['ssd_conv_w'], lp['ssd_conv_b'].reshape(1, SCONV),
      pad_lanes(lp['ssd_dt_bias'], L_DT), pad_lanes(-jnp.exp(lp['ssd_a_log']), L_DT),
      jnp.repeat(lp['ssd_d'], SP).reshape(1, SI), lp['ssd_norm'].reshape(1, SI), e)


def _gdn_stage_kernel(qkv_ref, z_ref, sm_ref, cw_ref, dtb_ref, na_ref, nw_ref,
                      o_ref, st_ref, tail_ref, xbuf, s_ref, *, chunk, nb):
    n = pl.program_id(0)
    C = chunk

    @pl.when(n == 0)
    def _():
        xbuf[:, 0:8, :] = jnp.zeros((nb, 8, NCONV), F32)
        s_ref[...] = jnp.zeros(s_ref.shape, F32)

    r128 = lax.broadcasted_iota(jnp.int32, (LANES, LANES), 0)
    c128 = lax.broadcasted_iota(jnp.int32, (LANES, LANES), 1)
    tri128 = (r128 >= c128).astype(BF)
    row = lax.broadcasted_iota(jnp.int32, (C, C), 0)
    col = lax.broadcasted_iota(jnp.int32, (C, C), 1)
    eye = (row == col).astype(F32)
    ch = []
    for b in range(nb):
        xbuf[b, 8:8 + C, :] = qkv_ref[b].astype(F32)
        y = cw_ref[0:1, :] * xbuf[b, pl.ds(5, C), :]
        for t in range(1, CW):
            y = y + cw_ref[t:t + 1, :] * xbuf[b, pl.ds(5 + t, C), :]
        qkv = _silu(y)
        sm = sm_ref[b]
        beta = _sigmoid(sm)
        g = na_ref[...] * _softplus(sm + dtb_ref[...])
        gc = _dot_sel(tri128, jnp.concatenate([g, jnp.zeros((LANES - C, LANES), F32)], axis=0))
        gc_t = gc.T
        for h in range(NH):
            qh = qkv[:, h * NK:(h + 1) * NK]
            kh = qkv[:, NKW + h * NK:NKW + (h + 1) * NK]
            vh = qkv[:, 2 * NKW + h * NV:2 * NKW + (h + 1) * NV]
            qh = qh * lax.rsqrt(jnp.sum(qh * qh, axis=-1, keepdims=True) + RMS_EPS) * (NK ** -0.5)
            kh = kh * lax.rsqrt(jnp.sum(kh * kh, axis=-1, keepdims=True) + RMS_EPS)
            la = L_NA + h
            gcol = gc[0:C, la:la + 1]
            grow = gc_t[la:la + 1, 0:C]
            glast = gc[C - 1:C, la:la + 1]
            bh = beta[:, L_NB + h:L_NB + h + 1]
            kb = kh * bh
            eg = jnp.exp(gcol)
            ch.append(dict(
                b=b, h=h, glast=glast,
                dmat=jnp.exp(jnp.where(row >= col, gcol - grow, -1e30)),
                kbb=kb.astype(BF), khb=kh.astype(BF), qhb=qh.astype(BF),
                rhs=jnp.concatenate([vh * bh, kb * eg], axis=1),
                qg=(qh * eg).astype(BF), kd=(kh * jnp.exp(glast - gcol)).astype(BF)))
    for c in ch:
        c['x'] = -jnp.where(row > col, _dot_nt(c['kbb'], c['khb']) * c['dmat'], 0.0)
        c['qk'] = (_dot_nt(c['qhb'], c['khb']) * c['dmat']).astype(BF)
    for c in ch:
        c['p'] = _dot3(c['x'], c['x'])
    span = 2
    first = True
    while span < C:
        for c in ch:
            tm = (eye + c['x']) if first else c['t']
            if first:
                c['t'] = tm + _dot3(tm, c['p'])
            else:
                c['t'] = tm + _dot(tm.astype(BF), c['p'].astype(BF))
            if span * 2 < C:
                pb = c['p'].astype(BF)
                c['p'] = _dot(pb, pb)
        first = False
        span *= 2
    for c in ch:
        c['sol'] = _dot3(c['t'], c['rhs'])
    for c in ch:
        s_old = s_ref[c['b'], c['h']]
        sb = s_old.astype(BF)
        c['s_old'] = s_old
        c['vn'] = (c['sol'][:, :NV] - _dot(c['sol'][:, NV:].astype(BF), sb)).astype(BF)
        c['oq'] = _dot(c['qg'], sb)
    for c in ch:
        b, h = c['b'], c['h']
        o = c['oq'] + _dot(c['qk'], c['vn'])
        s_ref[b, h] = c['s_old'] * jnp.exp(c['glast']) + _dot_tn(c['kd'], c['vn'])
        o = o * lax.rsqrt(jnp.mean(o * o, axis=-1, keepdims=True) + RMS_EPS) * nw_ref[...]
        sl = slice(h * NV, (h + 1) * NV)
        o_ref[b, :, sl] = (o * _silu(z_ref[b, :, sl].astype(F32))).astype(BF)
    xbuf[:, 0:8, :] = xbuf[:, C:C + 8, :]

    @pl.when(n == pl.num_programs(0) - 1)
    def _():
        st_ref[...] = s_ref[...]
        tail_ref[...] = xbuf[:, pl.ds(C + 5, CW - 1), :]


def gdn_prompt_staged(qkv, zz, small, lp, nb, seq, chunk, z_col):
    nc = seq // chunk

    def pad_lanes(v, off):
        return jnp.zeros((1, LANES), F32).at[0, off:off + v.shape[0]].set(v)

    full = lambda shape: pl.BlockSpec(shape, lambda n: (0,) * len(shape))
    return pl.pallas_call(
        functools.partial(_gdn_stage_kernel, chunk=chunk, nb=nb),
        grid=(nc,),
        in_specs=[pl.BlockSpec((nb, chunk, NCONV), lambda n: (0, n, 0)),
                  pl.BlockSpec((nb, chunk, NH * NV), lambda n: (0, n, z_col)),
                  pl.BlockSpec((nb, chunk, LANES), lambda n: (0, n, 0)),
                  full((CW, NCONV)), full((1, LANES)), full((1, LANES)), full((1, NV))],
        out_specs=[pl.BlockSpec((nb, chunk, NH * NV), lambda n: (0, n, 0)),
                   full((nb, NH, NK, NV)), full((nb, CW - 1, NCONV))],
        out_shape=[jax.ShapeDtypeStruct((nb, seq, NH * NV), BF),
                   jax.ShapeDtypeStruct((nb, NH, NK, NV), F32),
                   jax.ShapeDtypeStruct((nb, CW - 1, NCONV), F32)],
        scratch_shapes=[pltpu.VMEM((nb, chunk + 8, NCONV), F32), pltpu.VMEM((nb, NH, NK, NV), F32)],
        compiler_params=_cparams(("arbitrary",)),
        name="gdn_chunk_scan",
    )(qkv, zz, small, lp['dn_conv_w'], pad_lanes(lp['dn_dt_bias'], L_NA),
      pad_lanes(-jnp.exp(lp['dn_a_log']), L_NA), lp['dn_norm'].reshape(1, NV))


def _merge_kernel(x_ref, ao_ref, so_ref, no_ref, g0_ref, g1_ref, g2_ref, wa_ref, ws_ref, wn_ref, wo_ref,
                  lg_ref, lb_ref, wr_ref, wsg_ref, wsu_ref, wsd_ref, x1p_ref, r_ref, lgt_ref):
    def gate(ref):
        return _sigmoid(ref[...].astype(F32))

    mixed = (gate(g0_ref) * _dot(ao_ref[...].astype(BF), wa_ref[...])
             + gate(g1_ref) * _dot(so_ref[...].astype(BF), ws_ref[...])
             + gate(g2_ref) * _dot(no_ref[...].astype(BF), wn_ref[...]))
    yv = ALPHA * x_ref[...] + _dot(mixed.astype(BF), wo_ref[...])
    mu = jnp.mean(yv, axis=-1, keepdims=True)
    yc = yv - mu
    var = jnp.mean(yc * yc, axis=-1, keepdims=True)
    x1 = yc * lax.rsqrt(var + LN_EPS) * lg_ref[...] + lb_ref[...]
    x1b = x1.astype(BF)
    bits = lax.bitcast_convert_type(x1b.astype(F32), jnp.int32)
    x1p_ref[...] = (bits[:, D // 2:] & PACK_HI) | lax.shift_right_logical(bits[:, :D // 2], 16)
    lgt_ref[...] = _dot(x1b, wr_ref[...])
    hsh = _silu(_dot(x1b, wsg_ref[...])) * _dot(x1b, wsu_ref[...])
    r_ref[...] = ALPHA * x1 + _dot(hsh.astype(BF), wsd_ref[...])


def merge_post(x, ao, so, no, gates, gate_col, wts, tm):
    m = x.shape[0]
    tm = min(tm, m)
    rowb = lambda c: pl.BlockSpec((tm, D), lambda i, c=c: (i, c))
    full = lambda a: pl.BlockSpec(a.shape, lambda i: (0,) * a.ndim)
    return pl.pallas_call(
        _merge_kernel,
        grid=(m // tm,),
        in_specs=[rowb(0), rowb(0), rowb(0), rowb(0), rowb(gate_col), rowb(gate_col + 1), rowb(gate_col + 2)]
        + [full(a) for a in wts],
        out_specs=[pl.BlockSpec((tm, D // 2), lambda i: (i, 0)), rowb(0), pl.BlockSpec((tm, LANES), lambda i: (i, 0))],
        out_shape=[jax.ShapeDtypeStruct((m, D // 2), jnp.int32), jax.ShapeDtypeStruct((m, D), F32),
                   jax.ShapeDtypeStruct((m, LANES), F32)],
        compiler_params=_cparams(("parallel",)),
        name="merge_ln_router_shared",
    )(x, ao, so, no, gates, gates, gates, *wts)


def _moe_kernel(be_ref, nu_ref, x_ref, wg_ref, wu_ref, wd_ref, o_ref):
    i = pl.program_id(0)

    @pl.when(i < nu_ref[0])
    def _():
        words = x_ref[...]
        lo = lax.bitcast_convert_type(lax.shift_left(words, 16), F32).astype(BF)
        hi = lax.bitcast_convert_type(words & PACK_HI, F32).astype(BF)
        h2 = D // 2
        gate = _dot(lo, wg_ref[0:h2, :]) + _dot(hi, wg_ref[h2:D, :])
        up = _dot(lo, wu_ref[0:h2, :]) + _dot(hi, wu_ref[h2:D, :])
        hid = _silu(gate) * up
        o_ref[...] = _dot(hid.astype(BF), wd_ref[...]).astype(o_ref.dtype)

    @pl.when(i >= nu_ref[0])
    def _():
        o_ref[...] = jnp.zeros(o_ref.shape, o_ref.dtype)


def moe_experts(xb, block_expert, n_used, wg, wu, wd, bm, out_dtype):
    n_rows = xb.shape[0]
    ff = wg.shape[2]
    grid_spec = pltpu.PrefetchScalarGridSpec(
        num_scalar_prefetch=2,
        grid=(n_rows // bm,),
        in_specs=[pl.BlockSpec((bm, D // 2), lambda i, be, nu: (i, 0)),
                  pl.BlockSpec((None, D, ff), lambda i, be, nu: (be[i], 0, 0)),
                  pl.BlockSpec((None, D, ff), lambda i, be, nu: (be[i], 0, 0)),
                  pl.BlockSpec((None, ff, D), lambda i, be, nu: (be[i], 0, 0))],
        out_specs=pl.BlockSpec((bm, D), lambda i, be, nu: (i, 0)),
    )
    return pl.pallas_call(
        _moe_kernel,
        grid_spec=grid_spec,
        out_shape=jax.ShapeDtypeStruct((n_rows, D), out_dtype),
        compiler_params=_cparams(("arbitrary",)),
        name="moe_grouped_ffn",
    )(block_expert, n_used, xb, wg, wu, wd)


def _route_kernel(lg_ref, bias_ref, idx_ref, gate_ref, sel_ref):
    lg = lg_ref[...]
    shape = lg.shape
    lane = lax.broadcasted_iota(jnp.int32, shape, 1)
    lanef = lane.astype(F32)
    neg = -jnp.inf
    scores = _sigmoid(lg)
    biased = jnp.where(lane < NE, scores + bias_ref[...], neg)

    def first_max(x):
        m = jnp.max(x, axis=-1, keepdims=True)
        first = jnp.min(jnp.where(x == m, lanef, 2.0 * LANES), axis=-1, keepdims=True)
        return m, first

    gsz = NE // NGRP
    in_group = [(lane >= g * gsz) & (lane < (g + 1) * gsz) for g in range(NGRP)]
    gscore = []
    for g in range(NGRP):
        xg = jnp.where(in_group[g], biased, neg)
        m1, first = first_max(xg)
        m2 = jnp.max(jnp.where(lanef == first, neg, xg), axis=-1, keepdims=True)
        gscore.append(m1 + m2)
    allowed = jnp.zeros(shape, jnp.bool_)
    for g in range(NGRP):
        rank = jnp.zeros_like(gscore[g])
        for g2 in range(NGRP):
            if g2 != g:
                ahead = (gscore[g2] > gscore[g]) | ((gscore[g2] == gscore[g]) & (g2 < g))
                rank = rank + jnp.where(ahead, 1.0, 0.0)
        allowed = allowed | ((rank < TOPG) & in_group[g])
    work = jnp.where(allowed, biased, neg)
    idx_out = jnp.zeros(shape, F32)
    gate_out = jnp.zeros(shape, F32)
    chosen = jnp.zeros(shape, jnp.bool_)
    for j in range(TOPK):
        _, first = first_max(work)
        sel = lanef == first
        sc = jnp.sum(jnp.where(sel, scores, 0.0), axis=-1, keepdims=True)
        idx_out = jnp.where(lane == j, first, idx_out)
        gate_out = jnp.where(lane == j, sc, gate_out)
        chosen = chosen | sel
        work = jnp.where(sel, neg, work)
    gate_ref[...] = gate_out / jnp.sum(gate_out, axis=-1, keepdims=True) * RSCALE
    sel_ref[...] = jnp.where(chosen, 1, 0).astype(jnp.int32)
    idx_ref[...] = idx_out.T[0:TOPK, :].astype(jnp.int32)


def _route(logits, router_bias):
    t = logits.shape[0]
    tm = min(ROUTE_ROWS, t)
    bias = jnp.concatenate([router_bias, jnp.zeros((LANES - NE,), F32)]).reshape(1, LANES)
    rows = pl.BlockSpec((tm, LANES), lambda i: (i, 0))
    return pl.pallas_call(
        _route_kernel,
        grid=(t // tm,),
        in_specs=[rows, pl.BlockSpec((1, LANES), lambda i: (0, 0))],
        out_specs=[pl.BlockSpec((TOPK, tm), lambda i: (0, i)), rows, rows],
        out_shape=[jax.ShapeDtypeStruct((TOPK, t), jnp.int32), jax.ShapeDtypeStruct((t, LANES), F32),
                   jax.ShapeDtypeStruct((t, LANES), jnp.int32)],
        compiler_params=_cparams(("parallel",)),
        name="moe_route_topk",
    )(logits, bias)


def _dispatch(idx_t, sel, bm):
    t = sel.shape[0]
    na = t * TOPK
    assert na % bm == 0
    cum = jnp.cumsum(sel, axis=0)
    counts = cum[-1]
    padded = (counts + bm - 1) // bm * bm
    pad_end = jnp.cumsum(padded)
    dest_t = ((pad_end - padded)[None, :] + cum - sel).T
    experts = jnp.arange(NE, dtype=jnp.int32)[None, :, None]
    pos_t = jnp.sum(jnp.where(idx_t[:, None, :] == experts, dest_t[None, :NE, :], 0), axis=1).astype(jnp.int32)
    n_blocks = na // bm + NE
    starts = jnp.arange(n_blocks, dtype=jnp.int32)[:, None] * bm
    block_expert = jnp.minimum(jnp.sum((pad_end[None, :NE] <= starts).astype(jnp.int32), axis=1), NE - 1)
    n_used = (pad_end[NE - 1:NE] // bm).astype(jnp.int32)
    return block_expert, n_used, pos_t


def _scatter_rows_kernel(pos_ref, x_ref, zeros_ref, o_ref, sem, *, tm):
    del zeros_ref

    def row_copy(t, j):
        return pltpu.make_async_copy(x_ref.at[pl.ds(t, 1), :], o_ref.at[pl.ds(pos_ref[j, t], 1), :], sem)

    def issue(t, carry):
        for j in range(TOPK):
            row_copy(t, j).start()
        return carry

    def drain(t, carry):
        for j in range(TOPK):
            row_copy(t, j).wait()
        return carry

    lax.fori_loop(0, tm, issue, 0)
    lax.fori_loop(0, tm, drain, 0)


def scatter_rows(x, pos_t, n_rows):
    t, w = x.shape
    tm = min(256, t)
    return pl.pallas_call(
        functools.partial(_scatter_rows_kernel, tm=tm),
        grid=(t // tm,),
        in_specs=[pl.BlockSpec((TOPK, tm), lambda i: (0, i), memory_space=pltpu.SMEM),
                  pl.BlockSpec((tm, w), lambda i: (i, 0)),
                  pl.BlockSpec(memory_space=pl.ANY)],
        out_specs=pl.BlockSpec(memory_space=pl.ANY),
        out_shape=jax.ShapeDtypeStruct((n_rows, w), x.dtype),
        scratch_shapes=[pltpu.SemaphoreType.DMA],
        input_output_aliases={2: 0},
        compiler_params=_cparams(("arbitrary",)),
        name="moe_dispatch_rows",
    )(pos_t, x, jnp.zeros((n_rows, w), x.dtype))


def _combine_kernel(yg_ref, gate_ref, r_ref, g_ref, b_ref, o_ref):
    acc = r_ref[...]
    gate = gate_ref[...]
    for j in range(TOPK):
        acc = acc + yg_ref[j].astype(F32) * gate[:, j:j + 1]
    mu = jnp.mean(acc, axis=-1, keepdims=True)
    yc = acc - mu
    var = jnp.mean(yc * yc, axis=-1, keepdims=True)
    o_ref[...] = yc * lax.rsqrt(var + LN_EPS) * g_ref[...] + b_ref[...]


def combine_ln(yg, gate, r, g, b, tm):
    t = r.shape[0]
    tm = min(tm, t)
    return pl.pallas_call(
        _combine_kernel,
        grid=(t // tm,),
        in_specs=[pl.BlockSpec((TOPK, tm, D), lambda i: (0, i, 0)), pl.BlockSpec((tm, LANES), lambda i: (i, 0)),
                  pl.BlockSpec((tm, D), lambda i: (i, 0)), pl.BlockSpec((1, D), lambda i: (0, 0)),
                  pl.BlockSpec((1, D), lambda i: (0, 0))],
        out_specs=pl.BlockSpec((tm, D), lambda i: (i, 0)),
        out_shape=jax.ShapeDtypeStruct((t, D), F32),
        compiler_params=_cparams(("parallel",)),
        name="moe_combine_ln",
    )(yg, gate, r, g.reshape(1, D), b.reshape(1, D))


def moe_finish(r, x1p, logits, lp, bm):
    t = r.shape[0]
    idx_t, gate, sel = _route(logits, lp['router_bias'])
    block_expert, n_used, pos_t = _dispatch(idx_t, sel, bm)
    xb = scatter_rows(x1p, pos_t, (t * TOPK // bm + NE) * bm)
    yb = moe_experts(xb, block_expert, n_used, lp['w_exp_gate'], lp['w_exp_up'], lp['w_exp_down'], bm, BF)
    yg = yb[pos_t.reshape(TOPK * t)].reshape(TOPK, t, D)
    return combine_ln(yg, gate, r, lp['ln2_g'], lp['ln2_b'], 256)


def _rms(x, g):
    return x * lax.rsqrt(jnp.mean(x * x, -1, keepdims=True) + RMS_EPS) * g


def _prep_layer(l, w_in, p):
    wi = w_in[l]
    small = jnp.concatenate([wi[:, R_DT:R_NQKV], wi[:, R_NB:R_GATE], jnp.zeros((D, LANES - 32), wi.dtype)], axis=1)
    lp = {k: v[l] for k, v in p.items()}
    lp.update(
        w_qkv=wi[:, R_Q:R_Z].astype(BF),
        w_zg=jnp.concatenate([wi[:, R_Z:R_XBC], wi[:, R_NZ:R_NB], wi[:, R_GATE:]], axis=1).astype(BF),
        w_xbc=wi[:, R_XBC:R_DT].astype(BF),
        w_nqkv=wi[:, R_NQKV:R_NZ].astype(BF),
        w_small=small.astype(BF),
        w_router_p=jnp.concatenate([lp['w_router'], jnp.zeros((D, LANES - NE), F32)], axis=1).astype(BF),
    )
    for k in ('w_attn_o', 'w_ssd_o', 'w_dn_o', 'w_out', 'w_sh_gate', 'w_sh_up', 'w_sh_down',
              'w_exp_gate', 'w_exp_up', 'w_exp_down'):
        lp[k] = lp[k].astype(BF)
    return lp


def _merge_weights(lp):
    return [lp['w_attn_o'], lp['w_ssd_o'], lp['w_dn_o'], lp['w_out'], lp['ln1_g'].reshape(1, D),
            lp['ln1_b'].reshape(1, D), lp['w_router_p'], lp['w_sh_gate'], lp['w_sh_up'], lp['w_sh_down']]


def prompt_layer(x, lp, layer_idx, tabs):
    lam_init = 0.8 - 0.6 * math.exp(-0.3 * layer_idx)
    q, kf, kb, vf, vb = qkv_project(x, lp['w_qkv'], tabs, 512)
    zg = matmul(x, lp['w_zg'], BF, tm=PROJ_ROWS)
    xbc = matmul(x, lp['w_xbc'], BF, tm=PROJ_ROWS)
    nqkv = matmul(x, lp['w_nqkv'], BF, tm=PROJ_ROWS)
    small = matmul(x, lp['w_small'], F32, tm=PROJ_ROWS)
    ao = flash_diff_attention(q, kb, vb, lp['attn_lambda'], lp['attn_subln'], lam_init, NB, SEQ, ATT_BLK)
    so, ssd_h, ssd_tail = ssd_prompt(xbc, zg, small, lp, NB, SEQ, SSD_CHUNK, 0)
    no, dn_s, dn_tail = gdn_prompt_staged(nqkv.reshape(NB, SEQ, NCONV), zg.reshape(NB, SEQ, -1),
                                          small.reshape(NB, SEQ, LANES), lp, NB, SEQ, GDN_CHUNK, 1)
    no = no.reshape(NB * SEQ, NH * NV)
    x1p, r, logits = merge_post(x, ao, so, no, zg, 2, _merge_weights(lp), 512)
    x2 = moe_finish(r, x1p, logits, lp, MOE_BLK_PROMPT)
    return (x2, kf.reshape(NB, SEQ, KVH, 2 * HD), vf.reshape(NB, SEQ, KVH, 2 * HD),
            ssd_h.reshape(NB, SH, SP, SN), ssd_tail, dn_s, dn_tail)


def _conv_step(hist, new, w, b=None):
    xp = jnp.concatenate([hist, new[:, None, :]], axis=1)
    y = jnp.sum(xp * w[None], axis=1)
    if b is not None:
        y = y + b
    return jax.nn.silu(y), xp[:, 1:]


def decode_layer(x, lp, layer_idx, tabs, cache_k, cache_v, page_table, ssd_h0, ssd_hist, dn_s0, dn_hist):
    lam_init = 0.8 - 0.6 * math.exp(-0.3 * layer_idx)
    q, kf, _, vf, _ = qkv_project(x, lp['w_qkv'], tabs, DB)
    zg = matmul(x, lp['w_zg'], F32, tm=DB)
    xbc_raw = matmul(x, lp['w_xbc'], F32, tm=DB)
    nqkv_raw = matmul(x, lp['w_nqkv'], F32, tm=DB)
    small = matmul(x, lp['w_small'], F32, tm=DB)
    qh = q.reshape(DB, KVH, 2, 2, HD)
    q16 = jnp.einsum('bhgcd,ce->bhgced', qh, jnp.eye(2, dtype=BF)).reshape(DB, 16, 2 * HD)
    k16 = jnp.repeat(kf.reshape(DB, KVH, 2 * HD), 4, axis=1)
    v16 = jnp.repeat(vf.reshape(DB, KVH, 2 * HD), 4, axis=1)
    oc = paged_attention(q16, k16, v16, cache_k, cache_v, page_table, layer_idx, PAGES_PER_STEP)
    oc = oc.reshape(DB, AH, 2, 2 * HD)
    lq = lp['attn_lambda']
    lam = jnp.exp(jnp.sum(lq[0] * lq[1])) - jnp.exp(jnp.sum(lq[2] * lq[3])) + lam_init
    o = oc[:, :, 0] - lam * oc[:, :, 1]
    ao = (_rms(o, lp['attn_subln']) * (1.0 - lam_init)).reshape(DB, AH * 2 * HD)
    xbc, ssd_tail = _conv_step(ssd_hist, xbc_raw, lp['ssd_conv_w'], lp['ssd_conv_b'])
    xs = xbc[:, :SI].reshape(DB, SH, SP)
    bm = jnp.repeat(xbc[:, SI:SI + SG * SN].reshape(DB, SG, SN), SH // SG, axis=1)
    cm = jnp.repeat(xbc[:, SI + SG * SN:].reshape(DB, SG, SN), SH // SG, axis=1)
    dt = jax.nn.softplus(small[:, L_DT:L_DT + SH] + lp['ssd_dt_bias'])
    a_coef = -jnp.exp(lp['ssd_a_log'])
    ssd_h = ssd_h0 * jnp.exp(dt * a_coef)[..., None, None] + (xs * dt[..., None])[..., :, None] * bm[..., None, :]
    ys = jnp.sum(ssd_h * cm[..., None, :], -1) + lp['ssd_d'][:, None] * xs
    gw = SI // SG
    ys = ys.reshape(DB, SG, gw) * jax.nn.silu(zg[:, :SI]).reshape(DB, SG, gw)
    so = _rms(ys, lp['ssd_norm'].reshape(SG, gw)).reshape(DB, SI)
    qkv, dn_tail = _conv_step(dn_hist, nqkv_raw, lp['dn_conv_w'])
    nq = qkv[:, :NKW].reshape(DB, NH, NK)
    nk = qkv[:, NKW:2 * NKW].reshape(DB, NH, NK)
    nv = qkv[:, 2 * NKW:].reshape(DB, NH, NV)
    nq = nq * lax.rsqrt(jnp.sum(nq * nq, -1, keepdims=True) + RMS_EPS) * (NK ** -0.5)
    nk = nk * lax.rsqrt(jnp.sum(nk * nk, -1, keepdims=True) + RMS_EPS)
    beta = jax.nn.sigmoid(small[:, L_NB:L_NB + NH])
    gl = -jnp.exp(lp['dn_a_log']) * jax.nn.softplus(small[:, L_NA:L_NA + NH] + lp['dn_dt_bias'])
    s = dn_s0 * jnp.exp(gl)[..., None, None]
    delta = (nv - jnp.sum(s * nk[..., :, None], -2)) * beta[..., None]
    dn_s = s + nk[..., :, None] * delta[..., None, :]
    on = jnp.sum(dn_s * nq[..., :, None], -2)
    no = (_rms(on, lp['dn_norm']) * jax.nn.silu(zg[:, SI:2 * SI].reshape(DB, NH, NV))).reshape(DB, NH * NV)
    x1p, r, logits = merge_post(x, ao, so, no, zg, 2, _merge_weights(lp), DB)
    x2 = moe_finish(r, x1p, logits, lp, MOE_BLK_DECODE)
    return (x2, kf.reshape(DB, 1, KVH, 2 * HD), vf.reshape(DB, 1, KVH, 2 * HD), ssd_h, ssd_tail, dn_s, dn_tail)


def kernel(x_prompt, x_sample, cache_k, cache_v, state_ssd, state_ssd_conv, state_dn, state_dn_conv, page_table,
           w_in, attn_lambda, attn_subln, w_attn_o, ssd_conv_w, ssd_conv_b, ssd_dt_bias, ssd_a_log, ssd_d, ssd_norm,
           w_ssd_o, dn_conv_w, dn_dt_bias, dn_a_log, dn_norm, w_dn_o, w_out, ln1_g, ln1_b, w_router, router_bias,
           w_exp_gate, w_exp_up, w_exp_down, w_sh_gate, w_sh_up, w_sh_down, ln2_g, ln2_b):
    params = dict(attn_lambda=attn_lambda, attn_subln=attn_subln, w_attn_o=w_attn_o, ssd_conv_w=ssd_conv_w,
                  ssd_conv_b=ssd_conv_b, ssd_dt_bias=ssd_dt_bias, ssd_a_log=ssd_a_log, ssd_d=ssd_d,
                  ssd_norm=ssd_norm, w_ssd_o=w_ssd_o, dn_conv_w=dn_conv_w, dn_dt_bias=dn_dt_bias,
                  dn_a_log=dn_a_log, dn_norm=dn_norm, w_dn_o=w_dn_o, w_out=w_out, ln1_g=ln1_g, ln1_b=ln1_b,
                  w_router=w_router, router_bias=router_bias, w_exp_gate=w_exp_gate, w_exp_up=w_exp_up,
                  w_exp_down=w_exp_down, w_sh_gate=w_sh_gate, w_sh_up=w_sh_up, w_sh_down=w_sh_down,
                  ln2_g=ln2_g, ln2_b=ln2_b)
    tabs_p = _rope_tables(jnp.arange(SEQ, dtype=jnp.int32))
    tabs_d = _rope_tables(jnp.full((DB,), PAST, dtype=jnp.int32))
    xp = x_prompt.reshape(NB * SEQ, D)
    xs = x_sample.reshape(DB, D)
    outs = [[] for _ in range(12)]
    for l in range(DEPTH):
        lp = _prep_layer(l, w_in, params)
        xp, kp, vp, hp, cp, sp, dp = prompt_layer(xp, lp, l, tabs_p)
        xs, ks, vs, hs, cs, ss, ds = decode_layer(xs, lp, l, tabs_d, cache_k, cache_v, page_table,
                                                  state_ssd[l], state_ssd_conv[l], state_dn[l], state_dn_conv[l])
        for lst, val in zip(outs, (kp, vp, ks, vs, hp, cp, hs, cs, sp, dp, ss, ds)):
            lst.append(val)
    return (xp.reshape(NB, SEQ, D), xs.reshape(DB, 1, D)) + tuple(jnp.stack(o) for o in outs)
```

```python
import functools
import math

import jax
import jax.numpy as jnp
from jax import lax
from jax.experimental import pallas as pl
from jax.experimental.pallas import tpu as pltpu

D = 1024
NB, SEQ = 4, 4096
DEPTH = 2
DB = 32
PAST = 16384
PAGE = 128
AH, KVH, HD = 8, 4, 64
ROT = 16
THETA = 500000.0
SH, SP, SI, SG, SN = 16, 64, 1024, 2, 128
SCONV = SI + 2 * SG * SN
NH, NK, NV = 8, 128, 128
NKW = 1024
NCONV = 3072
CW = 4
NE, TOPK, NGRP, TOPG = 64, 8, 8, 4
RSCALE = 2.5
ALPHA = (2 * DEPTH) ** 0.25
LN_EPS = 1e-5
RMS_EPS = 1e-6

R_Q, R_K, R_V, R_Z, R_XBC, R_DT, R_NQKV, R_NZ, R_NB, R_NA, R_GATE = (
    0, 1024, 1536, 2048, 3072, 4608, 4624, 7696, 8720, 8728, 8736)
L_DT, L_NB, L_NA = 0, 16, 24

LANES = 128
VMEM_LIMIT = 56 * 1024 * 1024
SSD_CHUNK = 128
GDN_CHUNK = 64
ATT_BLK = 512
FLASH_ROWS = 32
PAGES_PER_STEP = 16
ROUTE_ROWS = 1024
PROJ_ROWS = 2048
MOE_BLK_PROMPT = 512
MOE_BLK_DECODE = 16

BF = jnp.bfloat16
F32 = jnp.float32
PACK_HI = -65536


def _cparams(sem):
    return pltpu.CompilerParams(dimension_semantics=sem, vmem_limit_bytes=VMEM_LIMIT)


def _dot(a, b):
    return jnp.dot(a, b, preferred_element_type=F32)


def _dot_nt(a, b):
    return lax.dot_general(a, b, (((1,), (1,)), ((), ())), preferred_element_type=F32)


def _dot_tn(a, b):
    return lax.dot_general(a, b, (((0,), (0,)), ((), ())), preferred_element_type=F32)


def _split2(x):
    hi = x.astype(BF)
    lo = (x - hi.astype(F32)).astype(BF)
    return hi, lo


def _split3(x):
    hi = x.astype(BF)
    r = x - hi.astype(F32)
    mid = r.astype(BF)
    lo = (r - mid.astype(F32)).astype(BF)
    return hi, mid, lo


def _dot_sel(sel, x):
    hi, mid, lo = _split3(x)
    return _dot(sel, hi) + _dot(sel, mid) + _dot(sel, lo)


def _dot_x_sel(x, sel):
    hi, mid, lo = _split3(x)
    return _dot(hi, sel) + _dot(mid, sel) + _dot(lo, sel)


def _dot3(a, b):
    ah, al = _split2(a)
    bh, bl = _split2(b)
    return _dot(ah, bh) + _dot(ah, bl) + _dot(al, bh)


def _sigmoid(x):
    return 1.0 / (1.0 + jnp.exp(-x))


def _silu(x):
    return x * _sigmoid(x)


def _softplus(x):
    return jnp.maximum(x, 0.0) + jnp.log(1.0 + jnp.exp(-jnp.abs(x)))


def _mm_kernel(x_ref, w_ref, o_ref, xs_ref):
    @pl.when(pl.program_id(1) == 0)
    def _():
        xs_ref[...] = x_ref[...].astype(BF)

    o_ref[...] = _dot(xs_ref[...], w_ref[...]).astype(o_ref.dtype)


def matmul(x, w, out_dtype, tm=512, tn=512):
    m, k = x.shape
    n = w.shape[1]
    tm, tn = min(tm, m), min(tn, n)
    assert m % tm == 0 and n % tn == 0, (m, n, tm, tn)
    return pl.pallas_call(
        _mm_kernel,
        grid=(m // tm, n // tn),
        in_specs=[pl.BlockSpec((tm, k), lambda i, j: (i, 0)),
                  pl.BlockSpec((k, tn), lambda i, j: (0, j))],
        out_specs=pl.BlockSpec((tm, tn), lambda i, j: (i, j)),
        out_shape=jax.ShapeDtypeStruct((m, n), out_dtype),
        scratch_shapes=[pltpu.VMEM((tm, k), BF)],
        compiler_params=_cparams(("parallel", "arbitrary")),
        name="proj_matmul",
    )(x, w)


def _rope_tables(pos):
    half = ROT // 2
    inv = THETA ** (-jnp.arange(half, dtype=F32) * 2.0 / ROT)
    ang = pos.astype(F32)[:, None] * inv
    cos, sin = jnp.cos(ang), jnp.sin(ang)
    t = pos.shape[0]
    one, zero, z8 = jnp.ones((t, HD - ROT), F32), jnp.zeros((t, HD - ROT), F32), jnp.zeros((t, half), F32)
    c = jnp.concatenate([cos, cos, one, cos, cos, one], 1)
    sa = jnp.concatenate([-sin, z8, zero, -sin, z8, zero], 1)
    sb = jnp.concatenate([z8, sin, zero, z8, sin, zero], 1)
    return c, sa, sb


def _qkv_kernel(x_ref, w_ref, c_ref, sa_ref, sb_ref, q_ref, kf_ref, kb_ref, vf_ref, vb_ref):
    acc = _dot(x_ref[...].astype(BF), w_ref[...])
    c, sa, sb = c_ref[...], sa_ref[...], sb_ref[...]

    def rot(xg):
        return xg * c + pltpu.roll(xg, LANES - ROT // 2, 1) * sa + pltpu.roll(xg, ROT // 2, 1) * sb

    for h in range(AH):
        sl = slice(h * LANES, (h + 1) * LANES)
        q_ref[:, sl] = (rot(acc[:, sl]) * (HD ** -0.5)).astype(BF)
    for h in range(KVH):
        sl = slice(h * LANES, (h + 1) * LANES)
        kr = rot(acc[:, R_K + h * LANES:R_K + (h + 1) * LANES])
        kf_ref[:, h, :] = kr
        kb_ref[:, sl] = kr.astype(BF)
    v = acc[:, R_V:R_Z]
    for h in range(KVH):
        vf_ref[:, h, :] = v[:, h * LANES:(h + 1) * LANES]
    vb_ref[...] = v.astype(BF)


def qkv_project(x, w, tabs, tm):
    m = x.shape[0]
    tm = min(tm, m)
    nt = tabs[0].shape[0] // tm
    kw = KVH * 2 * HD
    tab_spec = pl.BlockSpec((tm, LANES), lambda i: (i % nt, 0))
    return pl.pallas_call(
        _qkv_kernel,
        grid=(m // tm,),
        in_specs=[pl.BlockSpec((tm, D), lambda i: (i, 0)),
                  pl.BlockSpec((D, R_Z), lambda i: (0, 0)),
                  tab_spec, tab_spec, tab_spec],
        out_specs=[pl.BlockSpec((tm, AH * 2 * HD), lambda i: (i, 0)),
                   pl.BlockSpec((tm, KVH, 2 * HD), lambda i: (i, 0, 0)), pl.BlockSpec((tm, kw), lambda i: (i, 0)),
                   pl.BlockSpec((tm, KVH, 2 * HD), lambda i: (i, 0, 0)), pl.BlockSpec((tm, kw), lambda i: (i, 0))],
        out_shape=[jax.ShapeDtypeStruct((m, AH * 2 * HD), BF),
                   jax.ShapeDtypeStruct((m, KVH, 2 * HD), F32), jax.ShapeDtypeStruct((m, kw), BF),
                   jax.ShapeDtypeStruct((m, KVH, 2 * HD), F32), jax.ShapeDtypeStruct((m, kw), BF)],
        compiler_params=_cparams(("parallel",)),
        name="qkv_rope",
    )(x, w, *tabs)


def _lambda_value(lq, lam_init):
    a = jnp.sum(lq[0:1, :] * lq[1:2, :], axis=-1, keepdims=True)
    b = jnp.sum(lq[2:3, :] * lq[3:4, :], axis=-1, keepdims=True)
    return jnp.exp(a) - jnp.exp(b) + lam_init


def _flash_kernel(qi_ref, ki_ref, lq_ref, sub_ref, q_ref, k_ref, v_ref, o_ref, qs_ref, m_ref, acc_ref, s_ref, p_ref,
                  *, blk, lam_init):
    qi, ki = qi_ref[pl.program_id(2)], ki_ref[pl.program_id(2)]

    @pl.when(ki == 0)
    def _():
        q = q_ref[...]
        lane = lax.broadcasted_iota(jnp.int32, (blk, LANES), 1)
        for c in range(2):
            msk = (lane < HD) if c == 0 else (lane >= HD)
            for g in range(2):
                qs_ref[c, g * blk:(g + 1) * blk, :] = jnp.where(msk, q[:, g * LANES:(g + 1) * LANES], 0)
        m_ref[...] = jnp.full(m_ref.shape, -jnp.inf, F32)
        acc_ref[...] = jnp.zeros(acc_ref.shape, F32)

    def update(masked):
        k = k_ref[...]
        vext = jnp.concatenate([v_ref[...], jnp.ones((blk, LANES), BF)], axis=1)
        units = [(c, g) for g in range(2) for c in range(2)]
        for u, (c, g) in enumerate(units):
            s_ref[u] = _dot_nt(qs_ref[c, g * blk:(g + 1) * blk, :], k)
        if masked:
            keep = (lax.broadcasted_iota(jnp.int32, (blk, blk), 1) <= lax.broadcasted_iota(jnp.int32, (blk, blk), 0))
            diff = (lax.broadcasted_iota(jnp.int32, (FLASH_ROWS, blk), 1)
                    - lax.broadcasted_iota(jnp.int32, (FLASH_ROWS, blk), 0))
        for u, (c, g) in enumerate(units):
            rows = slice(g * blk, (g + 1) * blk)
            s_full = s_ref[u]
            if masked:
                s_full = jnp.where(keep, s_full, -jnp.inf)
            m_old = m_ref[c, rows]
            m_new = jnp.maximum(m_old, jnp.max(s_full, axis=-1, keepdims=True))
            alpha = jnp.exp(m_old - m_new)
            for r in range(0, blk, FLASH_ROWS):
                s = s_ref[u, r:r + FLASH_ROWS, :]
                if masked:
                    s = jnp.where(diff <= r, s, -jnp.inf)
                p_ref[u, r:r + FLASH_ROWS, :] = jnp.exp(s - m_new[r:r + FLASH_ROWS]).astype(BF)
            m_ref[c, rows] = m_new
            acc_ref[c, rows] = alpha * acc_ref[c, rows] + _dot(p_ref[u], vext)

    @pl.when(ki < qi)
    def _():
        update(False)

    @pl.when(ki == qi)
    def _():
        update(True)
        lam = _lambda_value(lq_ref[...], lam_init)
        a0, a1 = acc_ref[0], acc_ref[1]
        o = a0[:, :LANES] / a0[:, LANES:] - lam * (a1[:, :LANES] / a1[:, LANES:])
        o = o * lax.rsqrt(jnp.mean(o * o, axis=-1, keepdims=True) + RMS_EPS) * sub_ref[...] * (1.0 - lam_init)
        o_ref[:, 0:LANES] = o[0:blk].astype(BF)
        o_ref[:, LANES:2 * LANES] = o[blk:2 * blk].astype(BF)


def flash_diff_attention(q, k, v, lq, subln, lam_init, nb, seq, blk):
    nblk = seq // blk
    pairs = [(i, j) for i in range(nblk) for j in range(i + 1)]
    qi_arr = jnp.asarray([p[0] for p in pairs], jnp.int32)
    ki_arr = jnp.asarray([p[1] for p in pairs], jnp.int32)
    kern = functools.partial(_flash_kernel, blk=blk, lam_init=lam_init)
    grid_spec = pltpu.PrefetchScalarGridSpec(
        num_scalar_prefetch=2,
        grid=(nb, KVH, len(pairs)),
        in_specs=[pl.BlockSpec((4, HD), lambda b, h, s, qa, ka: (0, 0)),
                  pl.BlockSpec((1, LANES), lambda b, h, s, qa, ka: (0, 0)),
                  pl.BlockSpec((blk, 2 * LANES), lambda b, h, s, qa, ka: (b * nblk + qa[s], h)),
                  pl.BlockSpec((blk, LANES), lambda b, h, s, qa, ka: (b * nblk + ka[s], h)),
                  pl.BlockSpec((blk, LANES), lambda b, h, s, qa, ka: (b * nblk + ka[s], h))],
        out_specs=pl.BlockSpec((blk, 2 * LANES), lambda b, h, s, qa, ka: (b * nblk + qa[s], h)),
        scratch_shapes=[pltpu.VMEM((2, 2 * blk, LANES), BF), pltpu.VMEM((2, 2 * blk, 1), F32),
                        pltpu.VMEM((2, 2 * blk, 2 * LANES), F32),
                        pltpu.VMEM((4, blk, blk), F32), pltpu.VMEM((4, blk, blk), BF)],
    )
    return pl.pallas_call(
        kern,
        grid_spec=grid_spec,
        out_shape=jax.ShapeDtypeStruct((nb * seq, AH * 2 * HD), BF),
        compiler_params=_cparams(("parallel", "parallel", "arbitrary")),
        name="flash_diff_attn",
    )(qi_arr, ki_arr, lq, subln.reshape(1, LANES), q, k, v)


def _paged_kernel(pt_ref, q_ref, ks_ref, vs_ref, *refs, pps):
    kp, vp = refs[:pps], refs[pps:2 * pps]
    o_ref, m_ref, l_ref, acc_ref = refs[2 * pps:]
    j = pl.program_id(1)
    q = q_ref[...]
    pw = PAGE * KVH

    @pl.when(j == 0)
    def _():
        ks = ks_ref[...].astype(BF).astype(F32)
        m_ref[...] = jnp.sum(q.astype(F32) * ks, axis=-1, keepdims=True)
        l_ref[...] = jnp.ones(l_ref.shape, F32)
        acc_ref[...] = vs_ref[...].astype(BF).astype(F32)

    row_head = lax.broadcasted_iota(jnp.int32, (16, pw), 0) // 4
    col_head = lax.broadcasted_iota(jnp.int32, (16, pw), 1) % KVH
    own = row_head == col_head
    s = jnp.concatenate([jnp.where(own, _dot_nt(q, kp[i][...].astype(BF)), -jnp.inf) for i in range(pps)], axis=1)
    m_old = m_ref[...]
    m_new = jnp.maximum(m_old, jnp.max(s, axis=-1, keepdims=True))
    alpha = jnp.exp(m_old - m_new)
    p = jnp.exp(s - m_new)
    l_ref[...] = alpha * l_ref[...] + jnp.sum(p, axis=-1, keepdims=True)
    pv = _dot(p[:, 0:pw].astype(BF), vp[0][...].astype(BF))
    for i in range(1, pps):
        pv = pv + _dot(p[:, i * pw:(i + 1) * pw].astype(BF), vp[i][...].astype(BF))
    acc_ref[...] = alpha * acc_ref[...] + pv
    m_ref[...] = m_new

    @pl.when(j == pl.num_programs(1) - 1)
    def _():
        o_ref[...] = acc_ref[...] / l_ref[...]


def paged_attention(q16, k16, v16, cache_k, cache_v, page_table, layer, pps):
    nseq, npg = page_table.shape
    pw = PAGE * KVH
    ck = cache_k.reshape(cache_k.shape[0], cache_k.shape[1], pw, 2 * HD)
    cv = cache_v.reshape(cache_v.shape[0], cache_v.shape[1], pw, 2 * HD)

    def page_spec(i):
        return pl.BlockSpec((None, None, pw, 2 * HD), lambda b, j, pt: (layer, pt[b, j * pps + i], 0, 0))

    row = pl.BlockSpec((None, 16, 2 * HD), lambda b, j, pt: (b, 0, 0))
    grid_spec = pltpu.PrefetchScalarGridSpec(
        num_scalar_prefetch=1,
        grid=(nseq, npg // pps),
        in_specs=[row, row, row] + [page_spec(i) for i in range(pps)] + [page_spec(i) for i in range(pps)],
        out_specs=row,
        scratch_shapes=[pltpu.VMEM((16, 1), F32), pltpu.VMEM((16, 1), F32), pltpu.VMEM((16, 2 * HD), F32)],
    )
    return pl.pallas_call(
        functools.partial(_paged_kernel, pps=pps),
        grid_spec=grid_spec,
        out_shape=jax.ShapeDtypeStruct((nseq, 16, 2 * HD), F32),
        compiler_params=_cparams(("parallel", "arbitrary")),
        name="paged_diff_attn",
    )(page_table, q16, k16, v16, *([ck] * pps), *([cv] * pps))


def _ssd_kernel(xbc_ref, z_ref, sm_ref, cw_ref, cb_ref, dtb_ref, ac_ref, dx_ref, nw_ref, e_ref,
                y_ref, st_ref, tail_ref, xbuf, s_ref, *, chunk):
    n = pl.program_id(1)
    L = chunk

    @pl.when(n == 0)
    def _():
        xbuf[0:8, :] = jnp.zeros((8, SCONV), F32)
        s_ref[...] = jnp.zeros(s_ref.shape, F32)

    xbuf[8:8 + L, :] = xbc_ref[...].astype(F32)
    y = cb_ref[...] + cw_ref[0:1, :] * xbuf[pl.ds(5, L), :]
    for t in range(1, CW):
        y = y + cw_ref[t:t + 1, :] * xbuf[pl.ds(5 + t, L), :]
    xbc = _silu(y)
    xs, bm, cm = xbc[:, :SI], xbc[:, SI:SI + SG * SN], xbc[:, SI + SG * SN:]

    dt = _softplus(sm_ref[...] + dtb_ref[...])
    a = dt * ac_ref[...]
    row = lax.broadcasted_iota(jnp.int32, (L, L), 0)
    col = lax.broadcasted_iota(jnp.int32, (L, L), 1)
    causal = row >= col
    acum = _dot_sel(causal.astype(BF), a)
    acum_t = acum.T
    alast = acum[L - 1:L, :]
    e = e_ref[...]
    dt_e = _dot_x_sel(dt, e)
    ea_e = _dot_x_sel(jnp.exp(acum), e)
    dend_e = _dot_x_sel(jnp.exp(alast - acum), e)
    xdt = xs * dt_e
    xd = (xdt * dend_e).astype(BF)
    lane = lax.broadcasted_iota(jnp.int32, (L, LANES), 1)
    gw = SI // SG
    hpg = SH // SG
    y_parts = []
    for g in range(SG):
        cg = cm[:, g * SN:(g + 1) * SN].astype(BF)
        bg = bm[:, g * SN:(g + 1) * SN].astype(BF)
        s_old = s_ref[g * gw:(g + 1) * gw, :]
        y_inter = _dot_nt(cg, s_old.astype(BF))
        cb = _dot_nt(cg, bg)
        pairs = []
        for jp in range(hpg // 2):
            h0 = g * hpg + 2 * jp
            xp = xdt[:, h0 * SP:h0 * SP + LANES]
            acc = None
            for d in range(2):
                h = h0 + d
                seg = jnp.where(causal, jnp.exp(acum[:, h:h + 1] - acum_t[h:h + 1, :]), 0.0)
                mh = (cb * seg).astype(BF)
                xm = jnp.where((lane < SP) if d == 0 else (lane >= SP), xp, 0.0).astype(BF)
                t = _dot(mh, xm)
                acc = t if acc is None else acc + t
            pairs.append(acc)
        y_intra = jnp.concatenate(pairs, axis=1)
        y_parts.append(y_intra + y_inter * ea_e[:, g * gw:(g + 1) * gw])
        upd = _dot_tn(xd[:, g * gw:(g + 1) * gw], bg)
        for hh in range(hpg):
            h = g * hpg + hh
            r0 = h * SP
            s_ref[r0:r0 + SP, :] = (s_old[hh * SP:(hh + 1) * SP, :] * jnp.exp(alast[:, h:h + 1])
                                    + upd[hh * SP:(hh + 1) * SP, :])
    yv = jnp.concatenate(y_parts, axis=1) + dx_ref[...] * xs
    yv = yv * _silu(z_ref[...].astype(F32))
    outs = []
    for g in range(SG):
        yg = yv[:, g * gw:(g + 1) * gw]
        outs.append(yg * lax.rsqrt(jnp.mean(yg * yg, axis=-1, keepdims=True) + RMS_EPS) * nw_ref[:, g * gw:(g + 1) * gw])
    y_ref[...] = jnp.concatenate(outs, axis=1).astype(BF)
    xbuf[0:8, :] = xbuf[L:L + 8, :]

    @pl.when(n == pl.num_programs(1) - 1)
    def _():
        st_ref[...] = s_ref[...]
        tail_ref[...] = xbuf[pl.ds(L + 5, CW - 1), :]


def ssd_prompt(xbc, zz, small, lp, nb, seq, chunk, z_col):
    nc = seq // chunk
    heads = jnp.arange(LANES)[:, None]
    e = (heads == (jnp.arange(SI)[None, :] // SP)).astype(BF)

    def pad_lanes(v, off):
        return jnp.zeros((1, LANES), F32).at[0, off:off + v.shape[0]].set(v)

    full = lambda shape: pl.BlockSpec(shape, lambda b, n: (0,) * len(shape))
    return pl.pallas_call(
        functools.partial(_ssd_kernel, chunk=chunk),
        grid=(nb, nc),
        in_specs=[pl.BlockSpec((chunk, SCONV), lambda b, n: (b * nc + n, 0)),
                  pl.BlockSpec((chunk, SI), lambda b, n: (b * nc + n, z_col)),
                  pl.BlockSpec((chunk, LANES), lambda b, n: (b * nc + n, 0)),
                  full((CW, SCONV)), full((1, SCONV)), full((1, LANES)), full((1, LANES)),
                  full((1, SI)), full((1, SI)), full((LANES, SI))],
        out_specs=[pl.BlockSpec((chunk, SI), lambda b, n: (b * nc + n, 0)),
                   pl.BlockSpec((None, SH * SP, SN), lambda b, n: (b, 0, 0)),
                   pl.BlockSpec((None, CW - 1, SCONV), lambda b, n: (b, 0, 0))],
        out_shape=[jax.ShapeDtypeStruct((nb * seq, SI), BF),
                   jax.ShapeDtypeStruct((nb, SH * SP, SN), F32),
                   jax.ShapeDtypeStruct((nb, CW - 1, SCONV), F32)],
        scratch_shapes=[pltpu.VMEM((chunk + 8, SCONV), F32), pltpu.VMEM((SH * SP, SN), F32)],
        compiler_params=_cparams(("parallel", "arbitrary")),
        name="ssd_chunk_scan",
    )(xbc, zz, small, lp['ssd_conv_w'], lp['ssd_conv_b'].reshape(1, SCONV),
      pad_lanes(lp['ssd_dt_bias'], L_DT), pad_lanes(-jnp.exp(lp['ssd_a_log']), L_DT),
      jnp.repeat(lp['ssd_d'], SP).reshape(1, SI), lp['ssd_norm'].reshape(1, SI), e)


def _gdn_stage_kernel(qkv_ref, z_ref, sm_ref, cw_ref, dtb_ref, na_ref, nw_ref,
                      o_ref, st_ref, tail_ref, xbuf, s_ref, *, chunk, nb):
    n = pl.program_id(0)
    C = chunk

    @pl.when(n == 0)
    def _():
        xbuf[:, 0:8, :] = jnp.zeros((nb, 8, NCONV), F32)
        s_ref[...] = jnp.zeros(s_ref.shape, F32)

    r128 = lax.broadcasted_iota(jnp.int32, (LANES, LANES), 0)
    c128 = lax.broadcasted_iota(jnp.int32, (LANES, LANES), 1)
    tri128 = (r128 >= c128).astype(BF)
    row = lax.broadcasted_iota(jnp.int32, (C, C), 0)
    col = lax.broadcasted_iota(jnp.int32, (C, C), 1)
    eye = (row == col).astype(F32)
    ch = []
    for b in range(nb):
        xbuf[b, 8:8 + C, :] = qkv_ref[b].astype(F32)
        y = cw_ref[0:1, :] * xbuf[b, pl.ds(5, C), :]
        for t in range(1, CW):
            y = y + cw_ref[t:t + 1, :] * xbuf[b, pl.ds(5 + t, C), :]
        qkv = _silu(y)
        sm = sm_ref[b]
        beta = _sigmoid(sm)
        g = na_ref[...] * _softplus(sm + dtb_ref[...])
        gc = _dot_sel(tri128, jnp.concatenate([g, jnp.zeros((LANES - C, LANES), F32)], axis=0))
        gc_t = gc.T
        for h in range(NH):
            qh = qkv[:, h * NK:(h + 1) * NK]
            kh = qkv[:, NKW + h * NK:NKW + (h + 1) * NK]
            vh = qkv[:, 2 * NKW + h * NV:2 * NKW + (h + 1) * NV]
            qh = qh * lax.rsqrt(jnp.sum(qh * qh, axis=-1, keepdims=True) + RMS_EPS) * (NK ** -0.5)
            kh = kh * lax.rsqrt(jnp.sum(kh * kh, axis=-1, keepdims=True) + RMS_EPS)
            la = L_NA + h
            gcol = gc[0:C, la:la + 1]
            grow = gc_t[la:la + 1, 0:C]
            glast = gc[C - 1:C, la:la + 1]
            bh = beta[:, L_NB + h:L_NB + h + 1]
            kb = kh * bh
            eg = jnp.exp(gcol)
            ch.append(dict(
                b=b, h=h, glast=glast,
                dmat=jnp.exp(jnp.where(row >= col, gcol - grow, -1e30)),
                kbb=kb.astype(BF), khb=kh.astype(BF), qhb=qh.astype(BF),
                rhs=jnp.concatenate([vh * bh, kb * eg], axis=1),
                qg=(qh * eg).astype(BF), kd=(kh * jnp.exp(glast - gcol)).astype(BF)))
    for c in ch:
        c['x'] = -jnp.where(row > col, _dot_nt(c['kbb'], c['khb']) * c['dmat'], 0.0)
        c['qk'] = (_dot_nt(c['qhb'], c['khb']) * c['dmat']).astype(BF)
    for c in ch:
        c['p'] = _dot3(c['x'], c['x'])
    span = 2
    first = True
    while span < C:
        for c in ch:
            tm = (eye + c['x']) if first else c['t']
            if first:
                c['t'] = tm + _dot3(tm, c['p'])
            else:
                c['t'] = tm + _dot(tm.astype(BF), c['p'].astype(BF))
            if span * 2 < C:
                pb = c['p'].astype(BF)
                c['p'] = _dot(pb, pb)
        first = False
        span *= 2
    for c in ch:
        c['sol'] = _dot3(c['t'], c['rhs'])
    for c in ch:
        s_old = s_ref[c['b'], c['h']]
        sb = s_old.astype(BF)
        c['s_old'] = s_old
        c['vn'] = (c['sol'][:, :NV] - _dot(c['sol'][:, NV:].astype(BF), sb)).astype(BF)
        c['oq'] = _dot(c['qg'], sb)
    for c in ch:
        b, h = c['b'], c['h']
        o = c['oq'] + _dot(c['qk'], c['vn'])
        s_ref[b, h] = c['s_old'] * jnp.exp(c['glast']) + _dot_tn(c['kd'], c['vn'])
        o = o * lax.rsqrt(jnp.mean(o * o, axis=-1, keepdims=True) + RMS_EPS) * nw_ref[...]
        sl = slice(h * NV, (h + 1) * NV)
        o_ref[b, :, sl] = (o * _silu(z_ref[b, :, sl].astype(F32))).astype(BF)
    xbuf[:, 0:8, :] = xbuf[:, C:C + 8, :]

    @pl.when(n == pl.num_programs(0) - 1)
    def _():
        st_ref[...] = s_ref[...]
        tail_ref[...] = xbuf[:, pl.ds(C + 5, CW - 1), :]


def gdn_prompt_staged(qkv, zz, small, lp, nb, seq, chunk, z_col):
    nc = seq // chunk

    def pad_lanes(v, off):
        return jnp.zeros((1, LANES), F32).at[0, off:off + v.shape[0]].set(v)

    full = lambda shape: pl.BlockSpec(shape, lambda n: (0,) * len(shape))
    return pl.pallas_call(
        functools.partial(_gdn_stage_kernel, chunk=chunk, nb=nb),
        grid=(nc,),
        in_specs=[pl.BlockSpec((nb, chunk, NCONV), lambda n: (0, n, 0)),
                  pl.BlockSpec((nb, chunk, NH * NV), lambda n: (0, n, z_col)),
                  pl.BlockSpec((nb, chunk, LANES), lambda n: (0, n, 0)),
                  full((CW, NCONV)), full((1, LANES)), full((1, LANES)), full((1, NV))],
        out_specs=[pl.BlockSpec((nb, chunk, NH * NV), lambda n: (0, n, 0)),
                   full((nb, NH, NK, NV)), full((nb, CW - 1, NCONV))],
        out_shape=[jax.ShapeDtypeStruct((nb, seq, NH * NV), BF),
                   jax.ShapeDtypeStruct((nb, NH, NK, NV), F32),
                   jax.ShapeDtypeStruct((nb, CW - 1, NCONV), F32)],
        scratch_shapes=[pltpu.VMEM((nb, chunk + 8, NCONV), F32), pltpu.VMEM((nb, NH, NK, NV), F32)],
        compiler_params=_cparams(("arbitrary",)),
        name="gdn_chunk_scan",
    )(qkv, zz, small, lp['dn_conv_w'], pad_lanes(lp['dn_dt_bias'], L_NA),
      pad_lanes(-jnp.exp(lp['dn_a_log']), L_NA), lp['dn_norm'].reshape(1, NV))


def _merge_kernel(x_ref, ao_ref, so_ref, no_ref, g0_ref, g1_ref, g2_ref, wa_ref, ws_ref, wn_ref, wo_ref,
                  lg_ref, lb_ref, wr_ref, wsg_ref, wsu_ref, wsd_ref, x1p_ref, r_ref, lgt_ref):
    def gate(ref):
        return _sigmoid(ref[...].astype(F32))

    mixed = (gate(g0_ref) * _dot(ao_ref[...].astype(BF), wa_ref[...])
             + gate(g1_ref) * _dot(so_ref[...].astype(BF), ws_ref[...])
             + gate(g2_ref) * _dot(no_ref[...].astype(BF), wn_ref[...]))
    yv = ALPHA * x_ref[...] + _dot(mixed.astype(BF), wo_ref[...])
    mu = jnp.mean(yv, axis=-1, keepdims=True)
    yc = yv - mu
    var = jnp.mean(yc * yc, axis=-1, keepdims=True)
    x1 = yc * lax.rsqrt(var + LN_EPS) * lg_ref[...] + lb_ref[...]
    x1b = x1.astype(BF)
    bits = lax.bitcast_convert_type(x1b.astype(F32), jnp.int32)
    x1p_ref[...] = (bits[:, D // 2:] & PACK_HI) | lax.shift_right_logical(bits[:, :D // 2], 16)
    lgt_ref[...] = _dot(x1b, wr_ref[...])
    hsh = _silu(_dot(x1b, wsg_ref[...])) * _dot(x1b, wsu_ref[...])
    r_ref[...] = ALPHA * x1 + _dot(hsh.astype(BF), wsd_ref[...])


def merge_post(x, ao, so, no, gates, gate_col, wts, tm):
    m = x.shape[0]
    tm = min(tm, m)
    rowb = lambda c: pl.BlockSpec((tm, D), lambda i, c=c: (i, c))
    full = lambda a: pl.BlockSpec(a.shape, lambda i: (0,) * a.ndim)
    return pl.pallas_call(
        _merge_kernel,
        grid=(m // tm,),
        in_specs=[rowb(0), rowb(0), rowb(0), rowb(0), rowb(gate_col), rowb(gate_col + 1), rowb(gate_col + 2)]
        + [full(a) for a in wts],
        out_specs=[pl.BlockSpec((tm, D // 2), lambda i: (i, 0)), rowb(0), pl.BlockSpec((tm, LANES), lambda i: (i, 0))],
        out_shape=[jax.ShapeDtypeStruct((m, D // 2), jnp.int32), jax.ShapeDtypeStruct((m, D), F32),
                   jax.ShapeDtypeStruct((m, LANES), F32)],
        compiler_params=_cparams(("parallel",)),
        name="merge_ln_router_shared",
    )(x, ao, so, no, gates, gates, gates, *wts)


def _moe_kernel(be_ref, nu_ref, x_ref, wg_ref, wu_ref, wd_ref, o_ref):
    i = pl.program_id(0)

    @pl.when(i < nu_ref[0])
    def _():
        words = x_ref[...]
        lo = lax.bitcast_convert_type(lax.shift_left(words, 16), F32).astype(BF)
        hi = lax.bitcast_convert_type(words & PACK_HI, F32).astype(BF)
        h2 = D // 2
        gate = _dot(lo, wg_ref[0:h2, :]) + _dot(hi, wg_ref[h2:D, :])
        up = _dot(lo, wu_ref[0:h2, :]) + _dot(hi, wu_ref[h2:D, :])
        hid = _silu(gate) * up
        o_ref[...] = _dot(hid.astype(BF), wd_ref[...]).astype(o_ref.dtype)

    @pl.when(i >= nu_ref[0])
    def _():
        o_ref[...] = jnp.zeros(o_ref.shape, o_ref.dtype)


def moe_experts(xb, block_expert, n_used, wg, wu, wd, bm, out_dtype):
    n_rows = xb.shape[0]
    ff = wg.shape[2]
    grid_spec = pltpu.PrefetchScalarGridSpec(
        num_scalar_prefetch=2,
        grid=(n_rows // bm,),
        in_specs=[pl.BlockSpec((bm, D // 2), lambda i, be, nu: (i, 0)),
                  pl.BlockSpec((None, D, ff), lambda i, be, nu: (be[i], 0, 0)),
                  pl.BlockSpec((None, D, ff), lambda i, be, nu: (be[i], 0, 0)),
                  pl.BlockSpec((None, ff, D), lambda i, be, nu: (be[i], 0, 0))],
        out_specs=pl.BlockSpec((bm, D), lambda i, be, nu: (i, 0)),
    )
    return pl.pallas_call(
        _moe_kernel,
        grid_spec=grid_spec,
        out_shape=jax.ShapeDtypeStruct((n_rows, D), out_dtype),
        compiler_params=_cparams(("arbitrary",)),
        name="moe_grouped_ffn",
    )(block_expert, n_used, xb, wg, wu, wd)


def _route_kernel(lg_ref, bias_ref, idx_ref, gate_ref, sel_ref):
    lg = lg_ref[...]
    shape = lg.shape
    lane = lax.broadcasted_iota(jnp.int32, shape, 1)
    lanef = lane.astype(F32)
    neg = -jnp.inf
    scores = _sigmoid(lg)
    biased = jnp.where(lane < NE, scores + bias_ref[...], neg)

    def first_max(x):
        m = jnp.max(x, axis=-1, keepdims=True)
        first = jnp.min(jnp.where(x == m, lanef, 2.0 * LANES), axis=-1, keepdims=True)
        return m, first

    gsz = NE // NGRP
    in_group = [(lane >= g * gsz) & (lane < (g + 1) * gsz) for g in range(NGRP)]
    gscore = []
    for g in range(NGRP):
        xg = jnp.where(in_group[g], biased, neg)
        m1, first = first_max(xg)
        m2 = jnp.max(jnp.where(lanef == first, neg, xg), axis=-1, keepdims=True)
        gscore.append(m1 + m2)
    allowed = jnp.zeros(shape, jnp.bool_)
    for g in range(NGRP):
        rank = jnp.zeros_like(gscore[g])
        for g2 in range(NGRP):
            if g2 != g:
                ahead = (gscore[g2] > gscore[g]) | ((gscore[g2] == gscore[g]) & (g2 < g))
                rank = rank + jnp.where(ahead, 1.0, 0.0)
        allowed = allowed | ((rank < TOPG) & in_group[g])
    work = jnp.where(allowed, biased, neg)
    idx_out = jnp.zeros(shape, F32)
    gate_out = jnp.zeros(shape, F32)
    chosen = jnp.zeros(shape, jnp.bool_)
    for j in range(TOPK):
        _, first = first_max(work)
        sel = lanef == first
        sc = jnp.sum(jnp.where(sel, scores, 0.0), axis=-1, keepdims=True)
        idx_out = jnp.where(lane == j, first, idx_out)
        gate_out = jnp.where(lane == j, sc, gate_out)
        chosen = chosen | sel
        work = jnp.where(sel, neg, work)
    gate_ref[...] = gate_out / jnp.sum(gate_out, axis=-1, keepdims=True) * RSCALE
    sel_ref[...] = jnp.where(chosen, 1, 0).astype(jnp.int32)
    idx_ref[...] = idx_out.T[0:TOPK, :].astype(jnp.int32)


def _route(logits, router_bias):
    t = logits.shape[0]
    tm = min(ROUTE_ROWS, t)
    bias = jnp.concatenate([router_bias, jnp.zeros((LANES - NE,), F32)]).reshape(1, LANES)
    rows = pl.BlockSpec((tm, LANES), lambda i: (i, 0))
    return pl.pallas_call(
        _route_kernel,
        grid=(t // tm,),
        in_specs=[rows, pl.BlockSpec((1, LANES), lambda i: (0, 0))],
        out_specs=[pl.BlockSpec((TOPK, tm), lambda i: (0, i)), rows, rows],
        out_shape=[jax.ShapeDtypeStruct((TOPK, t), jnp.int32), jax.ShapeDtypeStruct((t, LANES), F32),
                   jax.ShapeDtypeStruct((t, LANES), jnp.int32)],
        compiler_params=_cparams(("parallel",)),
        name="moe_route_topk",
    )(logits, bias)


def _dispatch(idx_t, sel, bm):
    t = sel.shape[0]
    na = t * TOPK
    assert na % bm == 0
    cum = jnp.cumsum(sel, axis=0)
    counts = cum[-1]
    padded = (counts + bm - 1) // bm * bm
    pad_end = jnp.cumsum(padded)
    dest_t = ((pad_end - padded)[None, :] + cum - sel).T
    experts = jnp.arange(NE, dtype=jnp.int32)[None, :, None]
    pos_t = jnp.sum(jnp.where(idx_t[:, None, :] == experts, dest_t[None, :NE, :], 0), axis=1).astype(jnp.int32)
    n_blocks = na // bm + NE
    starts = jnp.arange(n_blocks, dtype=jnp.int32)[:, None] * bm
    block_expert = jnp.minimum(jnp.sum((pad_end[None, :NE] <= starts).astype(jnp.int32), axis=1), NE - 1)
    n_used = (pad_end[NE - 1:NE] // bm).astype(jnp.int32)
    return block_expert, n_used, pos_t


def _scatter_rows_kernel(pos_ref, x_ref, zeros_ref, o_ref, sem, *, tm):
    del zeros_ref

    def row_copy(t, j):
        return pltpu.make_async_copy(x_ref.at[pl.ds(t, 1), :], o_ref.at[pl.ds(pos_ref[j, t], 1), :], sem)

    def issue(t, carry):
        for j in range(TOPK):
            row_copy(t, j).start()
        return carry

    def drain(t, carry):
        for j in range(TOPK):
            row_copy(t, j).wait()
        return carry

    lax.fori_loop(0, tm, issue, 0)
    lax.fori_loop(0, tm, drain, 0)


def scatter_rows(x, pos_t, n_rows):
    t, w = x.shape
    tm = min(256, t)
    return pl.pallas_call(
        functools.partial(_scatter_rows_kernel, tm=tm),
        grid=(t // tm,),
        in_specs=[pl.BlockSpec((TOPK, tm), lambda i: (0, i), memory_space=pltpu.SMEM),
                  pl.BlockSpec((tm, w), lambda i: (i, 0)),
                  pl.BlockSpec(memory_space=pl.ANY)],
        out_specs=pl.BlockSpec(memory_space=pl.ANY),
        out_shape=jax.ShapeDtypeStruct((n_rows, w), x.dtype),
        scratch_shapes=[pltpu.SemaphoreType.DMA],
        input_output_aliases={2: 0},
        compiler_params=_cparams(("arbitrary",)),
        name="moe_dispatch_rows",
    )(pos_t, x, jnp.zeros((n_rows, w), x.dtype))


def _combine_kernel(yg_ref, gate_ref, r_ref, g_ref, b_ref, o_ref):
    acc = r_ref[...]
    gate = gate_ref[...]
    for j in range(TOPK):
        acc = acc + yg_ref[j].astype(F32) * gate[:, j:j + 1]
    mu = jnp.mean(acc, axis=-1, keepdims=True)
    yc = acc - mu
    var = jnp.mean(yc * yc, axis=-1, keepdims=True)
    o_ref[...] = yc * lax.rsqrt(var + LN_EPS) * g_ref[...] + b_ref[...]


def combine_ln(yg, gate, r, g, b, tm):
    t = r.shape[0]
    tm = min(tm, t)
    return pl.pallas_call(
        _combine_kernel,
        grid=(t // tm,),
        in_specs=[pl.BlockSpec((TOPK, tm, D), lambda i: (0, i, 0)), pl.BlockSpec((tm, LANES), lambda i: (i, 0)),
                  pl.BlockSpec((tm, D), lambda i: (i, 0)), pl.BlockSpec((1, D), lambda i: (0, 0)),
                  pl.BlockSpec((1, D), lambda i: (0, 0))],
        out_specs=pl.BlockSpec((tm, D), lambda i: (i, 0)),
        out_shape=jax.ShapeDtypeStruct((t, D), F32),
        compiler_params=_cparams(("parallel",)),
        name="moe_combine_ln",
    )(yg, gate, r, g.reshape(1, D), b.reshape(1, D))


def moe_finish(r, x1p, logits, lp, bm):
    t = r.shape[0]
    idx_t, gate, sel = _route(logits, lp['router_bias'])
    block_expert, n_used, pos_t = _dispatch(idx_t, sel, bm)
    xb = scatter_rows(x1p, pos_t, (t * TOPK // bm + NE) * bm)
    yb = moe_experts(xb, block_expert, n_used, lp['w_exp_gate'], lp['w_exp_up'], lp['w_exp_down'], bm, BF)
    yg = yb[pos_t.reshape(TOPK * t)].reshape(TOPK, t, D)
    return combine_ln(yg, gate, r, lp['ln2_g'], lp['ln2_b'], 256)


def _rms(x, g):
    return x * lax.rsqrt(jnp.mean(x * x, -1, keepdims=True) + RMS_EPS) * g


def _prep_layer(l, w_in, p):
    wi = w_in[l]
    small = jnp.concatenate([wi[:, R_DT:R_NQKV], wi[:, R_NB:R_GATE], jnp.zeros((D, LANES - 32), wi.dtype)], axis=1)
    lp = {k: v[l] for k, v in p.items()}
    lp.update(
        w_qkv=wi[:, R_Q:R_Z].astype(BF),
        w_zg=jnp.concatenate([wi[:, R_Z:R_XBC], wi[:, R_NZ:R_NB], wi[:, R_GATE:]], axis=1).astype(BF),
        w_xbc=wi[:, R_XBC:R_DT].astype(BF),
        w_nqkv=wi[:, R_NQKV:R_NZ].astype(BF),
        w_small=small.astype(BF),
        w_router_p=jnp.concatenate([lp['w_router'], jnp.zeros((D, LANES - NE), F32)], axis=1).astype(BF),
    )
    for k in ('w_attn_o', 'w_ssd_o', 'w_dn_o', 'w_out', 'w_sh_gate', 'w_sh_up', 'w_sh_down',
              'w_exp_gate', 'w_exp_up', 'w_exp_down'):
        lp[k] = lp[k].astype(BF)
    return lp


def _merge_weights(lp):
    return [lp['w_attn_o'], lp['w_ssd_o'], lp['w_dn_o'], lp['w_out'], lp['ln1_g'].reshape(1, D),
            lp['ln1_b'].reshape(1, D), lp['w_router_p'], lp['w_sh_gate'], lp['w_sh_up'], lp['w_sh_down']]


def prompt_layer(x, lp, layer_idx, tabs):
    lam_init = 0.8 - 0.6 * math.exp(-0.3 * layer_idx)
    q, kf, kb, vf, vb = qkv_project(x, lp['w_qkv'], tabs, 512)
    zg = matmul(x, lp['w_zg'], BF, tm=PROJ_ROWS)
    xbc = matmul(x, lp['w_xbc'], BF, tm=PROJ_ROWS)
    nqkv = matmul(x, lp['w_nqkv'], BF, tm=PROJ_ROWS)
    small = matmul(x, lp['w_small'], F32, tm=PROJ_ROWS)
    ao = flash_diff_attention(q, kb, vb, lp['attn_lambda'], lp['attn_subln'], lam_init, NB, SEQ, ATT_BLK)
    so, ssd_h, ssd_tail = ssd_prompt(xbc, zg, small, lp, NB, SEQ, SSD_CHUNK, 0)
    no, dn_s, dn_tail = gdn_prompt_staged(nqkv.reshape(NB, SEQ, NCONV), zg.reshape(NB, SEQ, -1),
                                          small.reshape(NB, SEQ, LANES), lp, NB, SEQ, GDN_CHUNK, 1)
    no = no.reshape(NB * SEQ, NH * NV)
    x1p, r, logits = merge_post(x, ao, so, no, zg, 2, _merge_weights(lp), 512)
    x2 = moe_finish(r, x1p, logits, lp, MOE_BLK_PROMPT)
    return (x2, kf.reshape(NB, SEQ, KVH, 2 * HD), vf.reshape(NB, SEQ, KVH, 2 * HD),
            ssd_h.reshape(NB, SH, SP, SN), ssd_tail, dn_s, dn_tail)


def _conv_step(hist, new, w, b=None):
    xp = jnp.concatenate([hist, new[:, None, :]], axis=1)
    y = jnp.sum(xp * w[None], axis=1)
    if b is not None:
        y = y + b
    return jax.nn.silu(y), xp[:, 1:]


def decode_layer(x, lp, layer_idx, tabs, cache_k, cache_v, page_table, ssd_h0, ssd_hist, dn_s0, dn_hist):
    lam_init = 0.8 - 0.6 * math.exp(-0.3 * layer_idx)
    q, kf, _, vf, _ = qkv_project(x, lp['w_qkv'], tabs, DB)
    zg = matmul(x, lp['w_zg'], F32, tm=DB)
    xbc_raw = matmul(x, lp['w_xbc'], F32, tm=DB)
    nqkv_raw = matmul(x, lp['w_nqkv'], F32, tm=DB)
    small = matmul(x, lp['w_small'], F32, tm=DB)
    qh = q.reshape(DB, KVH, 2, 2, HD)
    q16 = jnp.einsum('bhgcd,ce->bhgced', qh, jnp.eye(2, dtype=BF)).reshape(DB, 16, 2 * HD)
    k16 = jnp.repeat(kf.reshape(DB, KVH, 2 * HD), 4, axis=1)
    v16 = jnp.repeat(vf.reshape(DB, KVH, 2 * HD), 4, axis=1)
    oc = paged_attention(q16, k16, v16, cache_k, cache_v, page_table, layer_idx, PAGES_PER_STEP)
    oc = oc.reshape(DB, AH, 2, 2 * HD)
    lq = lp['attn_lambda']
    lam = jnp.exp(jnp.sum(lq[0] * lq[1])) - jnp.exp(jnp.sum(lq[2] * lq[3])) + lam_init
    o = oc[:, :, 0] - lam * oc[:, :, 1]
    ao = (_rms(o, lp['attn_subln']) * (1.0 - lam_init)).reshape(DB, AH * 2 * HD)
    xbc, ssd_tail = _conv_step(ssd_hist, xbc_raw, lp['ssd_conv_w'], lp['ssd_conv_b'])
    xs = xbc[:, :SI].reshape(DB, SH, SP)
    bm = jnp.repeat(xbc[:, SI:SI + SG * SN].reshape(DB, SG, SN), SH // SG, axis=1)
    cm = jnp.repeat(xbc[:, SI + SG * SN:].reshape(DB, SG, SN), SH // SG, axis=1)
    dt = jax.nn.softplus(small[:, L_DT:L_DT + SH] + lp['ssd_dt_bias'])
    a_coef = -jnp.exp(lp['ssd_a_log'])
    ssd_h = ssd_h0 * jnp.exp(dt * a_coef)[..., None, None] + (xs * dt[..., None])[..., :, None] * bm[..., None, :]
    ys = jnp.sum(ssd_h * cm[..., None, :], -1) + lp['ssd_d'][:, None] * xs
    gw = SI // SG
    ys = ys.reshape(DB, SG, gw) * jax.nn.silu(zg[:, :SI]).reshape(DB, SG, gw)
    so = _rms(ys, lp['ssd_norm'].reshape(SG, gw)).reshape(DB, SI)
    qkv, dn_tail = _conv_step(dn_hist, nqkv_raw, lp['dn_conv_w'])
    nq = qkv[:, :NKW].reshape(DB, NH, NK)
    nk = qkv[:, NKW:2 * NKW].reshape(DB, NH, NK)
    nv = qkv[:, 2 * NKW:].reshape(DB, NH, NV)
    nq = nq * lax.rsqrt(jnp.sum(nq * nq, -1, keepdims=True) + RMS_EPS) * (NK ** -0.5)
    nk = nk * lax.rsqrt(jnp.sum(nk * nk, -1, keepdims=True) + RMS_EPS)
    beta = jax.nn.sigmoid(small[:, L_NB:L_NB + NH])
    gl = -jnp.exp(lp['dn_a_log']) * jax.nn.softplus(small[:, L_NA:L_NA + NH] + lp['dn_dt_bias'])
    s = dn_s0 * jnp.exp(gl)[..., None, None]
    delta = (nv - jnp.sum(s * nk[..., :, None], -2)) * beta[..., None]
    dn_s = s + nk[..., :, None] * delta[..., None, :]
    on = jnp.sum(dn_s * nq[..., :, None], -2)
    no = (_rms(on, lp['dn_norm']) * jax.nn.silu(zg[:, SI:2 * SI].reshape(DB, NH, NV))).reshape(DB, NH * NV)
    x1p, r, logits = merge_post(x, ao, so, no, zg, 2, _merge_weights(lp), DB)
    x2 = moe_finish(r, x1p, logits, lp, MOE_BLK_DECODE)
    return (x2, kf.reshape(DB, 1, KVH, 2 * HD), vf.reshape(DB, 1, KVH, 2 * HD), ssd_h, ssd_tail, dn_s, dn_tail)


def kernel(x_prompt, x_sample, cache_k, cache_v, state_ssd, state_ssd_conv, state_dn, state_dn_conv, page_table,
           w_in, attn_lambda, attn_subln, w_attn_o, ssd_conv_w, ssd_conv_b, ssd_dt_bias, ssd_a_log, ssd_d, ssd_norm,
           w_ssd_o, dn_conv_w, dn_dt_bias, dn_a_log, dn_norm, w_dn_o, w_out, ln1_g, ln1_b, w_router, router_bias,
           w_exp_gate, w_exp_up, w_exp_down, w_sh_gate, w_sh_up, w_sh_down, ln2_g, ln2_b):
    params = dict(attn_lambda=attn_lambda, attn_subln=attn_subln, w_attn_o=w_attn_o, ssd_conv_w=ssd_conv_w,
                  ssd_conv_b=ssd_conv_b, ssd_dt_bias=ssd_dt_bias, ssd_a_log=ssd_a_log, ssd_d=ssd_d,
                  ssd_norm=ssd_norm, w_ssd_o=w_ssd_o, dn_conv_w=dn_conv_w, dn_dt_bias=dn_dt_bias,
                  dn_a_log=dn_a_log, dn_norm=dn_norm, w_dn_o=w_dn_o, w_out=w_out, ln1_g=ln1_g, ln1_b=ln1_b,
                  w_router=w_router, router_bias=router_bias, w_exp_gate=w_exp_gate, w_exp_up=w_exp_up,
                  w_exp_down=w_exp_down, w_sh_gate=w_sh_gate, w_sh_up=w_sh_up, w_sh_down=w_sh_down,
                  ln2_g=ln2_g, ln2_b=ln2_b)
    tabs_p = _rope_tables(jnp.arange(SEQ, dtype=jnp.int32))
    tabs_d = _rope_tables(jnp.full((DB,), PAST, dtype=jnp.int32))
    xp = x_prompt.reshape(NB * SEQ, D)
    xs = x_sample.reshape(DB, D)
    outs = [[] for _ in range(12)]
    for l in range(DEPTH):
        lp = _prep_layer(l, w_in, params)
        xp, kp, vp, hp, cp, sp, dp = prompt_layer(xp, lp, l, tabs_p)
        xs, ks, vs, hs, cs, ss, ds = decode_layer(xs, lp, l, tabs_d, cache_k, cache_v, page_table,
                                                  state_ssd[l], state_ssd_conv[l], state_dn[l], state_dn_conv[l])
        for lst, val in zip(outs, (kp, vp, ks, vs, hp, cp, hs, cs, sp, dp, ss, ds)):
            lst.append(val)
    return (xp.reshape(NB, SEQ, D), xs.reshape(DB, 1, D)) + tuple(jnp.stack(o) for o in outs)
```

```python
import functools
import math

import jax
import jax.numpy as jnp
from jax import lax
from jax.experimental import pallas as pl
from jax.experimental.pallas import tpu as pltpu

D = 1024
NB, SEQ = 4, 4096
DEPTH = 2
DB = 32
PAST = 16384
PAGE = 128
AH, KVH, HD = 8, 4, 64
ROT = 16
THETA = 500000.0
SH, SP, SI, SG, SN = 16, 64, 1024, 2, 128
SCONV = SI + 2 * SG * SN
NH, NK, NV = 8, 128, 128
NKW = 1024
NCONV = 3072
CW = 4
NE, TOPK, NGRP, TOPG = 64, 8, 8, 4
RSCALE = 2.5
ALPHA = (2 * DEPTH) ** 0.25
LN_EPS = 1e-5
RMS_EPS = 1e-6

R_Q, R_K, R_V, R_Z, R_XBC, R_DT, R_NQKV, R_NZ, R_NB, R_NA, R_GATE = (
    0, 1024, 1536, 2048, 3072, 4608, 4624, 7696, 8720, 8728, 8736)
L_DT, L_NB, L_NA = 0, 16, 24

LANES = 128
VMEM_LIMIT = 56 * 1024 * 1024
SSD_CHUNK = 128
GDN_CHUNK = 64
ATT_BLK = 512
FLASH_ROWS = 32
PAGES_PER_STEP = 16
ROUTE_ROWS = 1024
PROJ_ROWS = 2048
MOE_BLK_PROMPT = 512
MOE_BLK_DECODE = 16

BF = jnp.bfloat16
F32 = jnp.float32
PACK_HI = -65536


def _cparams(sem):
    return pltpu.CompilerParams(dimension_semantics=sem, vmem_limit_bytes=VMEM_LIMIT)


def _dot(a, b):
    return jnp.dot(a, b, preferred_element_type=F32)


def _dot_nt(a, b):
    return lax.dot_general(a, b, (((1,), (1,)), ((), ())), preferred_element_type=F32)


def _dot_tn(a, b):
    return lax.dot_general(a, b, (((0,), (0,)), ((), ())), preferred_element_type=F32)


def _split2(x):
    hi = x.astype(BF)
    lo = (x - hi.astype(F32)).astype(BF)
    return hi, lo


def _split3(x):
    hi = x.astype(BF)
    r = x - hi.astype(F32)
    mid = r.astype(BF)
    lo = (r - mid.astype(F32)).astype(BF)
    return hi, mid, lo


def _dot_sel(sel, x):
    hi, mid, lo = _split3(x)
    return _dot(sel, hi) + _dot(sel, mid) + _dot(sel, lo)


def _dot_x_sel(x, sel):
    hi, mid, lo = _split3(x)
    return _dot(hi, sel) + _dot(mid, sel) + _dot(lo, sel)


def _dot3(a, b):
    ah, al = _split2(a)
    bh, bl = _split2(b)
    return _dot(ah, bh) + _dot(ah, bl) + _dot(al, bh)


def _sigmoid(x):
    return 1.0 / (1.0 + jnp.exp(-x))


def _silu(x):
    return x * _sigmoid(x)


def _softplus(x):
    return jnp.maximum(x, 0.0) + jnp.log(1.0 + jnp.exp(-jnp.abs(x)))


def _mm_kernel(x_ref, w_ref, o_ref, xs_ref):
    @pl.when(pl.program_id(1) == 0)
    def _():
        xs_ref[...] = x_ref[...].astype(BF)

    o_ref[...] = _dot(xs_ref[...], w_ref[...]).astype(o_ref.dtype)


def matmul(x, w, out_dtype, tm=512, tn=512):
    m, k = x.shape
    n = w.shape[1]
    tm, tn = min(tm, m), min(tn, n)
    assert m % tm == 0 and n % tn == 0, (m, n, tm, tn)
    return pl.pallas_call(
        _mm_kernel,
        grid=(m // tm, n // tn),
        in_specs=[pl.BlockSpec((tm, k), lambda i, j: (i, 0)),
                  pl.BlockSpec((k, tn), lambda i, j: (0, j))],
        out_specs=pl.BlockSpec((tm, tn), lambda i, j: (i, j)),
        out_shape=jax.ShapeDtypeStruct((m, n), out_dtype),
        scratch_shapes=[pltpu.VMEM((tm, k), BF)],
        compiler_params=_cparams(("parallel", "arbitrary")),
        name="proj_matmul",
    )(x, w)


def _rope_tables(pos):
    half = ROT // 2
    inv = THETA ** (-jnp.arange(half, dtype=F32) * 2.0 / ROT)
    ang = pos.astype(F32)[:, None] * inv
    cos, sin = jnp.cos(ang), jnp.sin(ang)
    t = pos.shape[0]
    one, zero, z8 = jnp.ones((t, HD - ROT), F32), jnp.zeros((t, HD - ROT), F32), jnp.zeros((t, half), F32)
    c = jnp.concatenate([cos, cos, one, cos, cos, one], 1)
    sa = jnp.concatenate([-sin, z8, zero, -sin, z8, zero], 1)
    sb = jnp.concatenate([z8, sin, zero, z8, sin, zero], 1)
    return c, sa, sb


def _qkv_kernel(x_ref, w_ref, c_ref, sa_ref, sb_ref, q_ref, kf_ref, kb_ref, vf_ref, vb_ref):
    acc = _dot(x_ref[...].astype(BF), w_ref[...])
    c, sa, sb = c_ref[...], sa_ref[...], sb_ref[...]

    def rot(xg):
        return xg * c + pltpu.roll(xg, LANES - ROT // 2, 1) * sa + pltpu.roll(xg, ROT // 2, 1) * sb

    for h in range(AH):
        sl = slice(h * LANES, (h + 1) * LANES)
        q_ref[:, sl] = (rot(acc[:, sl]) * (HD ** -0.5)).astype(BF)
    for h in range(KVH):
        sl = slice(h * LANES, (h + 1) * LANES)
        kr = rot(acc[:, R_K + h * LANES:R_K + (h + 1) * LANES])
        kf_ref[:, h, :] = kr
        kb_ref[:, sl] = kr.astype(BF)
    v = acc[:, R_V:R_Z]
    for h in range(KVH):
        vf_ref[:, h, :] = v[:, h * LANES:(h + 1) * LANES]
    vb_ref[...] = v.astype(BF)


def qkv_project(x, w, tabs, tm):
    m = x.shape[0]
    tm = min(tm, m)
    nt = tabs[0].shape[0] // tm
    kw = KVH * 2 * HD
    tab_spec = pl.BlockSpec((tm, LANES), lambda i: (i % nt, 0))
    return pl.pallas_call(
        _qkv_kernel,
        grid=(m // tm,),
        in_specs=[pl.BlockSpec((tm, D), lambda i: (i, 0)),
                  pl.BlockSpec((D, R_Z), lambda i: (0, 0)),
                  tab_spec, tab_spec, tab_spec],
        out_specs=[pl.BlockSpec((tm, AH * 2 * HD), lambda i: (i, 0)),
                   pl.BlockSpec((tm, KVH, 2 * HD), lambda i: (i, 0, 0)), pl.BlockSpec((tm, kw), lambda i: (i, 0)),
                   pl.BlockSpec((tm, KVH, 2 * HD), lambda i: (i, 0, 0)), pl.BlockSpec((tm, kw), lambda i: (i, 0))],
        out_shape=[jax.ShapeDtypeStruct((m, AH * 2 * HD), BF),
                   jax.ShapeDtypeStruct((m, KVH, 2 * HD), F32), jax.ShapeDtypeStruct((m, kw), BF),
                   jax.ShapeDtypeStruct((m, KVH, 2 * HD), F32), jax.ShapeDtypeStruct((m, kw), BF)],
        compiler_params=_cparams(("parallel",)),
        name="qkv_rope",
    )(x, w, *tabs)


def _lambda_value(lq, lam_init):
    a = jnp.sum(lq[0:1, :] * lq[1:2, :], axis=-1, keepdims=True)
    b = jnp.sum(lq[2:3, :] * lq[3:4, :], axis=-1, keepdims=True)
    return jnp.exp(a) - jnp.exp(b) + lam_init


def _flash_kernel(qi_ref, ki_ref, lq_ref, sub_ref, q_ref, k_ref, v_ref, o_ref, qs_ref, m_ref, acc_ref, s_ref, p_ref,
                  *, blk, lam_init):
    qi, ki = qi_ref[pl.program_id(2)], ki_ref[pl.program_id(2)]

    @pl.when(ki == 0)
    def _():
        q = q_ref[...]
        lane = lax.broadcasted_iota(jnp.int32, (blk, LANES), 1)
        for c in range(2):
            msk = (lane < HD) if c == 0 else (lane >= HD)
            for g in range(2):
                qs_ref[c, g * blk:(g + 1) * blk, :] = jnp.where(msk, q[:, g * LANES:(g + 1) * LANES], 0)
        m_ref[...] = jnp.full(m_ref.shape, -jnp.inf, F32)
        acc_ref[...] = jnp.zeros(acc_ref.shape, F32)

    def update(masked):
        k = k_ref[...]
        vext = jnp.concatenate([v_ref[...], jnp.ones((blk, LANES), BF)], axis=1)
        units = [(c, g) for g in range(2) for c in range(2)]
        for u, (c, g) in enumerate(units):
            s_ref[u] = _dot_nt(qs_ref[c, g * blk:(g + 1) * blk, :], k)
        if masked:
            keep = (lax.broadcasted_iota(jnp.int32, (blk, blk), 1) <= lax.broadcasted_iota(jnp.int32, (blk, blk), 0))
            diff = (lax.broadcasted_iota(jnp.int32, (FLASH_ROWS, blk), 1)
                    - lax.broadcasted_iota(jnp.int32, (FLASH_ROWS, blk), 0))
        for u, (c, g) in enumerate(units):
            rows = slice(g * blk, (g + 1) * blk)
            s_full = s_ref[u]
            if masked:
                s_full = jnp.where(keep, s_full, -jnp.inf)
            m_old = m_ref[c, rows]
            m_new = jnp.maximum(m_old, jnp.max(s_full, axis=-1, keepdims=True))
            alpha = jnp.exp(m_old - m_new)
            for r in range(0, blk, FLASH_ROWS):
                s = s_ref[u, r:r + FLASH_ROWS, :]
                if masked:
                    s = jnp.where(diff <= r, s, -jnp.inf)
                p_ref[u, r:r + FLASH_ROWS, :] = jnp.exp(s - m_new[r:r + FLASH_ROWS]).astype(BF)
            m_ref[c, rows] = m_new
            acc_ref[c, rows] = alpha * acc_ref[c, rows] + _dot(p_ref[u], vext)

    @pl.when(ki < qi)
    def _():
        update(False)

    @pl.when(ki == qi)
    def _():
        update(True)
        lam = _lambda_value(lq_ref[...], lam_init)
        a0, a1 = acc_ref[0], acc_ref[1]
        o = a0[:, :LANES] / a0[:, LANES:] - lam * (a1[:, :LANES] / a1[:, LANES:])
        o = o * lax.rsqrt(jnp.mean(o * o, axis=-1, keepdims=True) + RMS_EPS) * sub_ref[...] * (1.0 - lam_init)
        o_ref[:, 0:LANES] = o[0:blk].astype(BF)
        o_ref[:, LANES:2 * LANES] = o[blk:2 * blk].astype(BF)


def flash_diff_attention(q, k, v, lq, subln, lam_init, nb, seq, blk):
    nblk = seq // blk
    pairs = [(i, j) for i in range(nblk) for j in range(i + 1)]
    qi_arr = jnp.asarray([p[0] for p in pairs], jnp.int32)
    ki_arr = jnp.asarray([p[1] for p in pairs], jnp.int32)
    kern = functools.partial(_flash_kernel, blk=blk, lam_init=lam_init)
    grid_spec = pltpu.PrefetchScalarGridSpec(
        num_scalar_prefetch=2,
        grid=(nb, KVH, len(pairs)),
        in_specs=[pl.BlockSpec((4, HD), lambda b, h, s, qa, ka: (0, 0)),
                  pl.BlockSpec((1, LANES), lambda b, h, s, qa, ka: (0, 0)),
                  pl.BlockSpec((blk, 2 * LANES), lambda b, h, s, qa, ka: (b * nblk + qa[s], h)),
                  pl.BlockSpec((blk, LANES), lambda b, h, s, qa, ka: (b * nblk + ka[s], h)),
                  pl.BlockSpec((blk, LANES), lambda b, h, s, qa, ka: (b * nblk + ka[s], h))],
        out_specs=pl.BlockSpec((blk, 2 * LANES), lambda b, h, s, qa, ka: (b * nblk + qa[s], h)),
        scratch_shapes=[pltpu.VMEM((2, 2 * blk, LANES), BF), pltpu.VMEM((2, 2 * blk, 1), F32),
                        pltpu.VMEM((2, 2 * blk, 2 * LANES), F32),
                        pltpu.VMEM((4, blk, blk), F32), pltpu.VMEM((4, blk, blk), BF)],
    )
    return pl.pallas_call(
        kern,
        grid_spec=grid_spec,
        out_shape=jax.ShapeDtypeStruct((nb * seq, AH * 2 * HD), BF),
        compiler_params=_cparams(("parallel", "parallel", "arbitrary")),
        name="flash_diff_attn",
    )(qi_arr, ki_arr, lq, subln.reshape(1, LANES), q, k, v)


def _paged_kernel(pt_ref, q_ref, ks_ref, vs_ref, *refs, pps):
    kp, vp = refs[:pps], refs[pps:2 * pps]
    o_ref, m_ref, l_ref, acc_ref = refs[2 * pps:]
    j = pl.program_id(1)
    q = q_ref[...]
    pw = PAGE * KVH

    @pl.when(j == 0)
    def _():
        ks = ks_ref[...].astype(BF).astype(F32)
        m_ref[...] = jnp.sum(q.astype(F32) * ks, axis=-1, keepdims=True)
        l_ref[...] = jnp.ones(l_ref.shape, F32)
        acc_ref[...] = vs_ref[...].astype(BF).astype(F32)

    row_head = lax.broadcasted_iota(jnp.int32, (16, pw), 0) // 4
    col_head = lax.broadcasted_iota(jnp.int32, (16, pw), 1) % KVH
    own = row_head == col_head
    s = jnp.concatenate([jnp.where(own, _dot_nt(q, kp[i][...].astype(BF)), -jnp.inf) for i in range(pps)], axis=1)
    m_old = m_ref[...]
    m_new = jnp.maximum(m_old, jnp.max(s, axis=-1, keepdims=True))
    alpha = jnp.exp(m_old - m_new)
    p = jnp.exp(s - m_new)
    l_ref[...] = alpha * l_ref[...] + jnp.sum(p, axis=-1, keepdims=True)
    pv = _dot(p[:, 0:pw].astype(BF), vp[0][...].astype(BF))
    for i in range(1, pps):
        pv = pv + _dot(p[:, i * pw:(i + 1) * pw].astype(BF), vp[i][...].astype(BF))
    acc_ref[...] = alpha * acc_ref[...] + pv
    m_ref[...] = m_new

    @pl.when(j == pl.num_programs(1) - 1)
    def _():
        o_ref[...] = acc_ref[...] / l_ref[...]


def paged_attention(q16, k16, v16, cache_k, cache_v, page_table, layer, pps):
    nseq, npg = page_table.shape
    pw = PAGE * KVH
    ck = cache_k.reshape(cache_k.shape[0], cache_k.shape[1], pw, 2 * HD)
    cv = cache_v.reshape(cache_v.shape[0], cache_v.shape[1], pw, 2 * HD)

    def page_spec(i):
        return pl.BlockSpec((None, None, pw, 2 * HD), lambda b, j, pt: (layer, pt[b, j * pps + i], 0, 0))

    row = pl.BlockSpec((None, 16, 2 * HD), lambda b, j, pt: (b, 0, 0))
    grid_spec = pltpu.PrefetchScalarGridSpec(
        num_scalar_prefetch=1,
        grid=(nseq, npg // pps),
        in_specs=[row, row, row] + [page_spec(i) for i in range(pps)] + [page_spec(i) for i in range(pps)],
        out_specs=row,
        scratch_shapes=[pltpu.VMEM((16, 1), F32), pltpu.VMEM((16, 1), F32), pltpu.VMEM((16, 2 * HD), F32)],
    )
    return pl.pallas_call(
        functools.partial(_paged_kernel, pps=pps),
        grid_spec=grid_spec,
        out_shape=jax.ShapeDtypeStruct((nseq, 16, 2 * HD), F32),
        compiler_params=_cparams(("parallel", "arbitrary")),
        name="paged_diff_attn",
    )(page_table, q16, k16, v16, *([ck] * pps), *([cv] * pps))


def _ssd_kernel(xbc_ref, z_ref, sm_ref, cw_ref, cb_ref, dtb_ref, ac_ref, dx_ref, nw_ref, e_ref,
                y_ref, st_ref, tail_ref, xbuf, s_ref, *, chunk):
    n = pl.program_id(1)
    L = chunk

    @pl.when(n == 0)
    def _():
        xbuf[0:8, :] = jnp.zeros((8, SCONV), F32)
        s_ref[...] = jnp.zeros(s_ref.shape, F32)

    xbuf[8:8 + L, :] = xbc_ref[...].astype(F32)
    y = cb_ref[...] + cw_ref[0:1, :] * xbuf[pl.ds(5, L), :]
    for t in range(1, CW):
        y = y + cw_ref[t:t + 1, :] * xbuf[pl.ds(5 + t, L), :]
    xbc = _silu(y)
    xs, bm, cm = xbc[:, :SI], xbc[:, SI:SI + SG * SN], xbc[:, SI + SG * SN:]

    dt = _softplus(sm_ref[...] + dtb_ref[...])
    a = dt * ac_ref[...]
    row = lax.broadcasted_iota(jnp.int32, (L, L), 0)
    col = lax.broadcasted_iota(jnp.int32, (L, L), 1)
    causal = row >= col
    acum = _dot_sel(causal.astype(BF), a)
    acum_t = acum.T
    alast = acum[L - 1:L, :]
    e = e_ref[...]
    dt_e = _dot_x_sel(dt, e)
    ea_e = _dot_x_sel(jnp.exp(acum), e)
    dend_e = _dot_x_sel(jnp.exp(alast - acum), e)
    xdt = xs * dt_e
    xd = (xdt * dend_e).astype(BF)
    lane = lax.broadcasted_iota(jnp.int32, (L, LANES), 1)
    gw = SI // SG
    hpg = SH // SG
    y_parts = []
    for g in range(SG):
        cg = cm[:, g * SN:(g + 1) * SN].astype(BF)
        bg = bm[:, g * SN:(g + 1) * SN].astype(BF)
        s_old = s_ref[g * gw:(g + 1) * gw, :]
        y_inter = _dot_nt(cg, s_old.astype(BF))
        cb = _dot_nt(cg, bg)
        pairs = []
        for jp in range(hpg // 2):
            h0 = g * hpg + 2 * jp
            xp = xdt[:, h0 * SP:h0 * SP + LANES]
            acc = None
            for d in range(2):
                h = h0 + d
                seg = jnp.where(causal, jnp.exp(acum[:, h:h + 1] - acum_t[h:h + 1, :]), 0.0)
                mh = (cb * seg).astype(BF)
                xm = jnp.where((lane < SP) if d == 0 else (lane >= SP), xp, 0.0).astype(BF)
                t = _dot(mh, xm)
                acc = t if acc is None else acc + t
            pairs.append(acc)
        y_intra = jnp.concatenate(pairs, axis=1)
        y_parts.append(y_intra + y_inter * ea_e[:, g * gw:(g + 1) * gw])
        upd = _dot_tn(xd[:, g * gw:(g + 1) * gw], bg)
        for hh in range(hpg):
            h = g * hpg + hh
            r0 = h * SP
            s_ref[r0:r0 + SP, :] = (s_old[hh * SP:(hh + 1) * SP, :] * jnp.exp(alast[:, h:h + 1])
                                    + upd[hh * SP:(hh + 1) * SP, :])
    yv = jnp.concatenate(y_parts, axis=1) + dx_ref[...] * xs
    yv = yv * _silu(z_ref[...].astype(F32))
    outs = []
    for g in range(SG):
        yg = yv[:, g * gw:(g + 1) * gw]
        outs.append(yg * lax.rsqrt(jnp.mean(yg * yg, axis=-1, keepdims=True) + RMS_EPS) * nw_ref[:, g * gw:(g + 1) * gw])
    y_ref[...] = jnp.concatenate(outs, axis=1).astype(BF)
    xbuf[0:8, :] = xbuf[L:L + 8, :]

    @pl.when(n == pl.num_programs(1) - 1)
    def _():
        st_ref[...] = s_ref[...]
        tail_ref[...] = xbuf[pl.ds(L + 5, CW - 1), :]


def ssd_prompt(xbc, zz, small, lp, nb, seq, chunk, z_col):
    nc = seq // chunk
    heads = jnp.arange(LANES)[:, None]
    e = (heads == (jnp.arange(SI)[None, :] // SP)).astype(BF)

    def pad_lanes(v, off):
        return jnp.zeros((1, LANES), F32).at[0, off:off + v.shape[0]].set(v)

    full = lambda shape: pl.BlockSpec(shape, lambda b, n: (0,) * len(shape))
    return pl.pallas_call(
        functools.partial(_ssd_kernel, chunk=chunk),
        grid=(nb, nc),
        in_specs=[pl.BlockSpec((chunk, SCONV), lambda b, n: (b * nc + n, 0)),
                  pl.BlockSpec((chunk, SI), lambda b, n: (b * nc + n, z_col)),
                  pl.BlockSpec((chunk, LANES), lambda b, n: (b * nc + n, 0)),
                  full((CW, SCONV)), full((1, SCONV)), full((1, LANES)), full((1, LANES)),
                  full((1, SI)), full((1, SI)), full((LANES, SI))],
        out_specs=[pl.BlockSpec((chunk, SI), lambda b, n: (b * nc + n, 0)),
                   pl.BlockSpec((None, SH * SP, SN), lambda b, n: (b, 0, 0)),
                   pl.BlockSpec((None, CW - 1, SCONV), lambda b, n: (b, 0, 0))],
        out_shape=[jax.ShapeDtypeStruct((nb * seq, SI), BF),
                   jax.ShapeDtypeStruct((nb, SH * SP, SN), F32),
                   jax.ShapeDtypeStruct((nb, CW - 1, SCONV), F32)],
        scratch_shapes=[pltpu.VMEM((chunk + 8, SCONV), F32), pltpu.VMEM((SH * SP, SN), F32)],
        compiler_params=_cparams(("parallel", "arbitrary")),
        name="ssd_chunk_scan",
    )(xbc, zz, small, lp['ssd_conv_w'], lp['ssd_conv_b'].reshape(1, SCONV),
      pad_lanes(lp['ssd_dt_bias'], L_DT), pad_lanes(-jnp.exp(lp['ssd_a_log']), L_DT),
      jnp.repeat(lp['ssd_d'], SP).reshape(1, SI), lp['ssd_norm'].reshape(1, SI), e)


def _gdn_stage_kernel(qkv_ref, z_ref, sm_ref, cw_ref, dtb_ref, na_ref, nw_ref,
                      o_ref, st_ref, tail_ref, xbuf, s_ref, *, chunk, nb):
    n = pl.program_id(0)
    C = chunk

    @pl.when(n == 0)
    def _():
        xbuf[:, 0:8, :] = jnp.zeros((nb, 8, NCONV), F32)
        s_ref[...] = jnp.zeros(s_ref.shape, F32)

    r128 = lax.broadcasted_iota(jnp.int32, (LANES, LANES), 0)
    c128 = lax.broadcasted_iota(jnp.int32, (LANES, LANES), 1)
    tri128 = (r128 >= c128).astype(BF)
    row = lax.broadcasted_iota(jnp.int32, (C, C), 0)
    col = lax.broadcasted_iota(jnp.int32, (C, C), 1)
    eye = (row == col).astype(F32)
    ch = []
    for b in range(nb):
        xbuf[b, 8:8 + C, :] = qkv_ref[b].astype(F32)
        y = cw_ref[0:1, :] * xbuf[b, pl.ds(5, C), :]
        for t in range(1, CW):
            y = y + cw_ref[t:t + 1, :] * xbuf[b, pl.ds(5 + t, C), :]
        qkv = _silu(y)
        sm = sm_ref[b]
        beta = _sigmoid(sm)
        g = na_ref[...] * _softplus(sm + dtb_ref[...])
        gc = _dot_sel(tri128, jnp.concatenate([g, jnp.zeros((LANES - C, LANES), F32)], axis=0))
        gc_t = gc.T
        for h in range(NH):
            qh = qkv[:, h * NK:(h + 1) * NK]
            kh = qkv[:, NKW + h * NK:NKW + (h + 1) * NK]
            vh = qkv[:, 2 * NKW + h * NV:2 * NKW + (h + 1) * NV]
            qh = qh * lax.rsqrt(jnp.sum(qh * qh, axis=-1, keepdims=True) + RMS_EPS) * (NK ** -0.5)
            kh = kh * lax.rsqrt(jnp.sum(kh * kh, axis=-1, keepdims=True) + RMS_EPS)
            la = L_NA + h
            gcol = gc[0:C, la:la + 1]
            grow = gc_t[la:la + 1, 0:C]
            glast = gc[C - 1:C, la:la + 1]
            bh = beta[:, L_NB + h:L_NB + h + 1]
            kb = kh * bh
            eg = jnp.exp(gcol)
            ch.append(dict(
                b=b, h=h, glast=glast,
                dmat=jnp.exp(jnp.where(row >= col, gcol - grow, -1e30)),
                kbb=kb.astype(BF), khb=kh.astype(BF), qhb=qh.astype(BF),
                rhs=jnp.concatenate([vh * bh, kb * eg], axis=1),
                qg=(qh * eg).astype(BF), kd=(kh * jnp.exp(glast - gcol)).astype(BF)))
    for c in ch:
        c['x'] = -jnp.where(row > col, _dot_nt(c['kbb'], c['khb']) * c['dmat'], 0.0)
        c['qk'] = (_dot_nt(c['qhb'], c['khb']) * c['dmat']).astype(BF)
    for c in ch:
        c['p'] = _dot3(c['x'], c['x'])
    span = 2
    first = True
    while span < C:
        for c in ch:
            tm = (eye + c['x']) if first else c['t']
            if first:
                c['t'] = tm + _dot3(tm, c['p'])
            else:
                c['t'] = tm + _dot(tm.astype(BF), c['p'].astype(BF))
            if span * 2 < C:
                pb = c['p'].astype(BF)
                c['p'] = _dot(pb, pb)
        first = False
        span *= 2
    for c in ch:
        c['sol'] = _dot3(c['t'], c['rhs'])
    for c in ch:
        s_old = s_ref[c['b'], c['h']]
        sb = s_old.astype(BF)
        c['s_old'] = s_old
        c['vn'] = (c['sol'][:, :NV] - _dot(c['sol'][:, NV:].astype(BF), sb)).astype(BF)
        c['oq'] = _dot(c['qg'], sb)
    for c in ch:
        b, h = c['b'], c['h']
        o = c['oq'] + _dot(c['qk'], c['vn'])
        s_ref[b, h] = c['s_old'] * jnp.exp(c['glast']) + _dot_tn(c['kd'], c['vn'])
        o = o * lax.rsqrt(jnp.mean(o * o, axis=-1, keepdims=True) + RMS_EPS) * nw_ref[...]
        sl = slice(h * NV, (h + 1) * NV)
        o_ref[b, :, sl] = (o * _silu(z_ref[b, :, sl].astype(F32))).astype(BF)
    xbuf[:, 0:8, :] = xbuf[:, C:C + 8, :]

    @pl.when(n == pl.num_programs(0) - 1)
    def _():
        st_ref[...] = s_ref[...]
        tail_ref[...] = xbuf[:, pl.ds(C + 5, CW - 1), :]


def gdn_prompt_staged(qkv, zz, small, lp, nb, seq, chunk, z_col):
    nc = seq // chunk

    def pad_lanes(v, off):
        return jnp.zeros((1, LANES), F32).at[0, off:off + v.shape[0]].set(v)

    full = lambda shape: pl.BlockSpec(shape, lambda n: (0,) * len(shape))
    return pl.pallas_call(
        functools.partial(_gdn_stage_kernel, chunk=chunk, nb=nb),
        grid=(nc,),
        in_specs=[pl.BlockSpec((nb, chunk, NCONV), lambda n: (0, n, 0)),
                  pl.BlockSpec((nb, chunk, NH * NV), lambda n: (0, n, z_col)),
                  pl.BlockSpec((nb, chunk, LANES), lambda n: (0, n, 0)),
                  full((CW, NCONV)), full((1, LANES)), full((1, LANES)), full((1, NV))],
        out_specs=[pl.BlockSpec((nb, chunk, NH * NV), lambda n: (0, n, 0)),
                   full((nb, NH, NK, NV)), full((nb, CW - 1, NCONV))],
        out_shape=[jax.ShapeDtypeStruct((nb, seq, NH * NV), BF),
                   jax.ShapeDtypeStruct((nb, NH, NK, NV), F32),
                   jax.ShapeDtypeStruct((nb, CW - 1, NCONV), F32)],
        scratch_shapes=[pltpu.VMEM((nb, chunk + 8, NCONV), F32), pltpu.VMEM((nb, NH, NK, NV), F32)],
        compiler_params=_cparams(("arbitrary",)),
        name="gdn_chunk_scan",
    )(qkv, zz, small, lp['dn_conv_w'], pad_lanes(lp['dn_dt_bias'], L_NA),
      pad_lanes(-jnp.exp(lp['dn_a_log']), L_NA), lp['dn_norm'].reshape(1, NV))


def _merge_kernel(x_ref, ao_ref, so_ref, no_ref, g0_ref, g1_ref, g2_ref, wa_ref, ws_ref, wn_ref, wo_ref,
                  lg_ref, lb_ref, wr_ref, wsg_ref, wsu_ref, wsd_ref, x1p_ref, r_ref, lgt_ref):
    def gate(ref):
        return _sigmoid(ref[...].astype(F32))

    mixed = (gate(g0_ref) * _dot(ao_ref[...].astype(BF), wa_ref[...])
             + gate(g1_ref) * _dot(so_ref[...].astype(BF), ws_ref[...])
             + gate(g2_ref) * _dot(no_ref[...].astype(BF), wn_ref[...]))
    yv = ALPHA * x_ref[...] + _dot(mixed.astype(BF), wo_ref[...])
    mu = jnp.mean(yv, axis=-1, keepdims=True)
    yc = yv - mu
    var = jnp.mean(yc * yc, axis=-1, keepdims=True)
    x1 = yc * lax.rsqrt(var + LN_EPS) * lg_ref[...] + lb_ref[...]
    x1b = x1.astype(BF)
    bits = lax.bitcast_convert_type(x1b.astype(F32), jnp.int32)
    x1p_ref[...] = (bits[:, D // 2:] & PACK_HI) | lax.shift_right_logical(bits[:, :D // 2], 16)
    lgt_ref[...] = _dot(x1b, wr_ref[...])
    hsh = _silu(_dot(x1b, wsg_ref[...])) * _dot(x1b, wsu_ref[...])
    r_ref[...] = ALPHA * x1 + _dot(hsh.astype(BF), wsd_ref[...])


def merge_post(x, ao, so, no, gates, gate_col, wts, tm):
    m = x.shape[0]
    tm = min(tm, m)
    rowb = lambda c: pl.BlockSpec((tm, D), lambda i, c=c: (i, c))
    full = lambda a: pl.BlockSpec(a.shape, lambda i: (0,) * a.ndim)
    return pl.pallas_call(
        _merge_kernel,
        grid=(m // tm,),
        in_specs=[rowb(0), rowb(0), rowb(0), rowb(0), rowb(gate_col), rowb(gate_col + 1), rowb(gate_col + 2)]
        + [full(a) for a in wts],
        out_specs=[pl.BlockSpec((tm, D // 2), lambda i: (i, 0)), rowb(0), pl.BlockSpec((tm, LANES), lambda i: (i, 0))],
        out_shape=[jax.ShapeDtypeStruct((m, D // 2), jnp.int32), jax.ShapeDtypeStruct((m, D), F32),
                   jax.ShapeDtypeStruct((m, LANES), F32)],
        compiler_params=_cparams(("parallel",)),
        name="merge_ln_router_shared",
    )(x, ao, so, no, gates, gates, gates, *wts)


def _moe_kernel(be_ref, nu_ref, x_ref, wg_ref, wu_ref, wd_ref, o_ref):
    i = pl.program_id(0)

    @pl.when(i < nu_ref[0])
    def _():
        words = x_ref[...]
        lo = lax.bitcast_convert_type(lax.shift_left(words, 16), F32).astype(BF)
        hi = lax.bitcast_convert_type(words & PACK_HI, F32).astype(BF)
        h2 = D // 2
        gate = _dot(lo, wg_ref[0:h2, :]) + _dot(hi, wg_ref[h2:D, :])
        up = _dot(lo, wu_ref[0:h2, :]) + _dot(hi, wu_ref[h2:D, :])
        hid = _silu(gate) * up
        o_ref[...] = _dot(hid.astype(BF), wd_ref[...]).astype(o_ref.dtype)

    @pl.when(i >= nu_ref[0])
    def _():
        o_ref[...] = jnp.zeros(o_ref.shape, o_ref.dtype)


def moe_experts(xb, block_expert, n_used, wg, wu, wd, bm, out_dtype):
    n_rows = xb.shape[0]
    ff = wg.shape[2]
    grid_spec = pltpu.PrefetchScalarGridSpec(
        num_scalar_prefetch=2,
        grid=(n_rows // bm,),
        in_specs=[pl.BlockSpec((bm, D // 2), lambda i, be, nu: (i, 0)),
                  pl.BlockSpec((None, D, ff), lambda i, be, nu: (be[i], 0, 0)),
                  pl.BlockSpec((None, D, ff), lambda i, be, nu: (be[i], 0, 0)),
                  pl.BlockSpec((None, ff, D), lambda i, be, nu: (be[i], 0, 0))],
        out_specs=pl.BlockSpec((bm, D), lambda i, be, nu: (i, 0)),
    )
    return pl.pallas_call(
        _moe_kernel,
        grid_spec=grid_spec,
        out_shape=jax.ShapeDtypeStruct((n_rows, D), out_dtype),
        compiler_params=_cparams(("arbitrary",)),
        name="moe_grouped_ffn",
    )(block_expert, n_used, xb, wg, wu, wd)


def _route_kernel(lg_ref, bias_ref, idx_ref, gate_ref, sel_ref):
    lg = lg_ref[...]
    shape = lg.shape
    lane = lax.broadcasted_iota(jnp.int32, shape, 1)
    lanef = lane.astype(F32)
    neg = -jnp.inf
    scores = _sigmoid(lg)
    biased = jnp.where(lane < NE, scores + bias_ref[...], neg)

    def first_max(x):
        m = jnp.max(x, axis=-1, keepdims=True)
        first = jnp.min(jnp.where(x == m, lanef, 2.0 * LANES), axis=-1, keepdims=True)
        return m, first

    gsz = NE // NGRP
    in_group = [(lane >= g * gsz) & (lane < (g + 1) * gsz) for g in range(NGRP)]
    gscore = []
    for g in range(NGRP):
        xg = jnp.where(in_group[g], biased, neg)
        m1, first = first_max(xg)
        m2 = jnp.max(jnp.where(lanef == first, neg, xg), axis=-1, keepdims=True)
        gscore.append(m1 + m2)
    allowed = jnp.zeros(shape, jnp.bool_)
    for g in range(NGRP):
        rank = jnp.zeros_like(gscore[g])
        for g2 in range(NGRP):
            if g2 != g:
                ahead = (gscore[g2] > gscore[g]) | ((gscore[g2] == gscore[g]) & (g2 < g))
                rank = rank + jnp.where(ahead, 1.0, 0.0)
        allowed = allowed | ((rank < TOPG) & in_group[g])
    work = jnp.where(allowed, biased, neg)
    idx_out = jnp.zeros(shape, F32)
    gate_out = jnp.zeros(shape, F32)
    chosen = jnp.zeros(shape, jnp.bool_)
    for j in range(TOPK):
        _, first = first_max(work)
        sel = lanef == first
        sc = jnp.sum(jnp.where(sel, scores, 0.0), axis=-1, keepdims=True)
        idx_out = jnp.where(lane == j, first, idx_out)
        gate_out = jnp.where(lane == j, sc, gate_out)
        chosen = chosen | sel
        work = jnp.where(sel, neg, work)
    gate_ref[...] = gate_out / jnp.sum(gate_out, axis=-1, keepdims=True) * RSCALE
    sel_ref[...] = jnp.where(chosen, 1, 0).astype(jnp.int32)
    idx_ref[...] = idx_out.T[0:TOPK, :].astype(jnp.int32)


def _route(logits, router_bias):
    t = logits.shape[0]
    tm = min(ROUTE_ROWS, t)
    bias = jnp.concatenate([router_bias, jnp.zeros((LANES - NE,), F32)]).reshape(1, LANES)
    rows = pl.BlockSpec((tm, LANES), lambda i: (i, 0))
    return pl.pallas_call(
        _route_kernel,
        grid=(t // tm,),
        in_specs=[rows, pl.BlockSpec((1, LANES), lambda i: (0, 0))],
        out_specs=[pl.BlockSpec((TOPK, tm), lambda i: (0, i)), rows, rows],
        out_shape=[jax.ShapeDtypeStruct((TOPK, t), jnp.int32), jax.ShapeDtypeStruct((t, LANES), F32),
                   jax.ShapeDtypeStruct((t, LANES), jnp.int32)],
        compiler_params=_cparams(("parallel",)),
        name="moe_route_topk",
    )(logits, bias)


def _dispatch(idx_t, sel, bm):
    t = sel.shape[0]
    na = t * TOPK
    assert na % bm == 0
    cum = jnp.cumsum(sel, axis=0)
    counts = cum[-1]
    padded = (counts + bm - 1) // bm * bm
    pad_end = jnp.cumsum(padded)
    dest_t = ((pad_end - padded)[None, :] + cum - sel).T
    experts = jnp.arange(NE, dtype=jnp.int32)[None, :, None]
    pos_t = jnp.sum(jnp.where(idx_t[:, None, :] == experts, dest_t[None, :NE, :], 0), axis=1).astype(jnp.int32)
    n_blocks = na // bm + NE
    starts = jnp.arange(n_blocks, dtype=jnp.int32)[:, None] * bm
    block_expert = jnp.minimum(jnp.sum((pad_end[None, :NE] <= starts).astype(jnp.int32), axis=1), NE - 1)
    n_used = (pad_end[NE - 1:NE] // bm).astype(jnp.int32)
    return block_expert, n_used, pos_t


def _scatter_rows_kernel(pos_ref, x_ref, zeros_ref, o_ref, sem, *, tm):
    del zeros_ref

    def row_copy(t, j):
        return pltpu.make_async_copy(x_ref.at[pl.ds(t, 1), :], o_ref.at[pl.ds(pos_ref[j, t], 1), :], sem)

    def issue(t, carry):
        for j in range(TOPK):
            row_copy(t, j).start(priority=j % 2)
        return carry

    def drain(t, carry):
        for j in range(TOPK):
            row_copy(t, j).wait()
        return carry

    lax.fori_loop(0, tm, issue, 0)
    lax.fori_loop(0, tm, drain, 0)


def scatter_rows(x, pos_t, n_rows):
    t, w = x.shape
    tm = min(256, t)
    return pl.pallas_call(
        functools.partial(_scatter_rows_kernel, tm=tm),
        grid=(t // tm,),
        in_specs=[pl.BlockSpec((TOPK, tm), lambda i: (0, i), memory_space=pltpu.SMEM),
                  pl.BlockSpec((tm, w), lambda i: (i, 0)),
                  pl.BlockSpec(memory_space=pl.ANY)],
        out_specs=pl.BlockSpec(memory_space=pl.ANY),
        out_shape=jax.ShapeDtypeStruct((n_rows, w), x.dtype),
        scratch_shapes=[pltpu.SemaphoreType.DMA],
        input_output_aliases={2: 0},
        compiler_params=_cparams(("arbitrary",)),
        name="moe_dispatch_rows",
    )(pos_t, x, jnp.zeros((n_rows, w), x.dtype))


def _combine_kernel(yg_ref, gate_ref, r_ref, g_ref, b_ref, o_ref):
    acc = r_ref[...]
    gate = gate_ref[...]
    for j in range(TOPK):
        acc = acc + yg_ref[j].astype(F32) * gate[:, j:j + 1]
    mu = jnp.mean(acc, axis=-1, keepdims=True)
    yc = acc - mu
    var = jnp.mean(yc * yc, axis=-1, keepdims=True)
    o_ref[...] = yc * lax.rsqrt(var + LN_EPS) * g_ref[...] + b_ref[...]


def combine_ln(yg, gate, r, g, b, tm):
    t = r.shape[0]
    tm = min(tm, t)
    return pl.pallas_call(
        _combine_kernel,
        grid=(t // tm,),
        in_specs=[pl.BlockSpec((TOPK, tm, D), lambda i: (0, i, 0)), pl.BlockSpec((tm, LANES), lambda i: (i, 0)),
                  pl.BlockSpec((tm, D), lambda i: (i, 0)), pl.BlockSpec((1, D), lambda i: (0, 0)),
                  pl.BlockSpec((1, D), lambda i: (0, 0))],
        out_specs=pl.BlockSpec((tm, D), lambda i: (i, 0)),
        out_shape=jax.ShapeDtypeStruct((t, D), F32),
        compiler_params=_cparams(("parallel",)),
        name="moe_combine_ln",
    )(yg, gate, r, g.reshape(1, D), b.reshape(1, D))


def moe_finish(r, x1p, logits, lp, bm):
    t = r.shape[0]
    idx_t, gate, sel = _route(logits, lp['router_bias'])
    block_expert, n_used, pos_t = _dispatch(idx_t, sel, bm)
    xb = scatter_rows(x1p, pos_t, (t * TOPK // bm + NE) * bm)
    yb = moe_experts(xb, block_expert, n_used, lp['w_exp_gate'], lp['w_exp_up'], lp['w_exp_down'], bm, BF)
    yg = yb[pos_t.reshape(TOPK * t)].reshape(TOPK, t, D)
    return combine_ln(yg, gate, r, lp['ln2_g'], lp['ln2_b'], 256)


def _rms(x, g):
    return x * lax.rsqrt(jnp.mean(x * x, -1, keepdims=True) + RMS_EPS) * g


def _prep_layer(l, w_in, p):
    wi = w_in[l]
    small = jnp.concatenate([wi[:, R_DT:R_NQKV], wi[:, R_NB:R_GATE], jnp.zeros((D, LANES - 32), wi.dtype)], axis=1)
    lp = {k: v[l] for k, v in p.items()}
    lp.update(
        w_qkv=wi[:, R_Q:R_Z].astype(BF),
        w_zg=jnp.concatenate([wi[:, R_Z:R_XBC], wi[:, R_NZ:R_NB], wi[:, R_GATE:]], axis=1).astype(BF),
        w_xbc=wi[:, R_XBC:R_DT].astype(BF),
        w_nqkv=wi[:, R_NQKV:R_NZ].astype(BF),
        w_small=small.astype(BF),
        w_router_p=jnp.concatenate([lp['w_router'], jnp.zeros((D, LANES - NE), F32)], axis=1).astype(BF),
    )
    for k in ('w_attn_o', 'w_ssd_o', 'w_dn_o', 'w_out', 'w_sh_gate', 'w_sh_up', 'w_sh_down',
              'w_exp_gate', 'w_exp_up', 'w_exp_down'):
        lp[k] = lp[k].astype(BF)
    return lp


def _merge_weights(lp):
    return [lp['w_attn_o'], lp['w_ssd_o'], lp['w_dn_o'], lp['w_out'], lp['ln1_g'].reshape(1, D),
            lp['ln1_b'].reshape(1, D), lp['w_router_p'], lp['w_sh_gate'], lp['w_sh_up'], lp['w_sh_down']]


def prompt_layer(x, lp, layer_idx, tabs):
    lam_init = 0.8 - 0.6 * math.exp(-0.3 * layer_idx)
    q, kf, kb, vf, vb = qkv_project(x, lp['w_qkv'], tabs, 512)
    zg = matmul(x, lp['w_zg'], BF, tm=PROJ_ROWS)
    xbc = matmul(x, lp['w_xbc'], BF, tm=PROJ_ROWS)
    nqkv = matmul(x, lp['w_nqkv'], BF, tm=PROJ_ROWS)
    small = matmul(x, lp['w_small'], F32, tm=PROJ_ROWS)
    ao = flash_diff_attention(q, kb, vb, lp['attn_lambda'], lp['attn_subln'], lam_init, NB, SEQ, ATT_BLK)
    so, ssd_h, ssd_tail = ssd_prompt(xbc, zg, small, lp, NB, SEQ, SSD_CHUNK, 0)
    no, dn_s, dn_tail = gdn_prompt_staged(nqkv.reshape(NB, SEQ, NCONV), zg.reshape(NB, SEQ, -1),
                                          small.reshape(NB, SEQ, LANES), lp, NB, SEQ, GDN_CHUNK, 1)
    no = no.reshape(NB * SEQ, NH * NV)
    x1p, r, logits = merge_post(x, ao, so, no, zg, 2, _merge_weights(lp), 512)
    x2 = moe_finish(r, x1p, logits, lp, MOE_BLK_PROMPT)
    return (x2, kf.reshape(NB, SEQ, KVH, 2 * HD), vf.reshape(NB, SEQ, KVH, 2 * HD),
            ssd_h.reshape(NB, SH, SP, SN), ssd_tail, dn_s, dn_tail)


def _conv_step(hist, new, w, b=None):
    xp = jnp.concatenate([hist, new[:, None, :]], axis=1)
    y = jnp.sum(xp * w[None], axis=1)
    if b is not None:
        y = y + b
    return jax.nn.silu(y), xp[:, 1:]


def decode_layer(x, lp, layer_idx, tabs, cache_k, cache_v, page_table, ssd_h0, ssd_hist, dn_s0, dn_hist):
    lam_init = 0.8 - 0.6 * math.exp(-0.3 * layer_idx)
    q, kf, _, vf, _ = qkv_project(x, lp['w_qkv'], tabs, DB)
    zg = matmul(x, lp['w_zg'], F32, tm=DB)
    xbc_raw = matmul(x, lp['w_xbc'], F32, tm=DB)
    nqkv_raw = matmul(x, lp['w_nqkv'], F32, tm=DB)
    small = matmul(x, lp['w_small'], F32, tm=DB)
    qh = q.reshape(DB, KVH, 2, 2, HD)
    q16 = jnp.einsum('bhgcd,ce->bhgced', qh, jnp.eye(2, dtype=BF)).reshape(DB, 16, 2 * HD)
    k16 = jnp.repeat(kf.reshape(DB, KVH, 2 * HD), 4, axis=1)
    v16 = jnp.repeat(vf.reshape(DB, KVH, 2 * HD), 4, axis=1)
    oc = paged_attention(q16, k16, v16, cache_k, cache_v, page_table, layer_idx, PAGES_PER_STEP)
    oc = oc.reshape(DB, AH, 2, 2 * HD)
    lq = lp['attn_lambda']
    lam = jnp.exp(jnp.sum(lq[0] * lq[1])) - jnp.exp(jnp.sum(lq[2] * lq[3])) + lam_init
    o = oc[:, :, 0] - lam * oc[:, :, 1]
    ao = (_rms(o, lp['attn_subln']) * (1.0 - lam_init)).reshape(DB, AH * 2 * HD)
    xbc, ssd_tail = _conv_step(ssd_hist, xbc_raw, lp['ssd_conv_w'], lp['ssd_conv_b'])
    xs = xbc[:, :SI].reshape(DB, SH, SP)
    bm = jnp.repeat(xbc[:, SI:SI + SG * SN].reshape(DB, SG, SN), SH // SG, axis=1)
    cm = jnp.repeat(xbc[:, SI + SG * SN:].reshape(DB, SG, SN), SH // SG, axis=1)
    dt = jax.nn.softplus(small[:, L_DT:L_DT + SH] + lp['ssd_dt_bias'])
    a_coef = -jnp.exp(lp['ssd_a_log'])
    ssd_h = ssd_h0 * jnp.exp(dt * a_coef)[..., None, None] + (xs * dt[..., None])[..., :, None] * bm[..., None, :]
    ys = jnp.sum(ssd_h * cm[..., None, :], -1) + lp['ssd_d'][:, None] * xs
    gw = SI // SG
    ys = ys.reshape(DB, SG, gw) * jax.nn.silu(zg[:, :SI]).reshape(DB, SG, gw)
    so = _rms(ys, lp['ssd_norm'].reshape(SG, gw)).reshape(DB, SI)
    qkv, dn_tail = _conv_step(dn_hist, nqkv_raw, lp['dn_conv_w'])
    nq = qkv[:, :NKW].reshape(DB, NH, NK)
    nk = qkv[:, NKW:2 * NKW].reshape(DB, NH, NK)
    nv = qkv[:, 2 * NKW:].reshape(DB, NH, NV)
    nq = nq * lax.rsqrt(jnp.sum(nq * nq, -1, keepdims=True) + RMS_EPS) * (NK ** -0.5)
    nk = nk * lax.rsqrt(jnp.sum(nk * nk, -1, keepdims=True) + RMS_EPS)
    beta = jax.nn.sigmoid(small[:, L_NB:L_NB + NH])
    gl = -jnp.exp(lp['dn_a_log']) * jax.nn.softplus(small[:, L_NA:L_NA + NH] + lp['dn_dt_bias'])
    s = dn_s0 * jnp.exp(gl)[..., None, None]
    delta = (nv - jnp.sum(s * nk[..., :, None], -2)) * beta[..., None]
    dn_s = s + nk[..., :, None] * delta[..., None, :]
    on = jnp.sum(dn_s * nq[..., :, None], -2)
    no = (_rms(on, lp['dn_norm']) * jax.nn.silu(zg[:, SI:2 * SI].reshape(DB, NH, NV))).reshape(DB, NH * NV)
    x1p, r, logits = merge_post(x, ao, so, no, zg, 2, _merge_weights(lp), DB)
    x2 = moe_finish(r, x1p, logits, lp, MOE_BLK_DECODE)
    return (x2, kf.reshape(DB, 1, KVH, 2 * HD), vf.reshape(DB, 1, KVH, 2 * HD), ssd_h, ssd_tail, dn_s, dn_tail)


def kernel(x_prompt, x_sample, cache_k, cache_v, state_ssd, state_ssd_conv, state_dn, state_dn_conv, page_table,
           w_in, attn_lambda, attn_subln, w_attn_o, ssd_conv_w, ssd_conv_b, ssd_dt_bias, ssd_a_log, ssd_d, ssd_norm,
           w_ssd_o, dn_conv_w, dn_dt_bias, dn_a_log, dn_norm, w_dn_o, w_out, ln1_g, ln1_b, w_router, router_bias,
           w_exp_gate, w_exp_up, w_exp_down, w_sh_gate, w_sh_up, w_sh_down, ln2_g, ln2_b):
    params = dict(attn_lambda=attn_lambda, attn_subln=attn_subln, w_attn_o=w_attn_o, ssd_conv_w=ssd_conv_w,
                  ssd_conv_b=ssd_conv_b, ssd_dt_bias=ssd_dt_bias, ssd_a_log=ssd_a_log, ssd_d=ssd_d,
                  ssd_norm=ssd_norm, w_ssd_o=w_ssd_o, dn_conv_w=dn_conv_w, dn_dt_bias=dn_dt_bias,
                  dn_a_log=dn_a_log, dn_norm=dn_norm, w_dn_o=w_dn_o, w_out=w_out, ln1_g=ln1_g, ln1_b=ln1_b,
                  w_router=w_router, router_bias=router_bias, w_exp_gate=w_exp_gate, w_exp_up=w_exp_up,
                  w_exp_down=w_exp_down, w_sh_gate=w_sh_gate, w_sh_up=w_sh_up, w_sh_down=w_sh_down,
                  ln2_g=ln2_g, ln2_b=ln2_b)
    tabs_p = _rope_tables(jnp.arange(SEQ, dtype=jnp.int32))
    tabs_d = _rope_tables(jnp.full((DB,), PAST, dtype=jnp.int32))
    xp = x_prompt.reshape(NB * SEQ, D)
    xs = x_sample.reshape(DB, D)
    outs = [[] for _ in range(12)]
    for l in range(DEPTH):
        lp = _prep_layer(l, w_in, params)
        xp, kp, vp, hp, cp, sp, dp = prompt_layer(xp, lp, l, tabs_p)
        xs, ks, vs, hs, cs, ss, ds = decode_layer(xs, lp, l, tabs_d, cache_k, cache_v, page_table,
                                                  state_ssd[l], state_ssd_conv[l], state_dn[l], state_dn_conv[l])
        for lst, val in zip(outs, (kp, vp, ks, vs, hp, cp, hs, cs, sp, dp, ss, ds)):
            lst.append(val)
    return (xp.reshape(NB, SEQ, D), xs.reshape(DB, 1, D)) + tuple(jnp.stack(o) for o in outs)
```
